```python
import jax, jax.numpy as jnp
from jax import lax
import numpy as np

D_MODEL = 1024
BATCH = 4
SEQ = 8192
DEPTH = 1

HEAD_DIM = 64
N_Q_HEADS = 8
N_KV_HEADS = 2
WINDOW = 128
ATTN_BLOCK = 128
ROPE_THETA = 10000.0
M_HEADS = 4
M_QK_DIM = 64
M_V_DIM = 128
M_CHUNK = 128
CONV_WIDTH = 4
D_FF = -(-8 * D_MODEL // (3 * 256)) * 256
N_BRANCH = 2
EPS = 1e-6

ATTN_Q_W = N_Q_HEADS * HEAD_DIM
ATTN_KV_W = N_KV_HEADS * HEAD_DIM
M_QK_W = M_HEADS * M_QK_DIM
M_V_W = M_HEADS * M_V_DIM
IN_WIDTHS = (ATTN_Q_W, ATTN_KV_W, ATTN_KV_W, M_QK_W, M_QK_W, M_V_W, M_V_W, M_HEADS, M_HEADS, N_BRANCH * D_MODEL)
IN_WIDTH = sum(IN_WIDTHS)

kernel_name = 'hybrid_swa_sink_mlstm_gated_sandwich'


def rmsnorm(x, g):
    xf = x.astype(jnp.float32)
    y = xf * lax.rsqrt(jnp.mean(xf * xf, axis=-1, keepdims=True) + EPS)
    return y.astype(x.dtype) * g


def rope(x, pos):
    inv_freq = ROPE_THETA ** (-jnp.arange(0, HEAD_DIM, 2, dtype=jnp.float32) / HEAD_DIM)
    ang = pos.astype(jnp.float32)[:, None] * inv_freq[None, :]
    emb = jnp.concatenate([ang, ang], axis=-1)
    cos = jnp.cos(emb)[None, :, None, :].astype(x.dtype)
    sin = jnp.sin(emb)[None, :, None, :].astype(x.dtype)
    x1, x2 = jnp.split(x, 2, axis=-1)
    rot = jnp.concatenate([-x2, x1], axis=-1)
    return x * cos + rot * sin


def sliding_window_attention(q, k, v, sinks):
    B, S = q.shape[0], q.shape[1]
    blk = ATTN_BLOCK
    nb = S // blk
    G = N_Q_HEADS // N_KV_HEADS
    qb = q.reshape(B, nb, blk, N_KV_HEADS, G, HEAD_DIM)
    kb = k.reshape(B, nb, blk, N_KV_HEADS, HEAD_DIM)
    vb = v.reshape(B, nb, blk, N_KV_HEADS, HEAD_DIM)

    def with_prev(t):
        prev = jnp.pad(t, ((0, 0), (1, 0), (0, 0), (0, 0), (0, 0)))[:, :-1]
        return jnp.concatenate([prev, t], axis=2)

    kw = with_prev(kb)
    vw = with_prev(vb)
    scores = jnp.einsum('bnqhgd,bnkhd->bnhgqk', qb, kw).astype(jnp.float32) * (HEAD_DIM ** -0.5)
    qi = jnp.arange(blk)[:, None]
    kj = jnp.arange(2 * blk)[None, :]
    rel = qi + blk - kj
    band = (rel >= 0) & (rel < WINDOW)
    not_before_start = (jnp.arange(nb)[:, None, None] > 0) | (kj >= blk)[None]
    mask = band[None] & not_before_start
    scores = jnp.where(mask[None, :, None, None], scores, -jnp.inf)
    sink = jnp.broadcast_to(sinks.astype(jnp.float32).reshape(1, 1, N_KV_HEADS, G, 1, 1), scores.shape[:-1] + (1,))
    probs = jax.nn.softmax(jnp.concatenate([scores, sink], axis=-1), axis=-1)[..., :-1]
    out = jnp.einsum('bnhgqk,bnkhd->bnqhgd', probs.astype(v.dtype), vw)
    return out.reshape(B, S, N_Q_HEADS * HEAD_DIM)


def causal_depthwise_conv(x, w, b):
    C = x.shape[-1]
    y = lax.conv_general_dilated(x, w[:, None, :], window_strides=(1,), padding=[(CONV_WIDTH - 1, 0)],
                                 dimension_numbers=('NWC', 'WIO', 'NWC'), feature_group_count=C)
    return y + b


def mlstm_chunkwise(q, k, v, i_pre, f_pre):
    B, S = q.shape[0], q.shape[1]
    L = M_CHUNK
    nc = S // L

    def heads_first(t):
        t = t.reshape((B, nc, L) + t.shape[2:])
        return jnp.moveaxis(t, 3, 1)

    qc = heads_first(q)
    kc = heads_first(k) * (M_QK_DIM ** -0.5)
    vc = heads_first(v)
    ig = heads_first(i_pre.astype(jnp.float32))
    log_f = jax.nn.log_sigmoid(heads_first(f_pre.astype(jnp.float32)))
    b = jnp.cumsum(log_f, axis=-1)
    g = b[..., -1]
    causal = jnp.tril(jnp.ones((L, L), dtype=bool))
    log_d = jnp.where(causal, b[..., :, None] - b[..., None, :] + ig[..., None, :], -jnp.inf)

    log_end = g[..., None] - b + ig
    m_loc = jnp.max(log_end, axis=-1)
    kw = kc * jnp.exp(log_end - m_loc[..., None])[..., None]
    a_c = jnp.einsum('bhcsd,bhcse->bhcde', kw, vc)
    n_c = jnp.sum(kw, axis=-2)

    def step(carry, xs):
        C, n, m = carry
        a_i, n_i, g_i, ml_i = xs
        m_new = jnp.maximum(g_i + m, ml_i)
        decay = jnp.exp(g_i + m - m_new)
        scale = jnp.exp(ml_i - m_new)
        C_new = decay[..., None, None] * C + scale[..., None, None] * a_i
        n_new = decay[..., None] * n + scale[..., None] * n_i
        return (C_new, n_new, m_new), (C, n, m)

    init = (jnp.zeros((B, M_HEADS, M_QK_DIM, M_V_DIM), jnp.float32),
            jnp.zeros((B, M_HEADS, M_QK_DIM), jnp.float32),
            jnp.zeros((B, M_HEADS), jnp.float32))
    xs = (jnp.moveaxis(a_c, 2, 0), jnp.moveaxis(n_c, 2, 0), jnp.moveaxis(g, 2, 0), jnp.moveaxis(m_loc, 2, 0))
    _, (c_prev, n_prev, m_prev) = lax.scan(step, init, xs)
    c_prev = jnp.moveaxis(c_prev, 0, 2)
    n_prev = jnp.moveaxis(n_prev, 0, 2)
    m_prev = jnp.moveaxis(m_prev, 0, 2)

    log_inter = b + m_prev[..., None]
    m_t = jnp.maximum(log_inter, jnp.max(log_d, axis=-1))
    s_mat = jnp.einsum('bhctd,bhcsd->bhcts', qc, kc) * jnp.exp(log_d - m_t[..., None])
    inter = jnp.exp(log_inter - m_t)
    num = jnp.einsum('bhcts,bhcse->bhcte', s_mat, vc) + inter[..., None] * jnp.einsum('bhctd,bhcde->bhcte', qc, c_prev)
    den = jnp.sum(s_mat, axis=-1) + inter * jnp.einsum('bhctd,bhcd->bhct', qc, n_prev)
    h = num / jnp.maximum(jnp.abs(den), jnp.exp(-m_t))[..., None]
    return jnp.moveaxis(h, 1, 3).reshape(B, S, M_HEADS, M_V_DIM)


def head_rmsnorm(h, g):
    hf = h.astype(jnp.float32)
    y = hf * lax.rsqrt(jnp.mean(hf * hf, axis=-1, keepdims=True) + EPS)
    return y * g


def setup_inputs(seed: int = 0) -> dict:
    key = jax.random.key(seed)
    ks = jax.random.split(key, 18)
    f32 = jnp.float32

    def nrm(k, shape, scale):
        return jax.random.normal(k, shape, f32) * scale

    def gain(k, n):
        return 1.0 + nrm(k, (DEPTH, n), 0.02)

    b_fgate = jnp.linspace(3.0, 6.0, M_HEADS, dtype=f32)[None, :] + nrm(ks[10], (DEPTH, M_HEADS), 0.1)
    return {
        'x': nrm(ks[0], (BATCH, SEQ, D_MODEL), 1.0),
        'norm_pre_mix': gain(ks[1], D_MODEL),
        'norm_post_mix': gain(ks[2], D_MODEL),
        'norm_pre_ffn': gain(ks[3], D_MODEL),
        'norm_post_ffn': gain(ks[4], D_MODEL),
        'w_in': nrm(ks[5], (DEPTH, D_MODEL, IN_WIDTH), D_MODEL ** -0.5),
        'attn_sinks': nrm(ks[6], (DEPTH, N_Q_HEADS), 0.5),
        'conv_w': nrm(ks[7], (DEPTH, CONV_WIDTH, 2 * M_QK_W), CONV_WIDTH ** -0.5),
        'conv_b': nrm(ks[8], (DEPTH, 2 * M_QK_W), 0.02),
        'b_igate': nrm(ks[9], (DEPTH, M_HEADS), 0.1),
        'b_fgate': b_fgate,
        'mlstm_head_norm': gain(ks[11], M_V_W),
        'w_attn_branch': nrm(ks[12], (DEPTH, ATTN_Q_W, D_MODEL), ATTN_Q_W ** -0.5),
        'w_mlstm_branch': nrm(ks[13], (DEPTH, M_V_W, D_MODEL), M_V_W ** -0.5),
        'w_out': nrm(ks[14], (DEPTH, D_MODEL, D_MODEL), D_MODEL ** -0.5),
        'w_ffn_in': nrm(ks[15], (DEPTH, D_MODEL, 2 * D_FF), D_MODEL ** -0.5),
        'w_ffn_out': nrm(ks[16], (DEPTH, D_FF, D_MODEL), D_FF ** -0.5),
    }


def reference(x, norm_pre_mix, norm_post_mix, norm_pre_ffn, norm_post_ffn, w_in, attn_sinks, conv_w, conv_b,
              b_igate, b_fgate, mlstm_head_norm, w_attn_branch, w_mlstm_branch, w_out, w_ffn_in, w_ffn_out):
    B, S, _ = x.shape
    pos = jnp.arange(S)
    split_points = np.cumsum(IN_WIDTHS)[:-1].tolist()
    for l in range(DEPTH):
        h = rmsnorm(x, norm_pre_mix[l])
        proj = h @ w_in[l]
        aq, ak, av, mq, mk, mv, mo, mi, mf, gates = jnp.split(proj, split_points, axis=-1)

        aq = rope(aq.reshape(B, S, N_Q_HEADS, HEAD_DIM), pos)
        ak = rope(ak.reshape(B, S, N_KV_HEADS, HEAD_DIM), pos)
        av = av.reshape(B, S, N_KV_HEADS, HEAD_DIM)
        attn_out = sliding_window_attention(aq, ak, av, attn_sinks[l])

        mqk = jax.nn.silu(causal_depthwise_conv(jnp.concatenate([mq, mk], axis=-1), conv_w[l], conv_b[l]))
        mq, mk = jnp.split(mqk, 2, axis=-1)
        cell = mlstm_chunkwise(mq.reshape(B, S, M_HEADS, M_QK_DIM), mk.reshape(B, S, M_HEADS, M_QK_DIM),
                               mv.reshape(B, S, M_HEADS, M_V_DIM), mi + b_igate[l], mf + b_fgate[l])
        cell = head_rmsnorm(cell, mlstm_head_norm[l].reshape(M_HEADS, M_V_DIM)).astype(x.dtype)
        mlstm_out = jax.nn.sigmoid(mo) * cell.reshape(B, S, M_V_W)

        g_attn, g_mlstm = jnp.split(jax.nn.sigmoid(gates), N_BRANCH, axis=-1)
        merged = g_attn * (attn_out @ w_attn_branch[l]) + g_mlstm * (mlstm_out @ w_mlstm_branch[l])
        x = x + rmsnorm(merged @ w_out[l], norm_post_mix[l])

        h2 = rmsnorm(x, norm_pre_ffn[l])
        gate, up = jnp.split(h2 @ w_ffn_in[l], 2, axis=-1)
        y = (jax.nn.silu(gate) * up) @ w_ffn_out[l]
        x = x + rmsnorm(y, norm_post_ffn[l])
    return x
```

```python
import functools

import jax
import jax.numpy as jnp
from jax import lax
from jax.experimental import pallas as pl
from jax.experimental.pallas import tpu as pltpu

D_MODEL = 1024
HEAD_DIM = 64
N_Q_HEADS = 8
N_KV_HEADS = 2
WINDOW = 128
ATTN_BLOCK = 128
ROPE_THETA = 10000.0
M_HEADS = 4
M_QK_DIM = 64
M_V_DIM = 128
M_CHUNK = 128
CONV_WIDTH = 4
D_FF = 2816
EPS = 1e-6

ATTN_Q_W = N_Q_HEADS * HEAD_DIM
ATTN_KV_W = N_KV_HEADS * HEAD_DIM
M_QK_W = M_HEADS * M_QK_DIM
M_V_W = M_HEADS * M_V_DIM

LANES = 128
VMEM_LIMIT = 56 * 1024 * 1024

TOK_TILE = 512
ATTN_TILE = 1024
MLSTM_TILE = 512
FF_CHUNK = 512

F32 = jnp.float32
BF16 = jnp.bfloat16


def _dot(a, b):
    return jnp.dot(a, b, preferred_element_type=F32)


def _dot_nt(a, b):
    return lax.dot_general(a, b, (((1,), (1,)), ((), ())), preferred_element_type=F32)


def _dot_tn(a, b):
    return lax.dot_general(a, b, (((0,), (0,)), ((), ())), preferred_element_type=F32)


def _rms(x):
    return x * lax.rsqrt(jnp.mean(x * x, axis=-1, keepdims=True) + EPS)


def _split3(x):
    hi = x.astype(BF16)
    r1 = x - hi.astype(F32)
    mid = r1.astype(BF16)
    lo = (r1 - mid.astype(F32)).astype(BF16)
    return hi, mid, lo


def _inproj_kernel(x_ref, g_ref, cos_ref, sa_ref, sb_ref, wa_ref, wm_ref, wv_ref, wif_ref, wg_ref,
                   q_ref, k_ref, v_ref, mqk_ref, mv_ref, so_ref, gcol_ref, grow_ref, sg_ref):
    h = (_rms(x_ref[...]) * g_ref[...]).astype(BF16)
    cos = cos_ref[...]
    sa = sa_ref[...]
    sb = sb_ref[...]
    lane = lax.broadcasted_iota(jnp.int32, cos.shape, 1)
    lo = lane < HEAD_DIM

    def rope(t):
        return t * cos + pltpu.roll(t, LANES - HEAD_DIM // 2, 1) * sa + pltpu.roll(t, HEAD_DIM // 2, 1) * sb

    def dup_heads(t):
        r = pltpu.roll(t, HEAD_DIM, 1)
        return jnp.where(lo, t, r), jnp.where(lo, r, t)

    a = _dot(h, wa_ref[...])
    for j in range(ATTN_Q_W // LANES):
        q_ref[:, j * LANES:(j + 1) * LANES] = (rope(a[:, j * LANES:(j + 1) * LANES]) * (HEAD_DIM ** -0.5)).astype(BF16)
    k00, k11 = dup_heads(rope(a[:, ATTN_Q_W:ATTN_Q_W + LANES]))
    k_ref[:, :LANES] = k00.astype(BF16)
    k_ref[:, LANES:] = k11.astype(BF16)
    v00, v11 = dup_heads(a[:, ATTN_Q_W + LANES:ATTN_Q_W + 2 * LANES])
    v_ref[:, :LANES] = v00.astype(BF16)
    v_ref[:, LANES:] = v11.astype(BF16)

    mqk_ref[...] = _dot(h, wm_ref[...])
    vo = _dot(h, wv_ref[...])
    mv_ref[...] = vo[:, :M_V_W].astype(BF16)
    so_ref[...] = jax.nn.sigmoid(vo[:, M_V_W:]).astype(BF16)
    gi = _dot(h, wif_ref[...])
    gcol_ref[...] = gi[:, :2 * M_HEADS]
    grow_ref[...] = gi.T[:2 * M_HEADS, :]
    sg_ref[...] = jax.nn.sigmoid(_dot(h, wg_ref[...])).astype(BF16)


def _inproj(x2, g, cos, sa, sb, wa, wm, wv, wif, wg, seq):
    T = x2.shape[0]
    tm = TOK_TILE
    n_seq_tiles = seq // tm
    row = lambda i: (i, 0)
    pos = lambda i: (i % n_seq_tiles, 0)
    const = lambda i: (0, 0)

    def wspec(w):
        return pl.BlockSpec(w.shape, const, pipeline_mode=pl.Buffered(1))

    out_shape = (
        jax.ShapeDtypeStruct((T, ATTN_Q_W), BF16),
        jax.ShapeDtypeStruct((T, 2 * LANES), BF16),
        jax.ShapeDtypeStruct((T, 2 * LANES), BF16),
        jax.ShapeDtypeStruct((T, 2 * M_QK_W), F32),
        jax.ShapeDtypeStruct((T, M_V_W), BF16),
        jax.ShapeDtypeStruct((T, M_V_W), BF16),
        jax.ShapeDtypeStruct((T, 2 * M_HEADS), F32),
        jax.ShapeDtypeStruct((2 * M_HEADS, T), F32),
        jax.ShapeDtypeStruct((T, 2 * D_MODEL), BF16),
    )
    out_specs = (
        pl.BlockSpec((tm, ATTN_Q_W), row),
        pl.BlockSpec((tm, 2 * LANES), row),
        pl.BlockSpec((tm, 2 * LANES), row),
        pl.BlockSpec((tm, 2 * M_QK_W), row),
        pl.BlockSpec((tm, M_V_W), row),
        pl.BlockSpec((tm, M_V_W), row),
        pl.BlockSpec((tm, 2 * M_HEADS), row),
        pl.BlockSpec((2 * M_HEADS, tm), lambda i: (0, i)),
        pl.BlockSpec((tm, 2 * D_MODEL), row),
    )
    in_specs = [
        pl.BlockSpec((tm, D_MODEL), row),
        pl.BlockSpec((1, D_MODEL), const),
        pl.BlockSpec((tm, LANES), pos),
        pl.BlockSpec((tm, LANES), pos),
        pl.BlockSpec((tm, LANES), pos),
        wspec(wa), wspec(wm), wspec(wv), wspec(wif), wspec(wg),
    ]
    return pl.pallas_call(
        _inproj_kernel,
        grid=(T // tm,),
        in_specs=in_specs,
        out_specs=out_specs,
        out_shape=out_shape,
        compiler_params=pltpu.CompilerParams(dimension_semantics=("parallel",), vmem_limit_bytes=VMEM_LIMIT),
        name="inproj",
    )(x2, g, cos, sa, sb, wa, wm, wv, wif, wg)


def _attn_kernel(sink_ref, q_ref, kc_ref, kp_ref, vc_ref, vp_ref, o_ref):
    t = pl.program_id(1)
    blk = ATTN_BLOCK
    n_blk = q_ref.shape[0] // blk
    lo_kv = lax.broadcasted_iota(jnp.int32, (2 * blk, LANES), 1) < HEAD_DIM
    lo_q = lax.broadcasted_iota(jnp.int32, (blk, LANES), 1) < HEAD_DIM
    qi = lax.broadcasted_iota(jnp.int32, (blk, 2 * blk), 0)
    kj = lax.broadcasted_iota(jnp.int32, (blk, 2 * blk), 1)
    rel = qi + blk - kj
    band = (rel >= 0) & (rel < WINDOW)
    band_first = band & (kj >= jnp.where(t > 0, 0, blk))
    zero = jnp.zeros((), BF16)

    for n in range(n_blk):
        rows = slice(n * blk, (n + 1) * blk)
        if n == 0:
            k_prev, v_prev, mask = kp_ref[...], vp_ref[...], band_first
        else:
            prev = slice((n - 1) * blk, n * blk)
            k_prev, v_prev, mask = kc_ref[prev, :], vc_ref[prev, :], band
        k_win = jnp.concatenate([k_prev, kc_ref[rows, :]], axis=0)
        v_win = jnp.concatenate([v_prev, vc_ref[rows, :]], axis=0)
        for j in range(N_KV_HEADS):
            kk = k_win[:, j * LANES:(j + 1) * LANES]
            vv = v_win[:, j * LANES:(j + 1) * LANES]
            k_bd = jnp.concatenate([jnp.where(lo_kv, kk, zero), jnp.where(lo_kv, zero, kk)], axis=0)
            v_bd = jnp.concatenate([jnp.where(lo_kv, vv, zero), jnp.where(lo_kv, zero, vv)], axis=0)
            c0 = 2 * j * LANES
            q2 = jnp.concatenate([q_ref[rows, c0:c0 + LANES], q_ref[rows, c0 + LANES:c0 + 2 * LANES]], axis=0)
            s = _dot_nt(q2, k_bd)
            e_rows, r_rows = [], []
            for p in range(2):
                e_cols, r_cols = [], []
                for hh in range(2):
                    head = 4 * j + 2 * p + hh
                    sink = sink_ref[head]
                    sh = jnp.where(mask, s[p * blk:(p + 1) * blk, hh * 2 * blk:(hh + 1) * 2 * blk], -jnp.inf)
                    m = jnp.maximum(jnp.max(sh, axis=-1, keepdims=True), sink)
                    e = jnp.exp(sh - m)
                    den = jnp.sum(e, axis=-1, keepdims=True) + jnp.exp(sink - m)
                    e_cols.append(e.astype(BF16))
                    r_cols.append(1.0 / den)
                e_rows.append(jnp.concatenate(e_cols, axis=1))
                r_rows.append(jnp.where(lo_q, r_cols[0], r_cols[1]))
            o = _dot(jnp.concatenate(e_rows, axis=0), v_bd)
            for p in range(2):
                o_ref[rows, c0 + p * LANES:c0 + (p + 1) * LANES] = (o[p * blk:(p + 1) * blk] * r_rows[p]).astype(BF16)


def _attention(sinks, q, k, v, batch, seq):
    tq = ATTN_TILE
    nt = seq // tq
    per = tq // ATTN_BLOCK
    cur = lambda b, t: (b * nt + t, 0)
    prev = lambda b, t: (jnp.maximum((b * nt + t) * per - 1, 0), 0)
    return pl.pallas_call(
        _attn_kernel,
        grid=(batch, nt),
        in_specs=[
            pl.BlockSpec(memory_space=pltpu.SMEM),
            pl.BlockSpec((tq, ATTN_Q_W), cur),
            pl.BlockSpec((tq, 2 * LANES), cur),
            pl.BlockSpec((ATTN_BLOCK, 2 * LANES), prev),
            pl.BlockSpec((tq, 2 * LANES), cur),
            pl.BlockSpec((ATTN_BLOCK, 2 * LANES), prev),
        ],
        out_specs=pl.BlockSpec((tq, ATTN_Q_W), cur),
        out_shape=jax.ShapeDtypeStruct(q.shape, BF16),
        compiler_params=pltpu.CompilerParams(dimension_semantics=("parallel", "parallel"), vmem_limit_bytes=VMEM_LIMIT),
        name="swa_attention",
    )(sinks, q, k, k, v, v)


def _mlstm_kernel(mqk_ref, mv_ref, so_ref, gcol_ref, grow_ref, cw_ref, cb_ref, bcol_ref, brow_ref, hn_ref,
                  out_ref, cbuf, state, m_scr):
    t = pl.program_id(1)
    TL = mqk_ref.shape[0]
    L = M_CHUNK
    HALO = 8

    @pl.when(t == 0)
    def _():
        cbuf[0:HALO, :] = jnp.zeros((HALO, cbuf.shape[1]), F32)
        state[...] = jnp.zeros(state.shape, F32)
        m_scr[...] = jnp.zeros(m_scr.shape, F32)

    cbuf[HALO:HALO + TL, :] = mqk_ref[...]
    y = cb_ref[...]
    for j in range(CONV_WIDTH):
        off = HALO - (CONV_WIDTH - 1) + j
        y = y + cw_ref[j:j + 1, :] * cbuf[off:off + TL, :]
    cbuf[0:HALO, :] = cbuf[TL:TL + HALO, :]
    y = y * jax.nn.sigmoid(y)

    gc = gcol_ref[...] + bcol_ref[...]
    gr = grow_ref[...] + brow_ref[...]
    lf_c = jax.nn.log_sigmoid(gc)
    lf_r = jax.nn.log_sigmoid(gr)

    ri = lax.broadcasted_iota(jnp.int32, (L, L), 0)
    ci = lax.broadcasted_iota(jnp.int32, (L, L), 1)
    causal = ci <= ri
    tril = causal.astype(BF16)
    triu = (ri <= ci).astype(BF16)
    lo = lax.broadcasted_iota(jnp.int32, (L, LANES), 1) < M_QK_DIM
    top = lax.broadcasted_iota(jnp.int32, (L, 1), 0) < M_QK_DIM
    ones_v = jnp.ones((L, M_V_DIM), BF16)

    for c in range(TL // L):
        rows = slice(c * L, (c + 1) * L)
        b_col = sum(_dot(tril, part) for part in _split3(lf_c[rows, :]))
        b_row = sum(_dot(part, triu) for part in _split3(lf_r[:, rows]))
        for p in range(M_HEADS // 2):
            q_pair = y[rows, p * LANES:(p + 1) * LANES]
            k_pair = y[rows, M_QK_W + p * LANES:M_QK_W + (p + 1) * LANES] * (M_QK_DIM ** -0.5)
            c_prev = state[p]
            c_prev_b = c_prev.astype(BF16)
            a_sum = None
            decays, scales = [], []
            for hh in range(2):
                hd = 2 * p + hh
                ig_c = gc[rows, hd:hd + 1]
                ig_r = gr[hd:hd + 1, rows]
                bc = b_col[:, M_HEADS + hd:M_HEADS + hd + 1]
                br = b_row[M_HEADS + hd:M_HEADS + hd + 1, :]
                g = bc[L - 1:L, :]
                m_prev = m_scr[hd:hd + 1, 0:1]
                sel = lo if hh == 0 else jnp.logical_not(lo)
                q_h = jnp.where(sel, q_pair, 0.0).astype(BF16)
                k_h = jnp.where(sel, k_pair, 0.0)
                v_ext = jnp.concatenate([mv_ref[rows, hd * M_V_DIM:(hd + 1) * M_V_DIM], ones_v], axis=1)

                log_d = jnp.where(causal, bc - br + ig_r, -jnp.inf)
                log_inter = bc + m_prev
                m_t = jnp.maximum(log_inter, jnp.max(log_d, axis=-1, keepdims=True))
                s_mat = _dot_nt(q_h, k_h.astype(BF16)) * jnp.exp(log_d - m_t)
                inter = jnp.exp(log_inter - m_t)
                nd = _dot(s_mat.astype(BF16), v_ext) + inter * _dot(q_h, c_prev_b)
                num = nd[:, :M_V_DIM]
                den = nd[:, M_V_DIM:]
                hcell = num / jnp.maximum(jnp.abs(den), jnp.exp(-m_t))
                cols = slice(hd * M_V_DIM, (hd + 1) * M_V_DIM)
                cell = _rms(hcell) * hn_ref[:, cols]
                out_ref[rows, cols] = (so_ref[rows, cols].astype(F32) * cell).astype(BF16)

                log_end = g - bc + ig_c
                m_loc = jnp.max(log_end, axis=0, keepdims=True)
                kw = (k_h * jnp.exp(log_end - m_loc)).astype(BF16)
                a_h = _dot_tn(kw, v_ext)
                a_sum = a_h if a_sum is None else a_sum + a_h
                m_new = jnp.maximum(g + m_prev, m_loc)
                decays.append(jnp.exp(g + m_prev - m_new))
                scales.append(jnp.exp(m_loc - m_new))
                m_scr[hd:hd + 1, :] = jnp.broadcast_to(m_new, (1, LANES))
            decay = jnp.where(top, decays[0], decays[1])
            scale = jnp.where(top, scales[0], scales[1])
            state[p] = decay * c_prev + scale * a_sum


def _mlstm(mqk, mv, so, gcol, grow, cw, cb, bcol, brow, hn, batch, seq):
    tl = MLSTM_TILE
    nt = seq // tl
    cur = lambda b, t: (b * nt + t, 0)
    const = lambda b, t: (0, 0)
    return pl.pallas_call(
        _mlstm_kernel,
        grid=(batch, nt),
        in_specs=[
            pl.BlockSpec((tl, 2 * M_QK_W), cur),
            pl.BlockSpec((tl, M_V_W), cur),
            pl.BlockSpec((tl, M_V_W), cur),
            pl.BlockSpec((tl, 2 * M_HEADS), cur),
            pl.BlockSpec((2 * M_HEADS, tl), lambda b, t: (0, b * nt + t)),
            pl.BlockSpec(cw.shape, const),
            pl.BlockSpec(cb.shape, const),
            pl.BlockSpec(bcol.shape, const),
            pl.BlockSpec(brow.shape, const),
            pl.BlockSpec(hn.shape, const),
        ],
        out_specs=pl.BlockSpec((tl, M_V_W), cur),
        out_shape=jax.ShapeDtypeStruct(mv.shape, BF16),
        scratch_shapes=[
            pltpu.VMEM((8 + tl, 2 * M_QK_W), F32),
            pltpu.VMEM((M_HEADS // 2, 2 * M_QK_DIM, 2 * M_V_DIM), F32),
            pltpu.VMEM((8, LANES), F32),
        ],
        compiler_params=pltpu.CompilerParams(dimension_semantics=("parallel", "arbitrary"), vmem_limit_bytes=VMEM_LIMIT),
        name="mlstm",
    )(mqk, mv, so, gcol, grow, cw, cb, bcol, brow, hn)


def _merge_kernel(x_ref, a_ref, m_ref, sg_ref, wa_ref, wm_ref, wo_ref, g_ref, o_ref):
    merged = (sg_ref[:, :D_MODEL].astype(F32) * _dot(a_ref[...], wa_ref[...])
              + sg_ref[:, D_MODEL:].astype(F32) * _dot(m_ref[...], wm_ref[...]))
    y = _dot(merged.astype(BF16), wo_ref[...])
    o_ref[...] = x_ref[...] + _rms(y) * g_ref[...]


def _merge(x2, attn, ml, sg, wa, wm, wo, g):
    T = x2.shape[0]
    tm = TOK_TILE
    row = lambda i: (i, 0)
    const = lambda i: (0, 0)

    def wspec(w):
        return pl.BlockSpec(w.shape, const, pipeline_mode=pl.Buffered(1))

    return pl.pallas_call(
        _merge_kernel,
        grid=(T // tm,),
        in_specs=[
            pl.BlockSpec((tm, D_MODEL), row),
            pl.BlockSpec((tm, ATTN_Q_W), row),
            pl.BlockSpec((tm, M_V_W), row),
            pl.BlockSpec((tm, 2 * D_MODEL), row),
            wspec(wa), wspec(wm), wspec(wo),
            pl.BlockSpec((1, D_MODEL), const),
        ],
        out_specs=pl.BlockSpec((tm, D_MODEL), row),
        out_shape=jax.ShapeDtypeStruct(x2.shape, F32),
        compiler_params=pltpu.CompilerParams(dimension_semantics=("parallel",), vmem_limit_bytes=VMEM_LIMIT),
        name="merge_outproj",
    )(x2, attn, ml, sg, wa, wm, wo, g)


def _ffn_kernel(x_ref, gpre_ref, gpost_ref, wg_ref, wu_ref, wo_ref, o_ref, acc_ref):
    x = x_ref[...]
    h = (_rms(x) * gpre_ref[...]).astype(BF16)
    first = True
    for c0 in range(0, D_FF, FF_CHUNK):
        c1 = min(c0 + FF_CHUNK, D_FF)
        gate = _dot(h, wg_ref[:, c0:c1])
        up = _dot(h, wu_ref[:, c0:c1])
        act = (gate * jax.nn.sigmoid(gate) * up).astype(BF16)
        part = _dot(act, wo_ref[c0:c1, :])
        if first:
            acc_ref[...] = part
            first = False
        else:
            acc_ref[...] += part
    o_ref[...] = x + _rms(acc_ref[...]) * gpost_ref[...]


def _ffn(x1, gpre, gpost, wg, wu, wo):
    T = x1.shape[0]
    tm = TOK_TILE
    row = lambda i: (i, 0)
    const = lambda i: (0, 0)

    def wspec(w):
        return pl.BlockSpec(w.shape, const, pipeline_mode=pl.Buffered(1))

    return pl.pallas_call(
        _ffn_kernel,
        grid=(T // tm,),
        in_specs=[
            pl.BlockSpec((tm, D_MODEL), row),
            pl.BlockSpec((1, D_MODEL), const),
            pl.BlockSpec((1, D_MODEL), const),
            wspec(wg), wspec(wu), wspec(wo),
        ],
        out_specs=pl.BlockSpec((tm, D_MODEL), row),
        out_shape=jax.ShapeDtypeStruct(x1.shape, F32),
        scratch_shapes=[pltpu.VMEM((tm, D_MODEL), F32)],
        compiler_params=pltpu.CompilerParams(dimension_semantics=("parallel",), vmem_limit_bytes=VMEM_LIMIT),
        name="swiglu_ffn",
    )(x1, gpre, gpost, wg, wu, wo)


def _rope_tables(seq):
    inv_freq = ROPE_THETA ** (-jnp.arange(0, HEAD_DIM, 2, dtype=F32) / HEAD_DIM)
    ang = jnp.arange(seq).astype(F32)[:, None] * inv_freq[None, :]
    emb = jnp.concatenate([ang, ang], axis=-1)
    cos = jnp.cos(emb)
    sin = jnp.sin(emb)
    first_half = jnp.arange(HEAD_DIM) < HEAD_DIM // 2
    sin_a = jnp.where(first_half, -sin, 0.0)
    sin_b = jnp.where(first_half, 0.0, sin)
    rep = LANES // HEAD_DIM
    return jnp.tile(cos, (1, rep)), jnp.tile(sin_a, (1, rep)), jnp.tile(sin_b, (1, rep))


def _layer(x2, batch, seq, norm_pre_mix, norm_post_mix, norm_pre_ffn, norm_post_ffn, w_in, attn_sinks, conv_w,
           conv_b, b_igate, b_fgate, mlstm_head_norm, w_attn_branch, w_mlstm_branch, w_out, w_ffn_in, w_ffn_out):
    o_a = ATTN_Q_W + 2 * ATTN_KV_W
    o_m = o_a + 2 * M_QK_W
    o_v = o_m + 2 * M_V_W
    o_g = o_v + 2 * M_HEADS
    wa = w_in[:, :o_a].astype(BF16)
    wm = w_in[:, o_a:o_m].astype(BF16)
    wv = w_in[:, o_m:o_v].astype(BF16)
    wif = jnp.pad(w_in[:, o_v:o_g], ((0, 0), (0, LANES - 2 * M_HEADS))).astype(BF16)
    wg = w_in[:, o_g:].astype(BF16)
    cos, sa, sb = _rope_tables(seq)

    q, k, v, mqk, mv, so, gcol, grow, sg = _inproj(
        x2, norm_pre_mix[None, :], cos, sa, sb, wa, wm, wv, wif, wg, seq)
    attn = _attention(attn_sinks, q, k, v, batch, seq)
    bias = jnp.concatenate([b_igate, b_fgate])
    ml = _mlstm(mqk, mv, so, gcol, grow, conv_w, conv_b[None, :], bias[None, :], bias[:, None],
                mlstm_head_norm[None, :], batch, seq)
    x1 = _merge(x2, attn, ml, sg, w_attn_branch.astype(BF16), w_mlstm_branch.astype(BF16), w_out.astype(BF16),
                norm_post_mix[None, :])
    return _ffn(x1, norm_pre_ffn[None, :], norm_post_ffn[None, :], w_ffn_in[:, :D_FF].astype(BF16),
                w_ffn_in[:, D_FF:].astype(BF16), w_ffn_out.astype(BF16))


def kernel(x, norm_pre_mix, norm_post_mix, norm_pre_ffn, norm_post_ffn, w_in, attn_sinks, conv_w, conv_b, b_igate,
           b_fgate, mlstm_head_norm, w_attn_branch, w_mlstm_branch, w_out, w_ffn_in, w_ffn_out):
    B, S, D = x.shape
    x2 = x.reshape(B * S, D)
    for l in range(w_in.shape[0]):
        x2 = _layer(x2, B, S, norm_pre_mix[l], norm_post_mix[l], norm_pre_ffn[l], norm_post_ffn[l], w_in[l],
                    attn_sinks[l], conv_w[l], conv_b[l], b_igate[l], b_fgate[l], mlstm_head_norm[l],
                    w_attn_branch[l], w_mlstm_branch[l], w_out[l], w_ffn_in[l], w_ffn_out[l])
    return x2.reshape(B, S, D)
```

```python
import functools

import jax
import jax.numpy as jnp
from jax import lax
from jax.experimental import pallas as pl
from jax.experimental.pallas import tpu as pltpu

D_MODEL = 1024
HEAD_DIM = 64
N_Q_HEADS = 8
N_KV_HEADS = 2
WINDOW = 128
ATTN_BLOCK = 128
ROPE_THETA = 10000.0
M_HEADS = 4
M_QK_DIM = 64
M_V_DIM = 128
M_CHUNK = 128
CONV_WIDTH = 4
D_FF = 2816
EPS = 1e-6

ATTN_Q_W = N_Q_HEADS * HEAD_DIM
ATTN_KV_W = N_KV_HEADS * HEAD_DIM
M_QK_W = M_HEADS * M_QK_DIM
M_V_W = M_HEADS * M_V_DIM

LANES = 128
SUBLANES = 8
VMEM_LIMIT = 56 * 1024 * 1024
CONV_HALO = SUBLANES

TOK_TILE = 512
ATTN_TILE = 1024
MLSTM_TILE = 512
FF_CHUNK = 512

F32 = jnp.float32
BF16 = jnp.bfloat16


def _dot(a, b):
    return jnp.dot(a, b, preferred_element_type=F32)


def _dot_nt(a, b):
    return lax.dot_general(a, b, (((1,), (1,)), ((), ())), preferred_element_type=F32)


def _rms(x):
    return x * lax.rsqrt(jnp.mean(x * x, axis=-1, keepdims=True) + EPS)


def _split3(x):
    hi = x.astype(BF16)
    r1 = x - hi.astype(F32)
    mid = r1.astype(BF16)
    lo = (r1 - mid.astype(F32)).astype(BF16)
    return hi, mid, lo


def _inproj_kernel(x_ref, g_ref, cos_ref, sa_ref, sb_ref, cw_ref, cb_ref, wa_ref, wm_ref, wv_ref, wif_ref, wg_ref,
                   q_ref, k_ref, v_ref, qk_ref, mv_ref, so_ref, grow_ref, sg_ref, cbuf, *, n_seq_tiles):
    tm = x_ref.shape[0]
    h = (_rms(x_ref[...]) * g_ref[...]).astype(BF16)
    cos = cos_ref[...]
    sa = sa_ref[...]
    sb = sb_ref[...]
    lane = lax.broadcasted_iota(jnp.int32, cos.shape, 1)
    lo = lane < HEAD_DIM

    def rope(t):
        return t * cos + pltpu.roll(t, LANES - HEAD_DIM // 2, 1) * sa + pltpu.roll(t, HEAD_DIM // 2, 1) * sb

    def dup_heads(t):
        r = pltpu.roll(t, HEAD_DIM, 1)
        return jnp.where(lo, t, r), jnp.where(lo, r, t)

    a = _dot(h, wa_ref[...])
    for j in range(ATTN_Q_W // LANES):
        q_ref[:, j * LANES:(j + 1) * LANES] = (rope(a[:, j * LANES:(j + 1) * LANES]) * (HEAD_DIM ** -0.5)).astype(BF16)
    k00, k11 = dup_heads(rope(a[:, ATTN_Q_W:ATTN_Q_W + LANES]))
    k_ref[:, :LANES] = k00.astype(BF16)
    k_ref[:, LANES:] = k11.astype(BF16)
    v00, v11 = dup_heads(a[:, ATTN_Q_W + LANES:ATTN_Q_W + 2 * LANES])
    v_ref[:, :LANES] = v00.astype(BF16)
    v_ref[:, LANES:] = v11.astype(BF16)

    @pl.when(pl.program_id(0) % n_seq_tiles == 0)
    def _():
        cbuf[0:CONV_HALO, :] = jnp.zeros((CONV_HALO, cbuf.shape[1]), F32)

    cbuf[CONV_HALO:CONV_HALO + tm, :] = _dot(h, wm_ref[...])
    y = cb_ref[...]
    for j in range(CONV_WIDTH):
        off = CONV_HALO - (CONV_WIDTH - 1) + j
        y = y + cw_ref[j:j + 1, :] * cbuf[off:off + tm, :]
    cbuf[0:CONV_HALO, :] = cbuf[tm:tm + CONV_HALO, :]
    y = y * jax.nn.sigmoid(y)
    qk_ref[:, :M_QK_W] = y[:, :M_QK_W].astype(BF16)
    qk_ref[:, M_QK_W:] = (y[:, M_QK_W:] * (M_QK_DIM ** -0.5)).astype(BF16)

    vo = _dot(h, wv_ref[...])
    mv_ref[...] = vo[:, :M_V_W].astype(BF16)
    so_ref[...] = jax.nn.sigmoid(vo[:, M_V_W:]).astype(BF16)
    gi = _dot(h, wif_ref[...])
    grow_ref[...] = gi.T[:2 * M_HEADS, :]
    sg_ref[...] = jax.nn.sigmoid(_dot(h, wg_ref[...])).astype(BF16)


def _inproj(x2, g, cos, sa, sb, cw, cb, wa, wm, wv, wif, wg, seq):
    T = x2.shape[0]
    tm = TOK_TILE
    n_seq_tiles = seq // tm
    row = lambda i: (i, 0)
    pos = lambda i: (i % n_seq_tiles, 0)
    const = lambda i: (0, 0)

    def wspec(w):
        return pl.BlockSpec(w.shape, const, pipeline_mode=pl.Buffered(1))

    out_shape = (
        jax.ShapeDtypeStruct((T, ATTN_Q_W), BF16),
        jax.ShapeDtypeStruct((T, 2 * LANES), BF16),
        jax.ShapeDtypeStruct((T, 2 * LANES), BF16),
        jax.ShapeDtypeStruct((T, 2 * M_QK_W), BF16),
        jax.ShapeDtypeStruct((T, M_V_W), BF16),
        jax.ShapeDtypeStruct((T, M_V_W), BF16),
        jax.ShapeDtypeStruct((2 * M_HEADS, T), F32),
        jax.ShapeDtypeStruct((T, 2 * D_MODEL), BF16),
    )
    out_specs = (
        pl.BlockSpec((tm, ATTN_Q_W), row),
        pl.BlockSpec((tm, 2 * LANES), row),
        pl.BlockSpec((tm, 2 * LANES), row),
        pl.BlockSpec((tm, 2 * M_QK_W), row),
        pl.BlockSpec((tm, M_V_W), row),
        pl.BlockSpec((tm, M_V_W), row),
        pl.BlockSpec((2 * M_HEADS, tm), lambda i: (0, i)),
        pl.BlockSpec((tm, 2 * D_MODEL), row),
    )
    in_specs = [
        pl.BlockSpec((tm, D_MODEL), row),
        pl.BlockSpec((1, D_MODEL), const),
        pl.BlockSpec((tm, LANES), pos),
        pl.BlockSpec((tm, LANES), pos),
        pl.BlockSpec((tm, LANES), pos),
        pl.BlockSpec(cw.shape, const),
        pl.BlockSpec(cb.shape, const),
        wspec(wa), wspec(wm), wspec(wv), wspec(wif), wspec(wg),
    ]
    return pl.pallas_call(
        functools.partial(_inproj_kernel, n_seq_tiles=n_seq_tiles),
        grid=(T // tm,),
        in_specs=in_specs,
        out_specs=out_specs,
        out_shape=out_shape,
        scratch_shapes=[pltpu.VMEM((CONV_HALO + tm, 2 * M_QK_W), F32)],
        compiler_params=pltpu.CompilerParams(dimension_semantics=("arbitrary",), vmem_limit_bytes=VMEM_LIMIT),
        name="inproj",
    )(x2, g, cos, sa, sb, cw, cb, wa, wm, wv, wif, wg)


def _attn_kernel(sink_ref, q_ref, kc_ref, kp_ref, vc_ref, vp_ref, o_ref):
    t = pl.program_id(1)
    blk = ATTN_BLOCK
    n_blk = q_ref.shape[0] // blk
    lo_kv = lax.broadcasted_iota(jnp.int32, (2 * blk, LANES), 1) < HEAD_DIM
    lo_q = lax.broadcasted_iota(jnp.int32, (blk, LANES), 1) < HEAD_DIM
    qi = lax.broadcasted_iota(jnp.int32, (blk, 2 * blk), 0)
    kj = lax.broadcasted_iota(jnp.int32, (blk, 2 * blk), 1)
    rel = qi + blk - kj
    band = (rel >= 0) & (rel < WINDOW)
    band_first = band & (kj >= jnp.where(t > 0, 0, blk))
    zero = jnp.zeros((), BF16)

    for n in range(n_blk):
        rows = slice(n * blk, (n + 1) * blk)
        if n == 0:
            k_prev, v_prev, mask = kp_ref[...], vp_ref[...], band_first
        else:
            prev = slice((n - 1) * blk, n * blk)
            k_prev, v_prev, mask = kc_ref[prev, :], vc_ref[prev, :], band
        k_win = jnp.concatenate([k_prev, kc_ref[rows, :]], axis=0)
        v_win = jnp.concatenate([v_prev, vc_ref[rows, :]], axis=0)
        for j in range(N_KV_HEADS):
            kk = k_win[:, j * LANES:(j + 1) * LANES]
            vv = v_win[:, j * LANES:(j + 1) * LANES]
            k_bd = jnp.concatenate([jnp.where(lo_kv, kk, zero), jnp.where(lo_kv, zero, kk)], axis=0)
            v_bd = jnp.concatenate([jnp.where(lo_kv, vv, zero), jnp.where(lo_kv, zero, vv)], axis=0)
            c0 = 2 * j * LANES
            q2 = jnp.concatenate([q_ref[rows, c0:c0 + LANES], q_ref[rows, c0 + LANES:c0 + 2 * LANES]], axis=0)
            s = _dot_nt(q2, k_bd)
            e_rows, r_rows = [], []
            for p in range(2):
                e_cols, r_cols = [], []
                for hh in range(2):
                    head = 4 * j + 2 * p + hh
                    sink = sink_ref[head]
                    sh = jnp.where(mask, s[p * blk:(p + 1) * blk, hh * 2 * blk:(hh + 1) * 2 * blk], -jnp.inf)
                    m = jnp.maximum(jnp.max(sh, axis=-1, keepdims=True), sink)
                    e = jnp.exp(sh - m)
                    den = jnp.sum(e, axis=-1, keepdims=True) + jnp.exp(sink - m)
                    e_cols.append(e.astype(BF16))
                    r_cols.append(1.0 / den)
                e_rows.append(jnp.concatenate(e_cols, axis=1))
                r_rows.append(jnp.where(lo_q, r_cols[0], r_cols[1]))
            o = _dot(jnp.concatenate(e_rows, axis=0), v_bd)
            for p in range(2):
                o_ref[rows, c0 + p * LANES:c0 + (p + 1) * LANES] = (o[p * blk:(p + 1) * blk] * r_rows[p]).astype(BF16)


def _attention(sinks, q, k, v, batch, seq):
    tq = ATTN_TILE
    nt = seq // tq
    per = tq // ATTN_BLOCK
    cur = lambda b, t: (b * nt + t, 0)
    prev = lambda b, t: (jnp.maximum((b * nt + t) * per - 1, 0), 0)
    return pl.pallas_call(
        _attn_kernel,
        grid=(batch, nt),
        in_specs=[
            pl.BlockSpec(memory_space=pltpu.SMEM),
            pl.BlockSpec((tq, ATTN_Q_W), cur),
            pl.BlockSpec((tq, 2 * LANES), cur),
            pl.BlockSpec((ATTN_BLOCK, 2 * LANES), prev),
            pl.BlockSpec((tq, 2 * LANES), cur),
            pl.BlockSpec((ATTN_BLOCK, 2 * LANES), prev),
        ],
        out_specs=pl.BlockSpec((tq, ATTN_Q_W), cur),
        out_shape=jax.ShapeDtypeStruct(q.shape, BF16),
        compiler_params=pltpu.CompilerParams(dimension_semantics=("parallel", "parallel"), vmem_limit_bytes=VMEM_LIMIT),
        name="swa_attention",
    )(sinks, q, k, k, v, v)


def _mlstm_kernel(qk_ref, mv_ref, so_ref, grow_ref, brow_ref, hn_ref, out_ref, state, m_scr):
    t = pl.program_id(1)
    TL = qk_ref.shape[0]
    L = M_CHUNK
    R = 2 * M_HEADS

    @pl.when(t == 0)
    def _():
        state[...] = jnp.zeros(state.shape, F32)
        m_scr[...] = jnp.zeros(m_scr.shape, F32)

    gr = grow_ref[...] + brow_ref[...]
    lf = jax.nn.log_sigmoid(gr)

    ri = lax.broadcasted_iota(jnp.int32, (L, L), 0)
    ci = lax.broadcasted_iota(jnp.int32, (L, L), 1)
    causal = ci <= ri
    triu = (ri <= ci).astype(BF16)
    eye = (ri == ci).astype(BF16)
    eye3 = jnp.concatenate([eye, eye, eye], axis=1)
    lane_in_chunk = lax.broadcasted_iota(jnp.int32, (R, TL), 1) & (L - 1)
    lo_b = lax.broadcasted_iota(jnp.int32, (L, LANES), 1) < M_QK_DIM
    top = lax.broadcasted_iota(jnp.int32, (L, LANES), 0) < M_QK_DIM
    top2 = lax.broadcasted_iota(jnp.int32, (L, 2 * M_V_DIM), 0) < M_QK_DIM
    ones_v = jnp.ones((L, M_V_DIM), BF16)
    zero_b = jnp.zeros((), BF16)

    def row_bcast(x8, h, n=L):
        return jnp.broadcast_to(x8[h:h + 1, :], (n, x8.shape[1]))

    chunk_rows = [slice(c * L, (c + 1) * L) for c in range(TL // L)]
    b8 = jnp.concatenate([sum(_dot(part, triu) for part in _split3(lf[:, rows])) for rows in chunk_rows], axis=1)
    b_all = pltpu.roll(b8, M_HEADS, 0)
    r_all = gr - b_all
    cm_all = r_all
    sh = 1
    while sh < L:
        cm_all = jnp.where(lane_in_chunk >= sh, jnp.maximum(cm_all, pltpu.roll(cm_all, sh, 1)), cm_all)
        sh *= 2

    m = m_scr[...]
    gates = []
    for rows in chunk_rows:
        b, r, cm = b_all[:, rows], r_all[:, rows], cm_all[:, rows]
        rmax = jnp.broadcast_to(cm[:, L - 1:L], (R, L))
        g = jnp.broadcast_to(b[:, L - 1:L], (R, L))
        big_m = jnp.maximum(m, cm)
        inter = jnp.exp(m - big_m)
        emt = jnp.exp(-(b + big_m))
        w = jnp.exp(r - rmax)
        m_loc = g + rmax
        m_new = jnp.maximum(g + m, m_loc)
        decay = jnp.exp(g + m - m_new)
        scale = jnp.exp(m_loc - m_new)
        m = m_new
        gates.append((r, w, decay, scale, _split3(big_m), _split3(inter), _split3(emt)))
    m_scr[...] = m

    for rows, (r, w, decay, scale, split_m, split_i, split_e) in zip(chunk_rows, gates):
        for p in range(M_HEADS // 2):
            pc = slice(p * LANES, (p + 1) * LANES)
            q_pair_b = qk_ref[rows, pc]
            q_pair = q_pair_b.astype(F32)
            k_pair_b = qk_ref[rows, M_QK_W + p * LANES:M_QK_W + (p + 1) * LANES]
            c_prev = state[p]
            c_prev_b = c_prev.astype(BF16)
            v_exts = []
            for hh in range(2):
                hd = 2 * p + hh
                cols = slice(hd * M_V_DIM, (hd + 1) * M_V_DIM)
                sel = lo_b if hh == 0 else jnp.logical_not(lo_b)
                v_ext = jnp.concatenate([mv_ref[rows, cols], ones_v], axis=1)
                v_exts.append(v_ext)
                rhs_t = jnp.concatenate(
                    [jnp.concatenate([row_bcast(part, hd) for part in parts], axis=1)
                     for parts in (split_m, split_i, split_e)], axis=0)
                colsq = _dot_nt(eye3, rhs_t)
                m_col, inter_col, emt_col = colsq[:, :L], colsq[:, L:2 * L], colsq[:, 2 * L:]

                d_mat = jnp.exp(jnp.where(causal, row_bcast(r, hd), -jnp.inf) - m_col)
                k_h = jnp.where(sel, k_pair_b, zero_b)
                s_mat = (_dot_nt(q_pair_b, k_h) * d_mat).astype(BF16)
                qs = jnp.where(sel, (q_pair * inter_col).astype(BF16), zero_b)
                nd = _dot(jnp.concatenate([s_mat, qs], axis=1), jnp.concatenate([v_ext, c_prev_b], axis=0))
                hcell = nd[:, :M_V_DIM] / jnp.maximum(jnp.abs(nd[:, M_V_DIM:]), emt_col)
                cell = _rms(hcell) * hn_ref[:, cols]
                out_ref[rows, cols] = (so_ref[rows, cols].astype(F32) * cell).astype(BF16)

            k_t = k_pair_b.astype(F32).T
            w_s = jnp.where(top, row_bcast(w, 2 * p), row_bcast(w, 2 * p + 1))
            kw_t = (k_t * w_s).astype(BF16)
            lhs = jnp.concatenate([jnp.where(top, kw_t, zero_b), jnp.where(top, zero_b, kw_t)], axis=1)
            a = _dot(lhs, jnp.concatenate(v_exts, axis=0))
            dec = jnp.where(top2, row_bcast(decay, 2 * p, L)[:, :1], row_bcast(decay, 2 * p + 1, L)[:, :1])
            sc = jnp.where(top2, row_bcast(scale, 2 * p, L)[:, :1], row_bcast(scale, 2 * p + 1, L)[:, :1])
            state[p] = dec * c_prev + sc * a


def _mlstm(qk, mv, so, grow, brow, hn, batch, seq):
    tl = MLSTM_TILE
    nt = seq // tl
    cur = lambda b, t: (b * nt + t, 0)
    const = lambda b, t: (0, 0)
    return pl.pallas_call(
        _mlstm_kernel,
        grid=(batch, nt),
        in_specs=[
            pl.BlockSpec((tl, 2 * M_QK_W), cur),
            pl.BlockSpec((tl, M_V_W), cur),
            pl.BlockSpec((tl, M_V_W), cur),
            pl.BlockSpec((2 * M_HEADS, tl), lambda b, t: (0, b * nt + t)),
            pl.BlockSpec(brow.shape, const),
            pl.BlockSpec(hn.shape, const),
        ],
        out_specs=pl.BlockSpec((tl, M_V_W), cur),
        out_shape=jax.ShapeDtypeStruct(mv.shape, BF16),
        scratch_shapes=[
            pltpu.VMEM((M_HEADS // 2, 2 * M_QK_DIM, 2 * M_V_DIM), F32),
            pltpu.VMEM((2 * M_HEADS, M_CHUNK), F32),
        ],
        compiler_params=pltpu.CompilerParams(dimension_semantics=("parallel", "arbitrary"), vmem_limit_bytes=VMEM_LIMIT),
        name="mlstm",
    )(qk, mv, so, grow, brow, hn)


def _merge_kernel(x_ref, a_ref, m_ref, sg_ref, wa_ref, wm_ref, wo_ref, g_ref, o_ref):
    merged = (sg_ref[:, :D_MODEL].astype(F32) * _dot(a_ref[...], wa_ref[...])
              + sg_ref[:, D_MODEL:].astype(F32) * _dot(m_ref[...], wm_ref[...]))
    y = _dot(merged.astype(BF16), wo_ref[...])
    o_ref[...] = x_ref[...] + _rms(y) * g_ref[...]


def _merge(x2, attn, ml, sg, wa, wm, wo, g):
    T = x2.shape[0]
    tm = TOK_TILE
    row = lambda i: (i, 0)
    const = lambda i: (0, 0)

    def wspec(w):
        return pl.BlockSpec(w.shape, const, pipeline_mode=pl.Buffered(1))

    return pl.pallas_call(
        _merge_kernel,
        grid=(T // tm,),
        in_specs=[
            pl.BlockSpec((tm, D_MODEL), row),
            pl.BlockSpec((tm, ATTN_Q_W), row),
            pl.BlockSpec((tm, M_V_W), row),
            pl.BlockSpec((tm, 2 * D_MODEL), row),
            wspec(wa), wspec(wm), wspec(wo),
            pl.BlockSpec((1, D_MODEL), const),
        ],
        out_specs=pl.BlockSpec((tm, D_MODEL), row),
        out_shape=jax.ShapeDtypeStruct(x2.shape, F32),
        compiler_params=pltpu.CompilerParams(dimension_semantics=("parallel",), vmem_limit_bytes=VMEM_LIMIT),
        name="merge_outproj",
    )(x2, attn, ml, sg, wa, wm, wo, g)


def _ffn_kernel(x_ref, gpre_ref, gpost_ref, wg_ref, wu_ref, wo_ref, o_ref, acc_ref):
    x = x_ref[...]
    h = (_rms(x) * gpre_ref[...]).astype(BF16)
    first = True
    for c0 in range(0, D_FF, FF_CHUNK):
        c1 = min(c0 + FF_CHUNK, D_FF)
        gate = _dot(h, wg_ref[:, c0:c1])
        up = _dot(h, wu_ref[:, c0:c1])
        act = (gate * jax.nn.sigmoid(gate) * up).astype(BF16)
        part = _dot(act, wo_ref[c0:c1, :])
        if first:
            acc_ref[...] = part
            first = False
        else:
            acc_ref[...] += part
    o_ref[...] = x + _rms(acc_ref[...]) * gpost_ref[...]


def _ffn(x1, gpre, gpost, wg, wu, wo):
    T = x1.shape[0]
    tm = TOK_TILE
    row = lambda i: (i, 0)
    const = lambda i: (0, 0)

    def wspec(w):
        return pl.BlockSpec(w.shape, const, pipeline_mode=pl.Buffered(1))

    return pl.pallas_call(
        _ffn_kernel,
        grid=(T // tm,),
        in_specs=[
            pl.BlockSpec((tm, D_MODEL), row),
            pl.BlockSpec((1, D_MODEL), const),
            pl.BlockSpec((1, D_MODEL), const),
            wspec(wg), wspec(wu), wspec(wo),
        ],
        out_specs=pl.BlockSpec((tm, D_MODEL), row),
        out_shape=jax.ShapeDtypeStruct(x1.shape, F32),
        scratch_shapes=[pltpu.VMEM((tm, D_MODEL), F32)],
        compiler_params=pltpu.CompilerParams(dimension_semantics=("parallel",), vmem_limit_bytes=VMEM_LIMIT),
        name="swiglu_ffn",
    )(x1, gpre, gpost, wg, wu, wo)


def _rope_tables(seq):
    inv_freq = ROPE_THETA ** (-jnp.arange(0, HEAD_DIM, 2, dtype=F32) / HEAD_DIM)
    ang = jnp.arange(seq).astype(F32)[:, None] * inv_freq[None, :]
    emb = jnp.concatenate([ang, ang], axis=-1)
    cos = jnp.cos(emb)
    sin = jnp.sin(emb)
    first_half = jnp.arange(HEAD_DIM) < HEAD_DIM // 2
    sin_a = jnp.where(first_half, -sin, 0.0)
    sin_b = jnp.where(first_half, 0.0, sin)
    rep = LANES // HEAD_DIM
    return jnp.tile(cos, (1, rep)), jnp.tile(sin_a, (1, rep)), jnp.tile(sin_b, (1, rep))


def _layer(x2, batch, seq, norm_pre_mix, norm_post_mix, norm_pre_ffn, norm_post_ffn, w_in, attn_sinks, conv_w,
           conv_b, b_igate, b_fgate, mlstm_head_norm, w_attn_branch, w_mlstm_branch, w_out, w_ffn_in, w_ffn_out):
    o_a = ATTN_Q_W + 2 * ATTN_KV_W
    o_m = o_a + 2 * M_QK_W
    o_v = o_m + 2 * M_V_W
    o_g = o_v + 2 * M_HEADS
    wa = w_in[:, :o_a].astype(BF16)
    wm = w_in[:, o_a:o_m].astype(BF16)
    wv = w_in[:, o_m:o_v].astype(BF16)
    wif = jnp.pad(w_in[:, o_v:o_g], ((0, 0), (0, LANES - 2 * M_HEADS))).astype(BF16)
    wg = w_in[:, o_g:].astype(BF16)
    cos, sa, sb = _rope_tables(seq)

    q, k, v, qk, mv, so, grow, sg = _inproj(
        x2, norm_pre_mix[None, :], cos, sa, sb, conv_w, conv_b[None, :], wa, wm, wv, wif, wg, seq)
    attn = _attention(attn_sinks, q, k, v, batch, seq)
    bias = jnp.concatenate([b_igate, b_fgate])
    ml = _mlstm(qk, mv, so, grow, bias[:, None], mlstm_head_norm[None, :], batch, seq)
    x1 = _merge(x2, attn, ml, sg, w_attn_branch.astype(BF16), w_mlstm_branch.astype(BF16), w_out.astype(BF16),
                norm_post_mix[None, :])
    return _ffn(x1, norm_pre_ffn[None, :], norm_post_ffn[None, :], w_ffn_in[:, :D_FF].astype(BF16),
                w_ffn_in[:, D_FF:].astype(BF16), w_ffn_out.astype(BF16))


def kernel(x, norm_pre_mix, norm_post_mix, norm_pre_ffn, norm_post_ffn, w_in, attn_sinks, conv_w, conv_b, b_igate,
           b_fgate, mlstm_head_norm, w_attn_branch, w_mlstm_branch, w_out, w_ffn_in, w_ffn_out):
    B, S, D = x.shape
    x2 = x.reshape(B * S, D)
    for l in range(w_in.shape[0]):
        x2 = _layer(x2, B, S, norm_pre_mix[l], norm_post_mix[l], norm_pre_ffn[l], norm_post_ffn[l], w_in[l],
                    attn_sinks[l], conv_w[l], conv_b[l], b_igate[l], b_fgate[l], mlstm_head_norm[l],
                    w_attn_branch[l], w_mlstm_branch[l], w_out[l], w_ffn_in[l], w_ffn_out[l])
    return x2.reshape(B, S, D)
```

```python
import functools

import jax
import jax.numpy as jnp
from jax import lax
from jax.experimental import pallas as pl
from jax.experimental.pallas import tpu as pltpu

D_MODEL = 1024
HEAD_DIM = 64
N_Q_HEADS = 8
N_KV_HEADS = 2
WINDOW = 128
ATTN_BLOCK = 128
ROPE_THETA = 10000.0
M_HEADS = 4
M_QK_DIM = 64
M_V_DIM = 128
M_CHUNK = 128
CONV_WIDTH = 4
D_FF = 2816
EPS = 1e-6

ATTN_Q_W = N_Q_HEADS * HEAD_DIM
ATTN_KV_W = N_KV_HEADS * HEAD_DIM
M_QK_W = M_HEADS * M_QK_DIM
M_V_W = M_HEADS * M_V_DIM
KV_BD_W = 2 * ATTN_KV_W * 2

LANES = 128
SUBLANES = 8
VMEM_LIMIT = 56 * 1024 * 1024
CONV_HALO = SUBLANES

TOK_TILE = 512
ATTN_TILE = 1024
MLSTM_TILE = 1024
FF_CHUNK = 512
MERGE_CHUNK = 256

F32 = jnp.float32
BF16 = jnp.bfloat16


def _dot(a, b):
    return jnp.dot(a, b, preferred_element_type=F32)


def _dot_nt(a, b):
    return lax.dot_general(a, b, (((1,), (1,)), ((), ())), preferred_element_type=F32)


def _rms(x):
    return x * lax.rsqrt(jnp.mean(x * x, axis=-1, keepdims=True) + EPS)


def _split2(x):
    hi = x.astype(BF16)
    return hi, (x - hi.astype(F32)).astype(BF16)


def _split3(x):
    hi = x.astype(BF16)
    r1 = x - hi.astype(F32)
    mid = r1.astype(BF16)
    lo = (r1 - mid.astype(F32)).astype(BF16)
    return hi, mid, lo


def _inproj_kernel(x_ref, g_ref, cos_ref, sa_ref, sb_ref, cw_ref, cb_ref, wa_ref, wm_ref, wv_ref, wg_ref,
                   q_ref, k_ref, v_ref, qk_ref, mv_ref, so_ref, grow_ref, sg_ref, cbuf, *, n_seq_tiles):
    tm = x_ref.shape[0]
    h = (_rms(x_ref[...]) * g_ref[...]).astype(BF16)
    cos = cos_ref[...]
    sa = sa_ref[...]
    sb = sb_ref[...]
    lane = lax.broadcasted_iota(jnp.int32, cos.shape, 1)
    lo = lane < HEAD_DIM

    def rope(t):
        return t * cos + pltpu.roll(t, LANES - HEAD_DIM // 2, 1) * sa + pltpu.roll(t, HEAD_DIM // 2, 1) * sb

    def store_block_diag(ref, t):
        tb = t.astype(BF16)
        rb = pltpu.roll(t, HEAD_DIM, 1).astype(BF16)
        zero = jnp.zeros((), BF16)
        for i, (src, keep_lo) in enumerate(((tb, True), (rb, False), (rb, True), (tb, False))):
            ref[:, i * LANES:(i + 1) * LANES] = jnp.where(lo, src, zero) if keep_lo else jnp.where(lo, zero, src)

    def attn_proj():
        a = _dot(h, wa_ref[...])
        grow_ref[...] = a[:, ATTN_Q_W + 2 * LANES:].T[:2 * M_HEADS, :]
        for j in range(ATTN_Q_W // LANES):
            q_ref[:, j * LANES:(j + 1) * LANES] = (
                rope(a[:, j * LANES:(j + 1) * LANES]) * (HEAD_DIM ** -0.5)).astype(BF16)
        store_block_diag(k_ref, rope(a[:, ATTN_Q_W:ATTN_Q_W + LANES]))
        store_block_diag(v_ref, a[:, ATTN_Q_W + LANES:ATTN_Q_W + 2 * LANES])

    @pl.when(pl.program_id(0) % n_seq_tiles == 0)
    def _():
        cbuf[0:CONV_HALO, :] = jnp.zeros((CONV_HALO, cbuf.shape[1]), F32)

    def mqk_proj(c0, c1):
        cbuf[CONV_HALO:CONV_HALO + tm, c0:c1] = _dot(h, wm_ref[:, c0:c1])

    def gates_proj(c0, c1):
        sg_ref[:, c0:c1] = jax.nn.sigmoid(_dot(h, wg_ref[:, c0:c1])).astype(BF16)

    def conv_chunk(j0):
        cols = slice(j0 * LANES, (j0 + 1) * LANES)
        y = cb_ref[:, cols]
        for j in range(CONV_WIDTH):
            off = CONV_HALO - (CONV_WIDTH - 1) + j
            y = y + cw_ref[j:j + 1, cols] * cbuf[off:off + tm, cols]
        cbuf[0:CONV_HALO, cols] = cbuf[tm:tm + CONV_HALO, cols]
        y = y * jax.nn.sigmoid(y)
        if j0 * LANES >= M_QK_W:
            y = y * (M_QK_DIM ** -0.5)
        qk_ref[:, cols] = y.astype(BF16)

    def mlstm_vo_proj():
        vo = _dot(h, wv_ref[...])
        mv_ref[...] = vo[:, :M_V_W].astype(BF16)
        so_ref[...] = jax.nn.sigmoid(vo[:, M_V_W:]).astype(BF16)

    mqk_proj(0, M_QK_W)
    gates_proj(0, D_MODEL // 2)
    conv_chunk(0)
    conv_chunk(1)
    attn_proj()
    mqk_proj(M_QK_W, 2 * M_QK_W)
    gates_proj(D_MODEL // 2, D_MODEL)
    conv_chunk(2)
    conv_chunk(3)
    mlstm_vo_proj()
    gates_proj(D_MODEL, 2 * D_MODEL)


def _inproj(x2, g, cos, sa, sb, cw, cb, wa, wm, wv, wg, seq):
    T = x2.shape[0]
    tm = TOK_TILE
    n_seq_tiles = seq // tm
    row = lambda i: (i, 0)
    pos = lambda i: (i % n_seq_tiles, 0)
    const = lambda i: (0, 0)

    def wspec(w):
        return pl.BlockSpec(w.shape, const, pipeline_mode=pl.Buffered(1))

    out_shape = (
        jax.ShapeDtypeStruct((T, ATTN_Q_W), BF16),
        jax.ShapeDtypeStruct((T, KV_BD_W), BF16),
        jax.ShapeDtypeStruct((T, KV_BD_W), BF16),
        jax.ShapeDtypeStruct((T, 2 * M_QK_W), BF16),
        jax.ShapeDtypeStruct((T, M_V_W), BF16),
        jax.ShapeDtypeStruct((T, M_V_W), BF16),
        jax.ShapeDtypeStruct((2 * M_HEADS, T), F32),
        jax.ShapeDtypeStruct((T, 2 * D_MODEL), BF16),
    )
    out_specs = (
        pl.BlockSpec((tm, ATTN_Q_W), row),
        pl.BlockSpec((tm, KV_BD_W), row),
        pl.BlockSpec((tm, KV_BD_W), row),
        pl.BlockSpec((tm, 2 * M_QK_W), row),
        pl.BlockSpec((tm, M_V_W), row),
        pl.BlockSpec((tm, M_V_W), row),
        pl.BlockSpec((2 * M_HEADS, tm), lambda i: (0, i)),
        pl.BlockSpec((tm, 2 * D_MODEL), row),
    )
    in_specs = [
        pl.BlockSpec((tm, D_MODEL), row),
        pl.BlockSpec((1, D_MODEL), const),
        pl.BlockSpec((tm, LANES), pos),
        pl.BlockSpec((tm, LANES), pos),
        pl.BlockSpec((tm, LANES), pos),
        pl.BlockSpec(cw.shape, const),
        pl.BlockSpec(cb.shape, const),
        wspec(wa), wspec(wm), wspec(wv), wspec(wg),
    ]
    return pl.pallas_call(
        functools.partial(_inproj_kernel, n_seq_tiles=n_seq_tiles),
        grid=(T // tm,),
        in_specs=in_specs,
        out_specs=out_specs,
        out_shape=out_shape,
        scratch_shapes=[pltpu.VMEM((CONV_HALO + tm, 2 * M_QK_W), F32)],
        compiler_params=pltpu.CompilerParams(dimension_semantics=("arbitrary",), vmem_limit_bytes=VMEM_LIMIT),
        name="inproj",
    )(x2, g, cos, sa, sb, cw, cb, wa, wm, wv, wg)


def _attn_kernel(sink_ref, q_ref, kc_ref, kp_ref, vc_ref, vp_ref, o_ref):
    t = pl.program_id(1)
    blk = ATTN_BLOCK
    n_blk = q_ref.shape[0] // blk
    lo_q = lax.broadcasted_iota(jnp.int32, (blk, LANES), 1) < HEAD_DIM
    ri = lax.broadcasted_iota(jnp.int32, (blk, blk), 0)
    ci = lax.broadcasted_iota(jnp.int32, (blk, blk), 1)
    own = ci <= ri
    prev_ok = ci > ri + jnp.where(t > 0, 0, blk)
    zero = jnp.zeros((), BF16)

    for n in range(n_blk):
        rows = slice(n * blk, (n + 1) * blk)
        k_prev = kp_ref if n == 0 else kc_ref.at[(n - 1) * blk:n * blk, :]
        v_prev = vp_ref if n == 0 else vc_ref.at[(n - 1) * blk:n * blk, :]
        for j in range(N_KV_HEADS):
            kv_cols = [slice((2 * j + i) * LANES, (2 * j + i + 1) * LANES) for i in range(2)]
            k_bd = jnp.concatenate([x for c in kv_cols for x in (k_prev[:, c], kc_ref[rows, c])], axis=0)
            v_bd = jnp.concatenate([x for c in kv_cols for x in (v_prev[:, c], vc_ref[rows, c])], axis=0)
            c0 = 2 * j * LANES
            q2 = jnp.concatenate([q_ref[rows, c0:c0 + LANES], q_ref[rows, c0 + LANES:c0 + 2 * LANES]], axis=0)
            s = _dot_nt(q2, k_bd)
            e_rows, r_rows = [], []
            for p in range(2):
                e_cols, r_cols = [], []
                for hh in range(2):
                    sink = sink_ref[4 * j + 2 * p + hh]
                    s_prev = s[p * blk:(p + 1) * blk, 2 * hh * blk:(2 * hh + 1) * blk]
                    s_own = s[p * blk:(p + 1) * blk, (2 * hh + 1) * blk:(2 * hh + 2) * blk]
                    if n == 0:
                        s_prev = jnp.where(prev_ok, s_prev, -jnp.inf)
                    sc = jnp.where(own, s_own, s_prev)
                    m = jnp.maximum(jnp.max(sc, axis=-1, keepdims=True), sink)
                    e = jnp.exp(sc - m)
                    den = jnp.sum(e, axis=-1, keepdims=True) + jnp.exp(sink - m)
                    eb = e.astype(BF16)
                    e_cols += [jnp.where(own, zero, eb), jnp.where(own, eb, zero)]
                    r_cols.append(1.0 / den)
                e_rows.append(jnp.concatenate(e_cols, axis=1))
                r_rows.append(jnp.where(lo_q, r_cols[0], r_cols[1]))
            o = _dot(jnp.concatenate(e_rows, axis=0), v_bd)
            for p in range(2):
                o_ref[rows, c0 + p * LANES:c0 + (p + 1) * LANES] = (o[p * blk:(p + 1) * blk] * r_rows[p]).astype(BF16)


def _attention(sinks, q, k, v, batch, seq):
    tq = ATTN_TILE
    nt = seq // tq
    per = tq // ATTN_BLOCK
    cur = lambda b, t: (b * nt + t, 0)
    prev = lambda b, t: (jnp.maximum((b * nt + t) * per - 1, 0), 0)
    return pl.pallas_call(
        _attn_kernel,
        grid=(batch, nt),
        in_specs=[
            pl.BlockSpec(memory_space=pltpu.SMEM),
            pl.BlockSpec((tq, ATTN_Q_W), cur),
            pl.BlockSpec((tq, KV_BD_W), cur),
            pl.BlockSpec((ATTN_BLOCK, KV_BD_W), prev),
            pl.BlockSpec((tq, KV_BD_W), cur),
            pl.BlockSpec((ATTN_BLOCK, KV_BD_W), prev),
        ],
        out_specs=pl.BlockSpec((tq, ATTN_Q_W), cur),
        out_shape=jax.ShapeDtypeStruct(q.shape, BF16),
        compiler_params=pltpu.CompilerParams(dimension_semantics=("parallel", "parallel"), vmem_limit_bytes=VMEM_LIMIT),
        name="swa_attention",
    )(sinks, q, k, k, v, v)


def _mlstm_kernel(qk_ref, mv_ref, so_ref, grow_ref, brow_ref, hn_ref, out_ref, state, m_scr):
    t = pl.program_id(1)
    TL = qk_ref.shape[0]
    L = M_CHUNK
    R = 2 * M_HEADS

    @pl.when(t == 0)
    def _():
        state[...] = jnp.zeros(state.shape, F32)
        m_scr[...] = jnp.zeros(m_scr.shape, F32)

    gr = grow_ref[...] + brow_ref[...]
    lf = jax.nn.log_sigmoid(gr)

    ri = lax.broadcasted_iota(jnp.int32, (L, L), 0)
    ci = lax.broadcasted_iota(jnp.int32, (L, L), 1)
    causal = ci <= ri
    triu = (ri <= ci).astype(BF16)
    eye = (ri == ci).astype(BF16)
    eye2 = jnp.concatenate([eye, eye], axis=1)
    zeros8 = jnp.zeros((R, L), BF16)
    lane_in_chunk = lax.broadcasted_iota(jnp.int32, (R, TL), 1) & (L - 1)
    lo_b = lax.broadcasted_iota(jnp.int32, (L, LANES), 1) < M_QK_DIM
    top = lax.broadcasted_iota(jnp.int32, (L, LANES), 0) < M_QK_DIM
    top2 = lax.broadcasted_iota(jnp.int32, (L, 2 * M_V_DIM), 0) < M_QK_DIM
    ones_v = jnp.ones((L, M_V_DIM), BF16)
    zero_b = jnp.zeros((), BF16)

    def row_bcast(x8, h, n=L):
        return jnp.broadcast_to(x8[h:h + 1, :], (n, x8.shape[1]))

    chunk_rows = [slice(c * L, (c + 1) * L) for c in range(TL // L)]
    b8 = jnp.concatenate([sum(_dot(part, triu) for part in _split3(lf[:, rows])) for rows in chunk_rows], axis=1)
    b_all = pltpu.roll(b8, M_HEADS, 0)
    r_all = gr - b_all
    cm_all = r_all
    sh = 1
    while sh < L:
        cm_all = jnp.where(lane_in_chunk >= sh, jnp.maximum(cm_all, pltpu.roll(cm_all, sh, 1)), cm_all)
        sh *= 2

    m = m_scr[...]
    gates = []
    for rows in chunk_rows:
        b, r, cm = b_all[:, rows], r_all[:, rows], cm_all[:, rows]
        rmax = jnp.broadcast_to(cm[:, L - 1:L], (R, L))
        g = jnp.broadcast_to(b[:, L - 1:L], (R, L))
        big_m = jnp.maximum(m, cm).astype(BF16).astype(F32)
        inter = jnp.exp(m - big_m)
        emt = jnp.exp(-(b + big_m))
        w = jnp.exp(r - rmax)
        m_loc = g + rmax
        m_new = jnp.maximum(g + m, m_loc)
        decay = jnp.exp(g + m - m_new)
        scale = jnp.exp(m_loc - m_new)
        m = m_new
        gates.append((r, w, decay, scale, (big_m.astype(BF16), zeros8), _split2(inter), _split2(emt)))
    m_scr[...] = m

    for rows, (r, w, decay, scale, split_m, split_i, split_e) in zip(chunk_rows, gates):
        for p in range(M_HEADS // 2):
            pc = slice(p * LANES, (p + 1) * LANES)
            q_pair_b = qk_ref[rows, pc]
            q_pair = q_pair_b.astype(F32)
            k_pair_b = qk_ref[rows, M_QK_W + p * LANES:M_QK_W + (p + 1) * LANES]
            c_prev = state[p]
            c_prev_b = c_prev.astype(BF16)
            v_exts = []
            for hh in range(2):
                hd = 2 * p + hh
                cols = slice(hd * M_V_DIM, (hd + 1) * M_V_DIM)
                sel = lo_b if hh == 0 else jnp.logical_not(lo_b)
                v_ext = jnp.concatenate([mv_ref[rows, cols], ones_v], axis=1)
                v_exts.append(v_ext)
                rhs_t = jnp.concatenate(
                    [jnp.concatenate([row_bcast(part, hd) for part in parts], axis=1)
                     for parts in (split_m, split_i, split_e)], axis=0)
                colsq = _dot_nt(eye2, rhs_t)
                m_col, inter_col, emt_col = colsq[:, :L], colsq[:, L:2 * L], colsq[:, 2 * L:]

                d_mat = jnp.exp(jnp.where(causal, row_bcast(r, hd), -jnp.inf) - m_col)
                k_h = jnp.where(sel, k_pair_b, zero_b)
                s_mat = (_dot_nt(q_pair_b, k_h) * d_mat).astype(BF16)
                qs = jnp.where(sel, (q_pair * inter_col).astype(BF16), zero_b)
                nd = _dot(jnp.concatenate([s_mat, qs], axis=1), jnp.concatenate([v_ext, c_prev_b], axis=0))
                hcell = nd[:, :M_V_DIM] / jnp.maximum(jnp.abs(nd[:, M_V_DIM:]), emt_col)
                cell = _rms(hcell) * hn_ref[:, cols]
                out_ref[rows, cols] = (so_ref[rows, cols].astype(F32) * cell).astype(BF16)

            k_t = k_pair_b.astype(F32).T
            w_s = jnp.where(top, row_bcast(w, 2 * p), row_bcast(w, 2 * p + 1))
            kw_t = (k_t * w_s).astype(BF16)
            lhs = jnp.concatenate([jnp.where(top, kw_t, zero_b), jnp.where(top, zero_b, kw_t)], axis=1)
            a = _dot(lhs, jnp.concatenate(v_exts, axis=0))
            dec = jnp.where(top2, row_bcast(decay, 2 * p, L)[:, :1], row_bcast(decay, 2 * p + 1, L)[:, :1])
            sc = jnp.where(top2, row_bcast(scale, 2 * p, L)[:, :1], row_bcast(scale, 2 * p + 1, L)[:, :1])
            state[p] = dec * c_prev + sc * a


def _mlstm(qk, mv, so, grow, brow, hn, batch, seq):
    tl = MLSTM_TILE
    nt = seq // tl
    cur = lambda b, t: (b * nt + t, 0)
    const = lambda b, t: (0, 0)
    return pl.pallas_call(
        _mlstm_kernel,
        grid=(batch, nt),
        in_specs=[
            pl.BlockSpec((tl, 2 * M_QK_W), cur),
            pl.BlockSpec((tl, M_V_W), cur),
            pl.BlockSpec((tl, M_V_W), cur),
            pl.BlockSpec((2 * M_HEADS, tl), lambda b, t: (0, b * nt + t)),
            pl.BlockSpec(brow.shape, const),
            pl.BlockSpec(hn.shape, const),
        ],
        out_specs=pl.BlockSpec((tl, M_V_W), cur),
        out_shape=jax.ShapeDtypeStruct(mv.shape, BF16),
        scratch_shapes=[
            pltpu.VMEM((M_HEADS // 2, 2 * M_QK_DIM, 2 * M_V_DIM), F32),
            pltpu.VMEM((2 * M_HEADS, M_CHUNK), F32),
        ],
        compiler_params=pltpu.CompilerParams(dimension_semantics=("parallel", "arbitrary"), vmem_limit_bytes=VMEM_LIMIT),
        name="mlstm",
    )(qk, mv, so, grow, brow, hn)


def _merge_kernel(x_ref, a_ref, m_ref, sg_ref, wa_ref, wm_ref, wo_ref, g_ref, o_ref):
    a = a_ref[...]
    m = m_ref[...]
    parts = []
    for c0 in range(0, D_MODEL, MERGE_CHUNK):
        cols = slice(c0, c0 + MERGE_CHUNK)
        gcols = slice(D_MODEL + c0, D_MODEL + c0 + MERGE_CHUNK)
        parts.append((sg_ref[:, cols].astype(F32) * _dot(a, wa_ref[:, cols])
                      + sg_ref[:, gcols].astype(F32) * _dot(m, wm_ref[:, cols])).astype(BF16))
    y = _dot(jnp.concatenate(parts, axis=1), wo_ref[...])
    o_ref[...] = x_ref[...] + _rms(y) * g_ref[...]


def _merge(x2, attn, ml, sg, wa, wm, wo, g):
    T = x2.shape[0]
    tm = TOK_TILE
    row = lambda i: (i, 0)
    const = lambda i: (0, 0)

    def wspec(w):
        return pl.BlockSpec(w.shape, const, pipeline_mode=pl.Buffered(1))

    return pl.pallas_call(
        _merge_kernel,
        grid=(T // tm,),
        in_specs=[
            pl.BlockSpec((tm, D_MODEL), row),
            pl.BlockSpec((tm, ATTN_Q_W), row),
            pl.BlockSpec((tm, M_V_W), row),
            pl.BlockSpec((tm, 2 * D_MODEL), row),
            wspec(wa), wspec(wm), wspec(wo),
            pl.BlockSpec((1, D_MODEL), const),
        ],
        out_specs=pl.BlockSpec((tm, D_MODEL), row),
        out_shape=jax.ShapeDtypeStruct(x2.shape, F32),
        compiler_params=pltpu.CompilerParams(dimension_semantics=("parallel",), vmem_limit_bytes=VMEM_LIMIT),
        name="merge_outproj",
    )(x2, attn, ml, sg, wa, wm, wo, g)


def _ff_chunks():
    return [(c0, min(c0 + FF_CHUNK, D_FF)) for c0 in range(0, D_FF, FF_CHUNK)]


def _ffn_kernel(x_ref, gpre_ref, gpost_ref, wgu_ref, wo_ref, o_ref, acc_ref):
    x = x_ref[...]
    h = (_rms(x) * gpre_ref[...]).astype(BF16)
    for c0, c1 in _ff_chunks():
        gu = _dot(h, wgu_ref[:, 2 * c0:2 * c1])
        gate, up = gu[:, :c1 - c0], gu[:, c1 - c0:]
        act = (gate * jax.nn.sigmoid(gate) * up).astype(BF16)
        part = _dot(act, wo_ref[c0:c1, :])
        if c0 == 0:
            acc_ref[...] = part
        else:
            acc_ref[...] += part
    o_ref[...] = x + _rms(acc_ref[...]) * gpost_ref[...]


def _ffn(x1, gpre, gpost, wgu, wo):
    T = x1.shape[0]
    tm = TOK_TILE
    row = lambda i: (i, 0)
    const = lambda i: (0, 0)

    def wspec(w):
        return pl.BlockSpec(w.shape, const, pipeline_mode=pl.Buffered(1))

    return pl.pallas_call(
        _ffn_kernel,
        grid=(T // tm,),
        in_specs=[
            pl.BlockSpec((tm, D_MODEL), row),
            pl.BlockSpec((1, D_MODEL), const),
            pl.BlockSpec((1, D_MODEL), const),
            wspec(wgu), wspec(wo),
        ],
        out_specs=pl.BlockSpec((tm, D_MODEL), row),
        out_shape=jax.ShapeDtypeStruct(x1.shape, F32),
        scratch_shapes=[pltpu.VMEM((tm, D_MODEL), F32)],
        compiler_params=pltpu.CompilerParams(dimension_semantics=("parallel",), vmem_limit_bytes=VMEM_LIMIT),
        name="swiglu_ffn",
    )(x1, gpre, gpost, wgu, wo)


def _rope_tables(seq):
    inv_freq = ROPE_THETA ** (-jnp.arange(0, HEAD_DIM, 2, dtype=F32) / HEAD_DIM)
    ang = jnp.arange(seq).astype(F32)[:, None] * inv_freq[None, :]
    emb = jnp.concatenate([ang, ang], axis=-1)
    cos = jnp.cos(emb)
    sin = jnp.sin(emb)
    first_half = jnp.arange(HEAD_DIM) < HEAD_DIM // 2
    sin_a = jnp.where(first_half, -sin, 0.0)
    sin_b = jnp.where(first_half, 0.0, sin)
    rep = LANES // HEAD_DIM
    return jnp.tile(cos, (1, rep)), jnp.tile(sin_a, (1, rep)), jnp.tile(sin_b, (1, rep))


def _layer(x2, batch, seq, norm_pre_mix, norm_post_mix, norm_pre_ffn, norm_post_ffn, w_in, attn_sinks, conv_w,
           conv_b, b_igate, b_fgate, mlstm_head_norm, w_attn_branch, w_mlstm_branch, w_out, w_ffn_in, w_ffn_out):
    o_a = ATTN_Q_W + 2 * ATTN_KV_W
    o_m = o_a + 2 * M_QK_W
    o_v = o_m + 2 * M_V_W
    o_g = o_v + 2 * M_HEADS
    wa = jnp.concatenate([w_in[:, :o_a], w_in[:, o_v:o_g],
                          jnp.zeros((D_MODEL, LANES - 2 * M_HEADS), w_in.dtype)], axis=1).astype(BF16)
    wm = w_in[:, o_a:o_m].astype(BF16)
    wv = w_in[:, o_m:o_v].astype(BF16)
    wg = w_in[:, o_g:].astype(BF16)
    cos, sa, sb = _rope_tables(seq)

    q, k, v, qk, mv, so, grow, sg = _inproj(
        x2, norm_pre_mix[None, :], cos, sa, sb, conv_w, conv_b[None, :], wa, wm, wv, wg, seq)
    attn = _attention(attn_sinks, q, k, v, batch, seq)
    bias = jnp.concatenate([b_igate, b_fgate])
    ml = _mlstm(qk, mv, so, grow, bias[:, None], mlstm_head_norm[None, :], batch, seq)
    x1 = _merge(x2, attn, ml, sg, w_attn_branch.astype(BF16), w_mlstm_branch.astype(BF16), w_out.astype(BF16),
                norm_post_mix[None, :])
    wgu = jnp.concatenate([w_ffn_in[:, lo:hi] for c0, c1 in _ff_chunks() for lo, hi in ((c0, c1), (D_FF + c0, D_FF + c1))],
                          axis=1).astype(BF16)
    return _ffn(x1, norm_pre_ffn[None, :], norm_post_ffn[None, :], wgu, w_ffn_out.astype(BF16))


def kernel(x, norm_pre_mix, norm_post_mix, norm_pre_ffn, norm_post_ffn, w_in, attn_sinks, conv_w, conv_b, b_igate,
           b_fgate, mlstm_head_norm, w_attn_branch, w_mlstm_branch, w_out, w_ffn_in, w_ffn_out):
    B, S, D = x.shape
    x2 = x.reshape(B * S, D)
    for l in range(w_in.shape[0]):
        x2 = _layer(x2, B, S, norm_pre_mix[l], norm_post_mix[l], norm_pre_ffn[l], norm_post_ffn[l], w_in[l],
                    attn_sinks[l], conv_w[l], conv_b[l], b_igate[l], b_fgate[l], mlstm_head_norm[l],
                    w_attn_branch[l], w_mlstm_branch[l], w_out[l], w_ffn_in[l], w_ffn_out[l])
    return x2.reshape(B, S, D)
```

```python
import functools

import jax
import jax.numpy as jnp
from jax import lax
from jax.experimental import pallas as pl
from jax.experimental.pallas import tpu as pltpu

D_MODEL = 1024
HEAD_DIM = 64
N_Q_HEADS = 8
N_KV_HEADS = 2
WINDOW = 128
ATTN_BLOCK = 128
ROPE_THETA = 10000.0
M_HEADS = 4
M_QK_DIM = 64
M_V_DIM = 128
M_CHUNK = 128
CONV_WIDTH = 4
D_FF = 2816
EPS = 1e-6

ATTN_Q_W = N_Q_HEADS * HEAD_DIM
ATTN_KV_W = N_KV_HEADS * HEAD_DIM
M_QK_W = M_HEADS * M_QK_DIM
M_V_W = M_HEADS * M_V_DIM
KV_BD_W = 2 * ATTN_KV_W * 2

LANES = 128
SUBLANES = 8
VMEM_LIMIT = 56 * 1024 * 1024
CONV_HALO = SUBLANES

TOK_TILE = 512
ATTN_TILE = 1024
MLSTM_TILE = 1024
FF_CHUNK = 512
MERGE_CHUNK = 256

F32 = jnp.float32
BF16 = jnp.bfloat16


def _dot(a, b):
    return jnp.dot(a, b, preferred_element_type=F32)


def _dot_nt(a, b):
    return lax.dot_general(a, b, (((1,), (1,)), ((), ())), preferred_element_type=F32)


def _rms(x):
    return x * lax.rsqrt(jnp.mean(x * x, axis=-1, keepdims=True) + EPS)


def _split2(x):
    hi = x.astype(BF16)
    return hi, (x - hi.astype(F32)).astype(BF16)


def _split3(x):
    hi = x.astype(BF16)
    r1 = x - hi.astype(F32)
    mid = r1.astype(BF16)
    lo = (r1 - mid.astype(F32)).astype(BF16)
    return hi, mid, lo


def _inproj_kernel(x_ref, g_ref, cos_ref, sa_ref, sb_ref, cw_ref, cb_ref, wa_ref, wm_ref, wv_ref,
                   q_ref, k_ref, v_ref, qk_ref, mv_ref, so_ref, grow_ref, cbuf, *, n_seq_tiles):
    tm = x_ref.shape[0]
    h = (_rms(x_ref[...]) * g_ref[...]).astype(BF16)
    cos = cos_ref[...]
    sa = sa_ref[...]
    sb = sb_ref[...]
    lane = lax.broadcasted_iota(jnp.int32, cos.shape, 1)
    lo = lane < HEAD_DIM

    def rope(t):
        return t * cos + pltpu.roll(t, LANES - HEAD_DIM // 2, 1) * sa + pltpu.roll(t, HEAD_DIM // 2, 1) * sb

    def store_block_diag(ref, t):
        tb = t.astype(BF16)
        rb = pltpu.roll(t, HEAD_DIM, 1).astype(BF16)
        zero = jnp.zeros((), BF16)
        for i, (src, keep_lo) in enumerate(((tb, True), (rb, False), (rb, True), (tb, False))):
            ref[:, i * LANES:(i + 1) * LANES] = jnp.where(lo, src, zero) if keep_lo else jnp.where(lo, zero, src)

    def attn_proj():
        a = _dot(h, wa_ref[...])
        grow_ref[...] = a[:, ATTN_Q_W + 2 * LANES:].T[:2 * M_HEADS, :]
        for j in range(ATTN_Q_W // LANES):
            q_ref[:, j * LANES:(j + 1) * LANES] = (
                rope(a[:, j * LANES:(j + 1) * LANES]) * (HEAD_DIM ** -0.5)).astype(BF16)
        store_block_diag(k_ref, rope(a[:, ATTN_Q_W:ATTN_Q_W + LANES]))
        store_block_diag(v_ref, a[:, ATTN_Q_W + LANES:ATTN_Q_W + 2 * LANES])

    @pl.when(pl.program_id(0) % n_seq_tiles == 0)
    def _():
        cbuf[0:CONV_HALO, :] = jnp.zeros((CONV_HALO, cbuf.shape[1]), F32)

    def mqk_proj(c0, c1):
        cbuf[CONV_HALO:CONV_HALO + tm, c0:c1] = _dot(h, wm_ref[:, c0:c1])

    def conv_chunk(j0):
        cols = slice(j0 * LANES, (j0 + 1) * LANES)
        y = cb_ref[:, cols]
        for j in range(CONV_WIDTH):
            off = CONV_HALO - (CONV_WIDTH - 1) + j
            y = y + cw_ref[j:j + 1, cols] * cbuf[off:off + tm, cols]
        cbuf[0:CONV_HALO, cols] = cbuf[tm:tm + CONV_HALO, cols]
        y = y * jax.nn.sigmoid(y)
        if j0 * LANES >= M_QK_W:
            y = y * (M_QK_DIM ** -0.5)
        qk_ref[:, cols] = y.astype(BF16)

    def mlstm_vo_proj():
        vo = _dot(h, wv_ref[...])
        mv_ref[...] = vo[:, :M_V_W].astype(BF16)
        so_ref[...] = jax.nn.sigmoid(vo[:, M_V_W:]).astype(BF16)

    mqk_proj(0, M_QK_W)
    conv_chunk(0)
    conv_chunk(1)
    attn_proj()
    mqk_proj(M_QK_W, 2 * M_QK_W)
    conv_chunk(2)
    conv_chunk(3)
    mlstm_vo_proj()


def _inproj(x2, g, cos, sa, sb, cw, cb, wa, wm, wv, seq):
    T = x2.shape[0]
    tm = TOK_TILE
    n_seq_tiles = seq // tm
    row = lambda i: (i, 0)
    pos = lambda i: (i % n_seq_tiles, 0)
    const = lambda i: (0, 0)

    def wspec(w):
        return pl.BlockSpec(w.shape, const, pipeline_mode=pl.Buffered(1))

    out_shape = (
        jax.ShapeDtypeStruct((T, ATTN_Q_W), BF16),
        jax.ShapeDtypeStruct((T, KV_BD_W), BF16),
        jax.ShapeDtypeStruct((T, KV_BD_W), BF16),
        jax.ShapeDtypeStruct((T, 2 * M_QK_W), BF16),
        jax.ShapeDtypeStruct((T, M_V_W), BF16),
        jax.ShapeDtypeStruct((T, M_V_W), BF16),
        jax.ShapeDtypeStruct((2 * M_HEADS, T), F32),
    )
    out_specs = (
        pl.BlockSpec((tm, ATTN_Q_W), row),
        pl.BlockSpec((tm, KV_BD_W), row),
        pl.BlockSpec((tm, KV_BD_W), row),
        pl.BlockSpec((tm, 2 * M_QK_W), row),
        pl.BlockSpec((tm, M_V_W), row),
        pl.BlockSpec((tm, M_V_W), row),
        pl.BlockSpec((2 * M_HEADS, tm), lambda i: (0, i)),
    )
    in_specs = [
        pl.BlockSpec((tm, D_MODEL), row),
        pl.BlockSpec((1, D_MODEL), const),
        pl.BlockSpec((tm, LANES), pos),
        pl.BlockSpec((tm, LANES), pos),
        pl.BlockSpec((tm, LANES), pos),
        pl.BlockSpec(cw.shape, const),
        pl.BlockSpec(cb.shape, const),
        wspec(wa), wspec(wm), wspec(wv),
    ]
    return pl.pallas_call(
        functools.partial(_inproj_kernel, n_seq_tiles=n_seq_tiles),
        grid=(T // tm,),
        in_specs=in_specs,
        out_specs=out_specs,
        out_shape=out_shape,
        scratch_shapes=[pltpu.VMEM((CONV_HALO + tm, 2 * M_QK_W), F32)],
        compiler_params=pltpu.CompilerParams(dimension_semantics=("arbitrary",), vmem_limit_bytes=VMEM_LIMIT),
        name="inproj",
    )(x2, g, cos, sa, sb, cw, cb, wa, wm, wv)


def _attn_kernel(sink_ref, q_ref, kc_ref, kp_ref, vc_ref, vp_ref, o_ref):
    t = pl.program_id(1)
    blk = ATTN_BLOCK
    n_blk = q_ref.shape[0] // blk
    lo_q = lax.broadcasted_iota(jnp.int32, (blk, LANES), 1) < HEAD_DIM
    ri = lax.broadcasted_iota(jnp.int32, (blk, blk), 0)
    ci = lax.broadcasted_iota(jnp.int32, (blk, blk), 1)
    own = ci <= ri
    prev_ok = ci > ri + jnp.where(t > 0, 0, blk)
    zero = jnp.zeros((), BF16)

    for n in range(n_blk):
        rows = slice(n * blk, (n + 1) * blk)
        k_prev = kp_ref if n == 0 else kc_ref.at[(n - 1) * blk:n * blk, :]
        v_prev = vp_ref if n == 0 else vc_ref.at[(n - 1) * blk:n * blk, :]
        for j in range(N_KV_HEADS):
            kv_cols = [slice((2 * j + i) * LANES, (2 * j + i + 1) * LANES) for i in range(2)]
            k_bd = jnp.concatenate([x for c in kv_cols for x in (k_prev[:, c], kc_ref[rows, c])], axis=0)
            v_bd = jnp.concatenate([x for c in kv_cols for x in (v_prev[:, c], vc_ref[rows, c])], axis=0)
            c0 = 2 * j * LANES
            q2 = jnp.concatenate([q_ref[rows, c0:c0 + LANES], q_ref[rows, c0 + LANES:c0 + 2 * LANES]], axis=0)
            s = _dot_nt(q2, k_bd)
            e_rows, r_rows = [], []
            for p in range(2):
                e_cols, r_cols = [], []
                for hh in range(2):
                    sink = sink_ref[4 * j + 2 * p + hh]
                    s_prev = s[p * blk:(p + 1) * blk, 2 * hh * blk:(2 * hh + 1) * blk]
                    s_own = s[p * blk:(p + 1) * blk, (2 * hh + 1) * blk:(2 * hh + 2) * blk]
                    if n == 0:
                        s_prev = jnp.where(prev_ok, s_prev, -jnp.inf)
                    sc = jnp.where(own, s_own, s_prev)
                    m = jnp.maximum(jnp.max(sc, axis=-1, keepdims=True), sink)
                    e = jnp.exp(sc - m)
                    den = jnp.sum(e, axis=-1, keepdims=True) + jnp.exp(sink - m)
                    eb = e.astype(BF16)
                    e_cols += [jnp.where(own, zero, eb), jnp.where(own, eb, zero)]
                    r_cols.append(1.0 / den)
                e_rows.append(jnp.concatenate(e_cols, axis=1))
                r_rows.append(jnp.where(lo_q, r_cols[0], r_cols[1]))
            o = _dot(jnp.concatenate(e_rows, axis=0), v_bd)
            for p in range(2):
                o_ref[rows, c0 + p * LANES:c0 + (p + 1) * LANES] = (o[p * blk:(p + 1) * blk] * r_rows[p]).astype(BF16)


def _attention(sinks, q, k, v, batch, seq):
    tq = ATTN_TILE
    nt = seq // tq
    per = tq // ATTN_BLOCK
    cur = lambda b, t: (b * nt + t, 0)
    prev = lambda b, t: (jnp.maximum((b * nt + t) * per - 1, 0), 0)
    return pl.pallas_call(
        _attn_kernel,
        grid=(batch, nt),
        in_specs=[
            pl.BlockSpec(memory_space=pltpu.SMEM),
            pl.BlockSpec((tq, ATTN_Q_W), cur),
            pl.BlockSpec((tq, KV_BD_W), cur),
            pl.BlockSpec((ATTN_BLOCK, KV_BD_W), prev),
            pl.BlockSpec((tq, KV_BD_W), cur),
            pl.BlockSpec((ATTN_BLOCK, KV_BD_W), prev),
        ],
        out_specs=pl.BlockSpec((tq, ATTN_Q_W), cur),
        out_shape=jax.ShapeDtypeStruct(q.shape, BF16),
        compiler_params=pltpu.CompilerParams(dimension_semantics=("parallel", "parallel"), vmem_limit_bytes=VMEM_LIMIT),
        name="swa_attention",
    )(sinks, q, k, k, v, v)


def _mlstm_kernel(qk_ref, mv_ref, so_ref, grow_ref, brow_ref, hn_ref, out_ref, state, m_scr):
    t = pl.program_id(1)
    TL = qk_ref.shape[0]
    L = M_CHUNK
    R = 2 * M_HEADS

    @pl.when(t == 0)
    def _():
        state[...] = jnp.zeros(state.shape, F32)
        m_scr[...] = jnp.zeros(m_scr.shape, F32)

    gr = grow_ref[...] + brow_ref[...]
    lf = jax.nn.log_sigmoid(gr)

    ri = lax.broadcasted_iota(jnp.int32, (L, L), 0)
    ci = lax.broadcasted_iota(jnp.int32, (L, L), 1)
    causal = ci <= ri
    triu = (ri <= ci).astype(BF16)
    eye = (ri == ci).astype(BF16)
    eye2 = jnp.concatenate([eye, eye], axis=1)
    zeros8 = jnp.zeros((R, L), BF16)
    lane_in_chunk = lax.broadcasted_iota(jnp.int32, (R, TL), 1) & (L - 1)
    lo_b = lax.broadcasted_iota(jnp.int32, (L, LANES), 1) < M_QK_DIM
    top = lax.broadcasted_iota(jnp.int32, (L, LANES), 0) < M_QK_DIM
    top2 = lax.broadcasted_iota(jnp.int32, (L, 2 * M_V_DIM), 0) < M_QK_DIM
    ones_v = jnp.ones((L, M_V_DIM), BF16)
    zero_b = jnp.zeros((), BF16)

    def row_bcast(x8, h, n=L):
        return jnp.broadcast_to(x8[h:h + 1, :], (n, x8.shape[1]))

    chunk_rows = [slice(c * L, (c + 1) * L) for c in range(TL // L)]
    b8 = jnp.concatenate([sum(_dot(part, triu) for part in _split3(lf[:, rows])) for rows in chunk_rows], axis=1)
    b_all = pltpu.roll(b8, M_HEADS, 0)
    r_all = gr - b_all
    cm_all = r_all
    sh = 1
    while sh < L:
        cm_all = jnp.where(lane_in_chunk >= sh, jnp.maximum(cm_all, pltpu.roll(cm_all, sh, 1)), cm_all)
        sh *= 2

    m = m_scr[...]
    gates = []
    for rows in chunk_rows:
        b, r, cm = b_all[:, rows], r_all[:, rows], cm_all[:, rows]
        rmax = jnp.broadcast_to(cm[:, L - 1:L], (R, L))
        g = jnp.broadcast_to(b[:, L - 1:L], (R, L))
        big_m = jnp.maximum(m, cm).astype(BF16).astype(F32)
        inter = jnp.exp(m - big_m)
        emt = jnp.exp(-(b + big_m))
        w = jnp.exp(r - rmax)
        m_loc = g + rmax
        m_new = jnp.maximum(g + m, m_loc)
        decay = jnp.exp(g + m - m_new)
        scale = jnp.exp(m_loc - m_new)
        m = m_new
        gates.append((r, w, decay, scale, (big_m.astype(BF16), zeros8), _split2(inter), _split2(emt)))
    m_scr[...] = m

    for rows, (r, w, decay, scale, split_m, split_i, split_e) in zip(chunk_rows, gates):
        for p in range(M_HEADS // 2):
            pc = slice(p * LANES, (p + 1) * LANES)
            q_pair_b = qk_ref[rows, pc]
            q_pair = q_pair_b.astype(F32)
            k_pair_b = qk_ref[rows, M_QK_W + p * LANES:M_QK_W + (p + 1) * LANES]
            c_prev = state[p]
            c_prev_b = c_prev.astype(BF16)
            v_exts = []
            for hh in range(2):
                hd = 2 * p + hh
                cols = slice(hd * M_V_DIM, (hd + 1) * M_V_DIM)
                sel = lo_b if hh == 0 else jnp.logical_not(lo_b)
                v_ext = jnp.concatenate([mv_ref[rows, cols], ones_v], axis=1)
                v_exts.append(v_ext)
                rhs_t = jnp.concatenate(
                    [jnp.concatenate([row_bcast(part, hd) for part in parts], axis=1)
                     for parts in (split_m, split_i, split_e)], axis=0)
                colsq = _dot_nt(eye2, rhs_t)
                m_col, inter_col, emt_col = colsq[:, :L], colsq[:, L:2 * L], colsq[:, 2 * L:]

                d_mat = jnp.exp(jnp.where(causal, row_bcast(r, hd), -jnp.inf) - m_col)
                k_h = jnp.where(sel, k_pair_b, zero_b)
                s_mat = (_dot_nt(q_pair_b, k_h) * d_mat).astype(BF16)
                qs = jnp.where(sel, (q_pair * inter_col).astype(BF16), zero_b)
                nd = _dot(jnp.concatenate([s_mat, qs], axis=1), jnp.concatenate([v_ext, c_prev_b], axis=0))
                hcell = nd[:, :M_V_DIM] / jnp.maximum(jnp.abs(nd[:, M_V_DIM:]), emt_col)
                cell = _rms(hcell) * hn_ref[:, cols]
                out_ref[rows, cols] = (so_ref[rows, cols].astype(F32) * cell).astype(BF16)

            k_t = k_pair_b.astype(F32).T
            w_s = jnp.where(top, row_bcast(w, 2 * p), row_bcast(w, 2 * p + 1))
            kw_t = (k_t * w_s).astype(BF16)
            lhs = jnp.concatenate([jnp.where(top, kw_t, zero_b), jnp.where(top, zero_b, kw_t)], axis=1)
            a = _dot(lhs, jnp.concatenate(v_exts, axis=0))
            dec = jnp.where(top2, row_bcast(decay, 2 * p, L)[:, :1], row_bcast(decay, 2 * p + 1, L)[:, :1])
            sc = jnp.where(top2, row_bcast(scale, 2 * p, L)[:, :1], row_bcast(scale, 2 * p + 1, L)[:, :1])
            state[p] = dec * c_prev + sc * a


def _mlstm(qk, mv, so, grow, brow, hn, batch, seq):
    tl = MLSTM_TILE
    nt = seq // tl
    cur = lambda b, t: (b * nt + t, 0)
    const = lambda b, t: (0, 0)
    return pl.pallas_call(
        _mlstm_kernel,
        grid=(batch, nt),
        in_specs=[
            pl.BlockSpec((tl, 2 * M_QK_W), cur),
            pl.BlockSpec((tl, M_V_W), cur),
            pl.BlockSpec((tl, M_V_W), cur),
            pl.BlockSpec((2 * M_HEADS, tl), lambda b, t: (0, b * nt + t)),
            pl.BlockSpec(brow.shape, const),
            pl.BlockSpec(hn.shape, const),
        ],
        out_specs=pl.BlockSpec((tl, M_V_W), cur),
        out_shape=jax.ShapeDtypeStruct(mv.shape, BF16),
        scratch_shapes=[
            pltpu.VMEM((M_HEADS // 2, 2 * M_QK_DIM, 2 * M_V_DIM), F32),
            pltpu.VMEM((2 * M_HEADS, M_CHUNK), F32),
        ],
        compiler_params=pltpu.CompilerParams(dimension_semantics=("parallel", "arbitrary"), vmem_limit_bytes=VMEM_LIMIT),
        name="mlstm",
    )(qk, mv, so, grow, brow, hn)


def _ff_chunks():
    return [(c0, min(c0 + FF_CHUNK, D_FF)) for c0 in range(0, D_FF, FF_CHUNK)]


def _mix_ffn_kernel(x_ref, a_ref, m_ref, gmix_ref, gpost_ref, gpre_ref, gffn_ref, wg_ref, wa_ref, wm_ref, wo_ref,
                    wgu_ref, wfo_ref, o_ref, acc_ref):
    x = x_ref[...]
    h = (_rms(x) * gmix_ref[...]).astype(BF16)
    a = a_ref[...]
    m = m_ref[...]
    parts = []
    for c0 in range(0, D_MODEL, MERGE_CHUNK):
        cols = slice(c0, c0 + MERGE_CHUNK)
        gcols = slice(D_MODEL + c0, D_MODEL + c0 + MERGE_CHUNK)
        g_attn = jax.nn.sigmoid(_dot(h, wg_ref[:, cols]))
        g_mlstm = jax.nn.sigmoid(_dot(h, wg_ref[:, gcols]))
        parts.append((g_attn * _dot(a, wa_ref[:, cols]) + g_mlstm * _dot(m, wm_ref[:, cols])).astype(BF16))
    y = _dot(jnp.concatenate(parts, axis=1), wo_ref[...])
    x1 = x + _rms(y) * gpost_ref[...]

    h2 = (_rms(x1) * gpre_ref[...]).astype(BF16)
    for c0, c1 in _ff_chunks():
        gu = _dot(h2, wgu_ref[:, 2 * c0:2 * c1])
        gate, up = gu[:, :c1 - c0], gu[:, c1 - c0:]
        act = (gate * jax.nn.sigmoid(gate) * up).astype(BF16)
        part = _dot(act, wfo_ref[c0:c1, :])
        if c0 == 0:
            acc_ref[...] = part
        else:
            acc_ref[...] += part
    o_ref[...] = x1 + _rms(acc_ref[...]) * gffn_ref[...]


def _mix_ffn(x2, attn, ml, gmix, gpost, gpre, gffn, wg, wa, wm, wo, wgu, wfo):
    T = x2.shape[0]
    tm = TOK_TILE
    row = lambda i: (i, 0)
    const = lambda i: (0, 0)

    def wspec(w):
        return pl.BlockSpec(w.shape, const, pipeline_mode=pl.Buffered(1))

    gain = pl.BlockSpec((1, D_MODEL), const)
    return pl.pallas_call(
        _mix_ffn_kernel,
        grid=(T // tm,),
        in_specs=[
            pl.BlockSpec((tm, D_MODEL), row),
            pl.BlockSpec((tm, ATTN_Q_W), row),
            pl.BlockSpec((tm, M_V_W), row),
            gain, gain, gain, gain,
            wspec(wg), wspec(wa), wspec(wm), wspec(wo), wspec(wgu), wspec(wfo),
        ],
        out_specs=pl.BlockSpec((tm, D_MODEL), row),
        out_shape=jax.ShapeDtypeStruct(x2.shape, F32),
        scratch_shapes=[pltpu.VMEM((tm, D_MODEL), F32)],
        compiler_params=pltpu.CompilerParams(dimension_semantics=("parallel",), vmem_limit_bytes=VMEM_LIMIT),
        name="mix_ffn",
    )(x2, attn, ml, gmix, gpost, gpre, gffn, wg, wa, wm, wo, wgu, wfo)


def _rope_tables(seq):
    inv_freq = ROPE_THETA ** (-jnp.arange(0, HEAD_DIM, 2, dtype=F32) / HEAD_DIM)
    ang = jnp.arange(seq).astype(F32)[:, None] * inv_freq[None, :]
    emb = jnp.concatenate([ang, ang], axis=-1)
    cos = jnp.cos(emb)
    sin = jnp.sin(emb)
    first_half = jnp.arange(HEAD_DIM) < HEAD_DIM // 2
    sin_a = jnp.where(first_half, -sin, 0.0)
    sin_b = jnp.where(first_half, 0.0, sin)
    rep = LANES // HEAD_DIM
    return jnp.tile(cos, (1, rep)), jnp.tile(sin_a, (1, rep)), jnp.tile(sin_b, (1, rep))


def _layer(x2, batch, seq, norm_pre_mix, norm_post_mix, norm_pre_ffn, norm_post_ffn, w_in, attn_sinks, conv_w,
           conv_b, b_igate, b_fgate, mlstm_head_norm, w_attn_branch, w_mlstm_branch, w_out, w_ffn_in, w_ffn_out):
    o_a = ATTN_Q_W + 2 * ATTN_KV_W
    o_m = o_a + 2 * M_QK_W
    o_v = o_m + 2 * M_V_W
    o_g = o_v + 2 * M_HEADS
    wa = jnp.concatenate([w_in[:, :o_a], w_in[:, o_v:o_g],
                          jnp.zeros((D_MODEL, LANES - 2 * M_HEADS), w_in.dtype)], axis=1).astype(BF16)
    wm = w_in[:, o_a:o_m].astype(BF16)
    wv = w_in[:, o_m:o_v].astype(BF16)
    wg = w_in[:, o_g:].astype(BF16)
    cos, sa, sb = _rope_tables(seq)

    q, k, v, qk, mv, so, grow = _inproj(
        x2, norm_pre_mix[None, :], cos, sa, sb, conv_w, conv_b[None, :], wa, wm, wv, seq)
    attn = _attention(attn_sinks, q, k, v, batch, seq)
    bias = jnp.concatenate([b_igate, b_fgate])
    ml = _mlstm(qk, mv, so, grow, bias[:, None], mlstm_head_norm[None, :], batch, seq)
    wgu = jnp.concatenate([w_ffn_in[:, lo:hi] for c0, c1 in _ff_chunks() for lo, hi in ((c0, c1), (D_FF + c0, D_FF + c1))],
                          axis=1).astype(BF16)
    return _mix_ffn(x2, attn, ml, norm_pre_mix[None, :], norm_post_mix[None, :], norm_pre_ffn[None, :],
                    norm_post_ffn[None, :], wg, w_attn_branch.astype(BF16), w_mlstm_branch.astype(BF16),
                    w_out.astype(BF16), wgu, w_ffn_out.astype(BF16))


def kernel(x, norm_pre_mix, norm_post_mix, norm_pre_ffn, norm_post_ffn, w_in, attn_sinks, conv_w, conv_b, b_igate,
           b_fgate, mlstm_head_norm, w_attn_branch, w_mlstm_branch, w_out, w_ffn_in, w_ffn_out):
    B, S, D = x.shape
    x2 = x.reshape(B * S, D)
    for l in range(w_in.shape[0]):
        x2 = _layer(x2, B, S, norm_pre_mix[l], norm_post_mix[l], norm_pre_ffn[l], norm_post_ffn[l], w_in[l],
                    attn_sinks[l], conv_w[l], conv_b[l], b_igate[l], b_fgate[l], mlstm_head_norm[l],
                    w_attn_branch[l], w_mlstm_branch[l], w_out[l], w_ffn_in[l], w_ffn_out[l])
    return x2.reshape(B, S, D)
```

```python
import functools

import jax
import jax.numpy as jnp
from jax import lax
from jax.experimental import pallas as pl
from jax.experimental.pallas import tpu as pltpu

D_MODEL = 1024
HEAD_DIM = 64
N_Q_HEADS = 8
N_KV_HEADS = 2
WINDOW = 128
ATTN_BLOCK = 128
ROPE_THETA = 10000.0
M_HEADS = 4
M_QK_DIM = 64
M_V_DIM = 128
M_CHUNK = 128
CONV_WIDTH = 4
D_FF = 2816
EPS = 1e-6

ATTN_Q_W = N_Q_HEADS * HEAD_DIM
ATTN_KV_W = N_KV_HEADS * HEAD_DIM
M_QK_W = M_HEADS * M_QK_DIM
M_V_W = M_HEADS * M_V_DIM
KV_BD_W = 2 * ATTN_KV_W * 2
O_MQK = ATTN_Q_W + 2 * ATTN_KV_W
O_MV = O_MQK + 2 * M_QK_W
O_GATES = O_MV + 2 * M_V_W
O_BRANCH = O_GATES + 2 * M_HEADS

LANES = 128
SUBLANES = 8
VMEM_LIMIT = 56 * 1024 * 1024
CONV_HALO = SUBLANES

TOK_TILE = 512
ATTN_TILE = 1024
MLSTM_TILE = 1024
FF_CHUNK = 512
MERGE_CHUNK = 256
ROW_GROUPS = 2

F32 = jnp.float32
BF16 = jnp.bfloat16


def _dot(a, b):
    return jnp.dot(a, b, preferred_element_type=F32)


def _dot_nt(a, b):
    return lax.dot_general(a, b, (((1,), (1,)), ((), ())), preferred_element_type=F32)


def _rms(x):
    return x * lax.rsqrt(jnp.mean(x * x, axis=-1, keepdims=True) + EPS)


def _split3(x):
    hi = x.astype(BF16)
    r1 = x - hi.astype(F32)
    mid = r1.astype(BF16)
    lo = (r1 - mid.astype(F32)).astype(BF16)
    return hi, mid, lo


def _inproj_kernel(x_ref, g_ref, cos_ref, sa_ref, sb_ref, cw_ref, cb_ref, w_ref, wif_ref,
                   q_ref, k_ref, v_ref, qk_ref, mv_ref, so_ref, grow_ref, cbuf, *, n_seq_tiles):
    tm = x_ref.shape[0]
    h = (_rms(x_ref[...]) * g_ref[...]).astype(BF16)
    cos = cos_ref[...]
    sa = sa_ref[...]
    sb = sb_ref[...]
    lane = lax.broadcasted_iota(jnp.int32, cos.shape, 1)
    lo = lane < HEAD_DIM

    def rope(t):
        return t * cos + pltpu.roll(t, LANES - HEAD_DIM // 2, 1) * sa + pltpu.roll(t, HEAD_DIM // 2, 1) * sb

    def store_block_diag(ref, t):
        tb = t.astype(BF16)
        rb = pltpu.roll(t, HEAD_DIM, 1).astype(BF16)
        zero = jnp.zeros((), BF16)
        for i, (src, keep_lo) in enumerate(((tb, True), (rb, False), (rb, True), (tb, False))):
            ref[:, i * LANES:(i + 1) * LANES] = jnp.where(lo, src, zero) if keep_lo else jnp.where(lo, zero, src)

    def attn_proj():
        a = _dot(h, w_ref[:, :O_MQK])
        for j in range(ATTN_Q_W // LANES):
            q_ref[:, j * LANES:(j + 1) * LANES] = (
                rope(a[:, j * LANES:(j + 1) * LANES]) * (HEAD_DIM ** -0.5)).astype(BF16)
        store_block_diag(k_ref, rope(a[:, ATTN_Q_W:ATTN_Q_W + LANES]))
        store_block_diag(v_ref, a[:, ATTN_Q_W + LANES:ATTN_Q_W + 2 * LANES])
        grow_ref[...] = _dot(h, wif_ref[...]).T[:2 * M_HEADS, :]

    @pl.when(pl.program_id(0) % n_seq_tiles == 0)
    def _():
        cbuf[0:CONV_HALO, :] = jnp.zeros((CONV_HALO, cbuf.shape[1]), F32)

    def mqk_proj(c0, c1):
        cbuf[CONV_HALO:CONV_HALO + tm, c0:c1] = _dot(h, w_ref[:, O_MQK + c0:O_MQK + c1])

    def conv_chunk(j0):
        cols = slice(j0 * LANES, (j0 + 1) * LANES)
        y = cb_ref[:, cols]
        for j in range(CONV_WIDTH):
            off = CONV_HALO - (CONV_WIDTH - 1) + j
            y = y + cw_ref[j:j + 1, cols] * cbuf[off:off + tm, cols]
        cbuf[0:CONV_HALO, cols] = cbuf[tm:tm + CONV_HALO, cols]
        y = y * jax.nn.sigmoid(y)
        if j0 * LANES >= M_QK_W:
            y = y * (M_QK_DIM ** -0.5)
        qk_ref[:, cols] = y.astype(BF16)

    def mlstm_vo_proj():
        vo = _dot(h, w_ref[:, O_MV:O_GATES])
        mv_ref[...] = vo[:, :M_V_W].astype(BF16)
        so_ref[...] = jax.nn.sigmoid(vo[:, M_V_W:]).astype(BF16)

    mqk_proj(0, M_QK_W)
    conv_chunk(0)
    conv_chunk(1)
    attn_proj()
    mqk_proj(M_QK_W, 2 * M_QK_W)
    conv_chunk(2)
    conv_chunk(3)
    mlstm_vo_proj()


def _inproj(x2, g, cos, sa, sb, cw, cb, w, wif, seq):
    T = x2.shape[0]
    tm = TOK_TILE
    n_seq_tiles = seq // tm
    row = lambda i: (i, 0)
    pos = lambda i: (i % n_seq_tiles, 0)
    const = lambda i: (0, 0)

    def wspec(w):
        return pl.BlockSpec(w.shape, const, pipeline_mode=pl.Buffered(1))

    out_shape = (
        jax.ShapeDtypeStruct((T, ATTN_Q_W), BF16),
        jax.ShapeDtypeStruct((T, KV_BD_W), BF16),
        jax.ShapeDtypeStruct((T, KV_BD_W), BF16),
        jax.ShapeDtypeStruct((T, 2 * M_QK_W), BF16),
        jax.ShapeDtypeStruct((T, M_V_W), BF16),
        jax.ShapeDtypeStruct((T, M_V_W), BF16),
        jax.ShapeDtypeStruct((2 * M_HEADS, T), F32),
    )
    out_specs = (
        pl.BlockSpec((tm, ATTN_Q_W), row),
        pl.BlockSpec((tm, KV_BD_W), row),
        pl.BlockSpec((tm, KV_BD_W), row),
        pl.BlockSpec((tm, 2 * M_QK_W), row),
        pl.BlockSpec((tm, M_V_W), row),
        pl.BlockSpec((tm, M_V_W), row),
        pl.BlockSpec((2 * M_HEADS, tm), lambda i: (0, i)),
    )
    in_specs = [
        pl.BlockSpec((tm, D_MODEL), row),
        pl.BlockSpec((1, D_MODEL), const),
        pl.BlockSpec((tm, LANES), pos),
        pl.BlockSpec((tm, LANES), pos),
        pl.BlockSpec((tm, LANES), pos),
        pl.BlockSpec(cw.shape, const),
        pl.BlockSpec(cb.shape, const),
        wspec(w), wspec(wif),
    ]
    return pl.pallas_call(
        functools.partial(_inproj_kernel, n_seq_tiles=n_seq_tiles),
        grid=(T // tm,),
        in_specs=in_specs,
        out_specs=out_specs,
        out_shape=out_shape,
        scratch_shapes=[pltpu.VMEM((CONV_HALO + tm, 2 * M_QK_W), F32)],
        compiler_params=pltpu.CompilerParams(dimension_semantics=("arbitrary",), vmem_limit_bytes=VMEM_LIMIT),
        name="inproj",
    )(x2, g, cos, sa, sb, cw, cb, w, wif)


def _attn_kernel(sink_ref, q_ref, kc_ref, kp_ref, vc_ref, vp_ref, o_ref):
    t = pl.program_id(1)
    blk = ATTN_BLOCK
    n_blk = q_ref.shape[0] // blk
    lo_q = lax.broadcasted_iota(jnp.int32, (blk, LANES), 1) < HEAD_DIM
    ri = lax.broadcasted_iota(jnp.int32, (blk, blk), 0)
    ci = lax.broadcasted_iota(jnp.int32, (blk, blk), 1)
    own = ci <= ri
    prev_ok = ci > ri + jnp.where(t > 0, 0, blk)
    zero = jnp.zeros((), BF16)

    for n in range(n_blk):
        rows = slice(n * blk, (n + 1) * blk)
        k_prev = kp_ref if n == 0 else kc_ref.at[(n - 1) * blk:n * blk, :]
        v_prev = vp_ref if n == 0 else vc_ref.at[(n - 1) * blk:n * blk, :]
        for j in range(N_KV_HEADS):
            kv_cols = [slice((2 * j + i) * LANES, (2 * j + i + 1) * LANES) for i in range(2)]
            k_bd = jnp.concatenate([x for c in kv_cols for x in (k_prev[:, c], kc_ref[rows, c])], axis=0)
            v_bd = jnp.concatenate([x for c in kv_cols for x in (v_prev[:, c], vc_ref[rows, c])], axis=0)
            c0 = 2 * j * LANES
            q2 = jnp.concatenate([q_ref[rows, c0:c0 + LANES], q_ref[rows, c0 + LANES:c0 + 2 * LANES]], axis=0)
            s = _dot_nt(q2, k_bd)
            e_rows, r_rows = [], []
            for p in range(2):
                e_cols, r_cols = [], []
                for hh in range(2):
                    sink = sink_ref[4 * j + 2 * p + hh]
                    s_prev = s[p * blk:(p + 1) * blk, 2 * hh * blk:(2 * hh + 1) * blk]
                    s_own = s[p * blk:(p + 1) * blk, (2 * hh + 1) * blk:(2 * hh + 2) * blk]
                    if n == 0:
                        s_prev = jnp.where(prev_ok, s_prev, -jnp.inf)
                    sc = jnp.where(own, s_own, s_prev)
                    m = jnp.maximum(jnp.max(sc, axis=-1, keepdims=True), sink)
                    e = jnp.exp(sc - m)
                    den = jnp.sum(e, axis=-1, keepdims=True) + jnp.exp(sink - m)
                    eb = e.astype(BF16)
                    e_cols += [jnp.where(own, zero, eb), jnp.where(own, eb, zero)]
                    r_cols.append(1.0 / den)
                e_rows.append(jnp.concatenate(e_cols, axis=1))
                r_rows.append(jnp.where(lo_q, r_cols[0], r_cols[1]))
            o = _dot(jnp.concatenate(e_rows, axis=0), v_bd)
            for p in range(2):
                o_ref[rows, c0 + p * LANES:c0 + (p + 1) * LANES] = (o[p * blk:(p + 1) * blk] * r_rows[p]).astype(BF16)


def _attention(sinks, q, k, v, batch, seq):
    tq = ATTN_TILE
    nt = seq // tq
    per = tq // ATTN_BLOCK
    cur = lambda b, t: (b * nt + t, 0)
    prev = lambda b, t: (jnp.maximum((b * nt + t) * per - 1, 0), 0)
    return pl.pallas_call(
        _attn_kernel,
        grid=(batch, nt),
        in_specs=[
            pl.BlockSpec(memory_space=pltpu.SMEM),
            pl.BlockSpec((tq, ATTN_Q_W), cur),
            pl.BlockSpec((tq, KV_BD_W), cur),
            pl.BlockSpec((ATTN_BLOCK, KV_BD_W), prev),
            pl.BlockSpec((tq, KV_BD_W), cur),
            pl.BlockSpec((ATTN_BLOCK, KV_BD_W), prev),
        ],
        out_specs=pl.BlockSpec((tq, ATTN_Q_W), cur),
        out_shape=jax.ShapeDtypeStruct(q.shape, BF16),
        compiler_params=pltpu.CompilerParams(dimension_semantics=("parallel", "parallel"), vmem_limit_bytes=VMEM_LIMIT),
        name="swa_attention",
    )(sinks, q, k, k, v, v)


def _mlstm_kernel(qk_ref, mv_ref, so_ref, grow_ref, brow_ref, hn_ref, out_ref, state, m_scr):
    t = pl.program_id(1)
    TL = qk_ref.shape[0]
    L = M_CHUNK
    R = 2 * M_HEADS

    @pl.when(t == 0)
    def _():
        state[...] = jnp.zeros(state.shape, F32)
        m_scr[...] = jnp.zeros(m_scr.shape, F32)

    gr = grow_ref[...] + brow_ref[...]
    lf = jax.nn.log_sigmoid(gr)

    ri = lax.broadcasted_iota(jnp.int32, (L, L), 0)
    ci = lax.broadcasted_iota(jnp.int32, (L, L), 1)
    causal = ci <= ri
    triu = (ri <= ci).astype(BF16)
    lane_in_chunk = lax.broadcasted_iota(jnp.int32, (R, TL), 1) & (L - 1)
    lo_b = lax.broadcasted_iota(jnp.int32, (L, LANES), 1) < M_QK_DIM
    top = lax.broadcasted_iota(jnp.int32, (L, LANES), 0) < M_QK_DIM
    top2 = lax.broadcasted_iota(jnp.int32, (L, 2 * M_V_DIM), 0) < M_QK_DIM
    ones_v = jnp.ones((L, M_V_DIM), BF16)
    zero_b = jnp.zeros((), BF16)

    def row_bcast(x8, h, n=L):
        return jnp.broadcast_to(x8[h:h + 1, :], (n, x8.shape[1]))

    chunk_rows = [slice(c * L, (c + 1) * L) for c in range(TL // L)]
    b8 = jnp.concatenate([sum(_dot(part, triu) for part in _split3(lf[:, rows])) for rows in chunk_rows], axis=1)
    b_all = pltpu.roll(b8, M_HEADS, 0)
    r_all = gr - b_all
    cm_all = r_all
    sh = 1
    while sh < L:
        cm_all = jnp.where(lane_in_chunk >= sh, jnp.maximum(cm_all, pltpu.roll(cm_all, sh, 1)), cm_all)
        sh *= 2

    m = m_scr[...]
    gates = []
    for rows in chunk_rows:
        b, r, cm = b_all[:, rows], r_all[:, rows], cm_all[:, rows]
        rmax = jnp.broadcast_to(cm[:, L - 1:L], (R, L))
        g = jnp.broadcast_to(b[:, L - 1:L], (R, L))
        big_m = jnp.maximum(m, cm)
        inter = jnp.exp(m - big_m)
        emt = jnp.exp(-(b + big_m))
        w = jnp.exp(r - rmax)
        m_loc = g + rmax
        m_new = jnp.maximum(g + m, m_loc)
        decay = jnp.exp(g + m - m_new)
        scale = jnp.exp(m_loc - m_new)
        m = m_new
        gates.append((r, w, decay, scale, big_m, inter, emt))
    m_scr[...] = m

    for rows, (r, w, decay, scale, big_m, inter, emt) in zip(chunk_rows, gates):
        for p in range(M_HEADS // 2):
            pc = slice(p * LANES, (p + 1) * LANES)
            q_pair_b = qk_ref[rows, pc]
            q_pair = q_pair_b.astype(F32)
            k_pair_b = qk_ref[rows, M_QK_W + p * LANES:M_QK_W + (p + 1) * LANES]
            c_prev = state[p]
            c_prev_b = c_prev.astype(BF16)
            v_exts = []
            for hh in range(2):
                hd = 2 * p + hh
                cols = slice(hd * M_V_DIM, (hd + 1) * M_V_DIM)
                sel = lo_b if hh == 0 else jnp.logical_not(lo_b)
                v_ext = jnp.concatenate([mv_ref[rows, cols], ones_v], axis=1)
                v_exts.append(v_ext)
                m_col, inter_col, emt_col = (row_bcast(x8, hd).T for x8 in (big_m, inter, emt))

                d_mat = jnp.exp(jnp.where(causal, row_bcast(r, hd), -jnp.inf) - m_col)
                k_h = jnp.where(sel, k_pair_b, zero_b)
                s_mat = (_dot_nt(q_pair_b, k_h) * d_mat).astype(BF16)
                qs = jnp.where(sel, (q_pair * inter_col).astype(BF16), zero_b)
                nd = _dot(jnp.concatenate([s_mat, qs], axis=1), jnp.concatenate([v_ext, c_prev_b], axis=0))
                hcell = nd[:, :M_V_DIM] / jnp.maximum(jnp.abs(nd[:, M_V_DIM:]), emt_col)
                cell = _rms(hcell) * hn_ref[:, cols]
                out_ref[rows, cols] = (so_ref[rows, cols].astype(F32) * cell).astype(BF16)

            k_t = k_pair_b.astype(F32).T
            w_s = jnp.where(top, row_bcast(w, 2 * p), row_bcast(w, 2 * p + 1))
            kw_t = (k_t * w_s).astype(BF16)
            lhs = jnp.concatenate([jnp.where(top, kw_t, zero_b), jnp.where(top, zero_b, kw_t)], axis=1)
            a = _dot(lhs, jnp.concatenate(v_exts, axis=0))
            dec = jnp.where(top2, row_bcast(decay, 2 * p, L)[:, :1], row_bcast(decay, 2 * p + 1, L)[:, :1])
            sc = jnp.where(top2, row_bcast(scale, 2 * p, L)[:, :1], row_bcast(scale, 2 * p + 1, L)[:, :1])
            state[p] = dec * c_prev + sc * a


def _mlstm(qk, mv, so, grow, brow, hn, batch, seq):
    tl = MLSTM_TILE
    nt = seq // tl
    cur = lambda b, t: (b * nt + t, 0)
    const = lambda b, t: (0, 0)
    return pl.pallas_call(
        _mlstm_kernel,
        grid=(batch, nt),
        in_specs=[
            pl.BlockSpec((tl, 2 * M_QK_W), cur),
            pl.BlockSpec((tl, M_V_W), cur),
            pl.BlockSpec((tl, M_V_W), cur),
            pl.BlockSpec((2 * M_HEADS, tl), lambda b, t: (0, b * nt + t)),
            pl.BlockSpec(brow.shape, const),
            pl.BlockSpec(hn.shape, const),
        ],
        out_specs=pl.BlockSpec((tl, M_V_W), cur),
        out_shape=jax.ShapeDtypeStruct(mv.shape, BF16),
        scratch_shapes=[
            pltpu.VMEM((M_HEADS // 2, 2 * M_QK_DIM, 2 * M_V_DIM), F32),
            pltpu.VMEM((2 * M_HEADS, M_CHUNK), F32),
        ],
        compiler_params=pltpu.CompilerParams(dimension_semantics=("parallel", "arbitrary"), vmem_limit_bytes=VMEM_LIMIT),
        name="mlstm",
    )(qk, mv, so, grow, brow, hn)


def _ff_chunks():
    return [(c0, min(c0 + FF_CHUNK, D_FF)) for c0 in range(0, D_FF, FF_CHUNK)]


def _mix_ffn_kernel(x_ref, a_ref, m_ref, gmix_ref, gpost_ref, gpre_ref, gffn_ref, wg_ref, wa_ref, wm_ref, wo_ref,
                    wfi_ref, wfo_ref, o_ref, acc_ref):
    tm = x_ref.shape[0]
    groups = [slice(i * tm // ROW_GROUPS, (i + 1) * tm // ROW_GROUPS) for i in range(ROW_GROUPS)]
    x = [x_ref[r, :] for r in groups]
    h = [(_rms(xi) * gmix_ref[...]).astype(BF16) for xi in x]
    a = [a_ref[r, :] for r in groups]
    m = [m_ref[r, :] for r in groups]
    parts = [[] for _ in groups]
    for c0 in range(0, D_MODEL, MERGE_CHUNK):
        cols = slice(c0, c0 + MERGE_CHUNK)
        gcols = slice(D_MODEL + c0, D_MODEL + c0 + MERGE_CHUNK)
        for i in range(ROW_GROUPS):
            g_attn = jax.nn.sigmoid(_dot(h[i], wg_ref[:, cols]))
            g_mlstm = jax.nn.sigmoid(_dot(h[i], wg_ref[:, gcols]))
            parts[i].append(
                (g_attn * _dot(a[i], wa_ref[:, cols]) + g_mlstm * _dot(m[i], wm_ref[:, cols])).astype(BF16))
    y = [_dot(jnp.concatenate(p, axis=1), wo_ref[...]) for p in parts]
    x1 = [xi + _rms(yi) * gpost_ref[...] for xi, yi in zip(x, y)]

    h2 = [(_rms(xi) * gpre_ref[...]).astype(BF16) for xi in x1]
    for c0, c1 in _ff_chunks():
        for i, r in enumerate(groups):
            gate = _dot(h2[i], wfi_ref[:, c0:c1])
            up = _dot(h2[i], wfi_ref[:, D_FF + c0:D_FF + c1])
            act = (gate * jax.nn.sigmoid(gate) * up).astype(BF16)
            part = _dot(act, wfo_ref[c0:c1, :])
            if c0 == 0:
                acc_ref[r, :] = part
            else:
                acc_ref[r, :] += part
    for i, r in enumerate(groups):
        o_ref[r, :] = x1[i] + _rms(acc_ref[r, :]) * gffn_ref[...]


def _mix_ffn(x2, attn, ml, gmix, gpost, gpre, gffn, wg, wa, wm, wo, wfi, wfo):
    T = x2.shape[0]
    tm = TOK_TILE
    row = lambda i: (i, 0)
    const = lambda i: (0, 0)

    def wspec(w):
        return pl.BlockSpec(w.shape, const, pipeline_mode=pl.Buffered(1))

    gain = pl.BlockSpec((1, D_MODEL), const)
    return pl.pallas_call(
        _mix_ffn_kernel,
        grid=(T // tm,),
        in_specs=[
            pl.BlockSpec((tm, D_MODEL), row),
            pl.BlockSpec((tm, ATTN_Q_W), row),
            pl.BlockSpec((tm, M_V_W), row),
            gain, gain, gain, gain,
            wspec(wg), wspec(wa), wspec(wm), wspec(wo), wspec(wfi), wspec(wfo),
        ],
        out_specs=pl.BlockSpec((tm, D_MODEL), row),
        out_shape=jax.ShapeDtypeStruct(x2.shape, F32),
        scratch_shapes=[pltpu.VMEM((tm, D_MODEL), F32)],
        compiler_params=pltpu.CompilerParams(dimension_semantics=("parallel",), vmem_limit_bytes=VMEM_LIMIT),
        name="mix_ffn",
    )(x2, attn, ml, gmix, gpost, gpre, gffn, wg, wa, wm, wo, wfi, wfo)


def _rope_tables(seq):
    inv_freq = ROPE_THETA ** (-jnp.arange(0, HEAD_DIM, 2, dtype=F32) / HEAD_DIM)
    ang = jnp.arange(seq).astype(F32)[:, None] * inv_freq[None, :]
    emb = jnp.concatenate([ang, ang], axis=-1)
    cos = jnp.cos(emb)
    sin = jnp.sin(emb)
    first_half = jnp.arange(HEAD_DIM) < HEAD_DIM // 2
    sin_a = jnp.where(first_half, -sin, 0.0)
    sin_b = jnp.where(first_half, 0.0, sin)
    rep = LANES // HEAD_DIM
    return jnp.tile(cos, (1, rep)), jnp.tile(sin_a, (1, rep)), jnp.tile(sin_b, (1, rep))


def _layer(x2, batch, seq, norm_pre_mix, norm_post_mix, norm_pre_ffn, norm_post_ffn, w_in, attn_sinks, conv_w,
           conv_b, b_igate, b_fgate, mlstm_head_norm, w_attn_branch, w_mlstm_branch, w_out, w_ffn_in, w_ffn_out):
    w_mix = w_in[:, :O_GATES].astype(BF16)
    wif = jnp.pad(w_in[:, O_GATES:O_BRANCH], ((0, 0), (0, LANES - 2 * M_HEADS))).astype(BF16)
    wg = w_in[:, O_BRANCH:].astype(BF16)
    cos, sa, sb = _rope_tables(seq)

    q, k, v, qk, mv, so, grow = _inproj(
        x2, norm_pre_mix[None, :], cos, sa, sb, conv_w, conv_b[None, :], w_mix, wif, seq)
    attn = _attention(attn_sinks, q, k, v, batch, seq)
    bias = jnp.concatenate([b_igate, b_fgate])
    ml = _mlstm(qk, mv, so, grow, bias[:, None], mlstm_head_norm[None, :], batch, seq)
    return _mix_ffn(x2, attn, ml, norm_pre_mix[None, :], norm_post_mix[None, :], norm_pre_ffn[None, :],
                    norm_post_ffn[None, :], wg, w_attn_branch.astype(BF16), w_mlstm_branch.astype(BF16),
                    w_out.astype(BF16), w_ffn_in.astype(BF16), w_ffn_out.astype(BF16))


def kernel(x, norm_pre_mix, norm_post_mix, norm_pre_ffn, norm_post_ffn, w_in, attn_sinks, conv_w, conv_b, b_igate,
           b_fgate, mlstm_head_norm, w_attn_branch, w_mlstm_branch, w_out, w_ffn_in, w_ffn_out):
    B, S, D = x.shape
    x2 = x.reshape(B * S, D)
    for l in range(w_in.shape[0]):
        x2 = _layer(x2, B, S, norm_pre_mix[l], norm_post_mix[l], norm_pre_ffn[l], norm_post_ffn[l], w_in[l],
                    attn_sinks[l], conv_w[l], conv_b[l], b_igate[l], b_fgate[l], mlstm_head_norm[l],
                    w_attn_branch[l], w_mlstm_branch[l], w_out[l], w_ffn_in[l], w_ffn_out[l])
    return x2.reshape(B, S, D)
```

```python
import functools

import numpy as np

import jax
import jax.numpy as jnp
from jax import lax
from jax.experimental import pallas as pl
from jax.experimental.pallas import tpu as pltpu

D_MODEL = 1024
HEAD_DIM = 64
N_Q_HEADS = 8
N_KV_HEADS = 2
WINDOW = 128
ATTN_BLOCK = 128
ROPE_THETA = 10000.0
M_HEADS = 4
M_QK_DIM = 64
M_V_DIM = 128
M_CHUNK = 128
CONV_WIDTH = 4
D_FF = 2816
EPS = 1e-6

ATTN_Q_W = N_Q_HEADS * HEAD_DIM
ATTN_KV_W = N_KV_HEADS * HEAD_DIM
M_QK_W = M_HEADS * M_QK_DIM
M_V_W = M_HEADS * M_V_DIM
KV_BD_W = 2 * ATTN_KV_W * 2
O_MQK = ATTN_Q_W + 2 * ATTN_KV_W
O_MV = O_MQK + 2 * M_QK_W
O_GATES = O_MV + 2 * M_V_W
O_BRANCH = O_GATES + 2 * M_HEADS

LANES = 128
SUBLANES = 8
VMEM_LIMIT = 56 * 1024 * 1024
CONV_HALO = SUBLANES

TOK_TILE = 512
ATTN_TILE = 1024
MLSTM_TILE = 1024
FF_CHUNK = 512
MERGE_CHUNK = 256
ROW_GROUPS = 1

F32 = jnp.float32
BF16 = jnp.bfloat16


def _dot(a, b):
    return jnp.dot(a, b, preferred_element_type=F32)


def _dot_nt(a, b):
    return lax.dot_general(a, b, (((1,), (1,)), ((), ())), preferred_element_type=F32)


def _rms(x):
    return x * lax.rsqrt(jnp.mean(x * x, axis=-1, keepdims=True) + EPS)


def _split3(x):
    hi = x.astype(BF16)
    r1 = x - hi.astype(F32)
    mid = r1.astype(BF16)
    lo = (r1 - mid.astype(F32)).astype(BF16)
    return hi, mid, lo


def _inproj_kernel(x_ref, g_ref, cos_ref, sa_ref, sb_ref, cw_ref, cb_ref, w_ref, wif_ref,
                   q_ref, k_ref, v_ref, qk_ref, mv_ref, so_ref, grow_ref, cbuf, *, n_seq_tiles):
    tm = x_ref.shape[0]
    h = (_rms(x_ref[...]) * g_ref[...]).astype(BF16)
    cos = cos_ref[...]
    sa = sa_ref[...]
    sb = sb_ref[...]
    lane = lax.broadcasted_iota(jnp.int32, cos.shape, 1)
    lo = lane < HEAD_DIM

    def rope(t):
        return t * cos + pltpu.roll(t, LANES - HEAD_DIM // 2, 1) * sa + pltpu.roll(t, HEAD_DIM // 2, 1) * sb

    def store_block_diag(ref, t):
        tb = t.astype(BF16)
        rb = pltpu.roll(t, HEAD_DIM, 1).astype(BF16)
        zero = jnp.zeros((), BF16)
        for i, (src, keep_lo) in enumerate(((tb, True), (rb, False), (rb, True), (tb, False))):
            ref[:, i * LANES:(i + 1) * LANES] = jnp.where(lo, src, zero) if keep_lo else jnp.where(lo, zero, src)

    def attn_proj():
        a = _dot(h, w_ref[:, :O_MQK])
        for j in range(ATTN_Q_W // LANES):
            q_ref[:, j * LANES:(j + 1) * LANES] = (
                rope(a[:, j * LANES:(j + 1) * LANES]) * (HEAD_DIM ** -0.5)).astype(BF16)
        store_block_diag(k_ref, rope(a[:, ATTN_Q_W:ATTN_Q_W + LANES]))
        store_block_diag(v_ref, a[:, ATTN_Q_W + LANES:ATTN_Q_W + 2 * LANES])
        grow_ref[...] = _dot(h, wif_ref[...]).T[:2 * M_HEADS, :]

    @pl.when(pl.program_id(0) % n_seq_tiles == 0)
    def _():
        cbuf[0:CONV_HALO, :] = jnp.zeros((CONV_HALO, cbuf.shape[1]), F32)

    def mqk_proj(c0, c1):
        cbuf[CONV_HALO:CONV_HALO + tm, c0:c1] = _dot(h, w_ref[:, O_MQK + c0:O_MQK + c1])

    def conv_chunk(j0):
        cols = slice(j0 * LANES, (j0 + 1) * LANES)
        y = cb_ref[:, cols]
        for j in range(CONV_WIDTH):
            off = CONV_HALO - (CONV_WIDTH - 1) + j
            y = y + cw_ref[j:j + 1, cols] * cbuf[off:off + tm, cols]
        cbuf[0:CONV_HALO, cols] = cbuf[tm:tm + CONV_HALO, cols]
        y = y * jax.nn.sigmoid(y)
        if j0 * LANES >= M_QK_W:
            y = y * (M_QK_DIM ** -0.5)
        qk_ref[:, cols] = y.astype(BF16)

    def mlstm_vo_proj():
        vo = _dot(h, w_ref[:, O_MV:O_GATES])
        mv_ref[...] = vo[:, :M_V_W].astype(BF16)
        so_ref[...] = jax.nn.sigmoid(vo[:, M_V_W:]).astype(BF16)

    mqk_proj(0, M_QK_W)
    conv_chunk(0)
    conv_chunk(1)
    attn_proj()
    mqk_proj(M_QK_W, 2 * M_QK_W)
    conv_chunk(2)
    conv_chunk(3)
    mlstm_vo_proj()


def _inproj(x2, g, cos, sa, sb, cw, cb, w, wif, seq):
    T = x2.shape[0]
    tm = TOK_TILE
    n_seq_tiles = seq // tm
    row = lambda i: (i, 0)
    pos = lambda i: (i % n_seq_tiles, 0)
    const = lambda i: (0, 0)

    def wspec(w):
        return pl.BlockSpec(w.shape, const, pipeline_mode=pl.Buffered(1))

    out_shape = (
        jax.ShapeDtypeStruct((T, ATTN_Q_W), BF16),
        jax.ShapeDtypeStruct((T, KV_BD_W), BF16),
        jax.ShapeDtypeStruct((T, KV_BD_W), BF16),
        jax.ShapeDtypeStruct((T, 2 * M_QK_W), BF16),
        jax.ShapeDtypeStruct((T, M_V_W), BF16),
        jax.ShapeDtypeStruct((T, M_V_W), BF16),
        jax.ShapeDtypeStruct((2 * M_HEADS, T), F32),
    )
    out_specs = (
        pl.BlockSpec((tm, ATTN_Q_W), row),
        pl.BlockSpec((tm, KV_BD_W), row),
        pl.BlockSpec((tm, KV_BD_W), row),
        pl.BlockSpec((tm, 2 * M_QK_W), row),
        pl.BlockSpec((tm, M_V_W), row),
        pl.BlockSpec((tm, M_V_W), row),
        pl.BlockSpec((2 * M_HEADS, tm), lambda i: (0, i)),
    )
    in_specs = [
        pl.BlockSpec((tm, D_MODEL), row),
        pl.BlockSpec((1, D_MODEL), const),
        pl.BlockSpec((tm, LANES), pos),
        pl.BlockSpec((tm, LANES), pos),
        pl.BlockSpec((tm, LANES), pos),
        pl.BlockSpec(cw.shape, const),
        pl.BlockSpec(cb.shape, const),
        wspec(w), wspec(wif),
    ]
    return pl.pallas_call(
        functools.partial(_inproj_kernel, n_seq_tiles=n_seq_tiles),
        grid=(T // tm,),
        in_specs=in_specs,
        out_specs=out_specs,
        out_shape=out_shape,
        scratch_shapes=[pltpu.VMEM((CONV_HALO + tm, 2 * M_QK_W), F32)],
        compiler_params=pltpu.CompilerParams(dimension_semantics=("arbitrary",), vmem_limit_bytes=VMEM_LIMIT),
        name="inproj",
    )(x2, g, cos, sa, sb, cw, cb, w, wif)


def _attn_kernel(sink_ref, q_ref, kc_ref, kp_ref, vc_ref, vp_ref, o_ref):
    t = pl.program_id(1)
    blk = ATTN_BLOCK
    n_blk = q_ref.shape[0] // blk
    lo_q = lax.broadcasted_iota(jnp.int32, (blk, LANES), 1) < HEAD_DIM
    ri = lax.broadcasted_iota(jnp.int32, (blk, blk), 0)
    ci = lax.broadcasted_iota(jnp.int32, (blk, blk), 1)
    own = ci <= ri
    prev_ok = ci > ri + jnp.where(t > 0, 0, blk)
    zero = jnp.zeros((), BF16)
    row_is_a = lax.broadcasted_iota(jnp.int32, (4 * blk, LANES), 0) < 2 * blk
    lane_is_a = lax.broadcasted_iota(jnp.int32, (4 * blk, LANES), 1) < HEAD_DIM
    ones_bd = (row_is_a == lane_is_a).astype(BF16)

    for n in range(n_blk):
        rows = slice(n * blk, (n + 1) * blk)
        k_prev = kp_ref if n == 0 else kc_ref.at[(n - 1) * blk:n * blk, :]
        v_prev = vp_ref if n == 0 else vc_ref.at[(n - 1) * blk:n * blk, :]
        for j in range(N_KV_HEADS):
            kv_cols = [slice((2 * j + i) * LANES, (2 * j + i + 1) * LANES) for i in range(2)]
            k_bd = jnp.concatenate([x for c in kv_cols for x in (k_prev[:, c], kc_ref[rows, c])], axis=0)
            v_bd = jnp.concatenate([x for c in kv_cols for x in (v_prev[:, c], vc_ref[rows, c])], axis=0)
            c0 = 2 * j * LANES
            q2 = jnp.concatenate([q_ref[rows, c0:c0 + LANES], q_ref[rows, c0 + LANES:c0 + 2 * LANES]], axis=0)
            s = _dot_nt(q2, k_bd)
            e_rows, sink_rows = [], []
            for p in range(2):
                e_cols, sink_cols = [], []
                for hh in range(2):
                    sink = sink_ref[4 * j + 2 * p + hh]
                    s_prev = s[p * blk:(p + 1) * blk, 2 * hh * blk:(2 * hh + 1) * blk]
                    s_own = s[p * blk:(p + 1) * blk, (2 * hh + 1) * blk:(2 * hh + 2) * blk]
                    if n == 0:
                        s_prev = jnp.where(prev_ok, s_prev, -jnp.inf)
                    sc = jnp.where(own, s_own, s_prev)
                    m = jnp.maximum(jnp.max(sc, axis=-1, keepdims=True), sink)
                    eb = jnp.exp(sc - m).astype(BF16)
                    e_cols += [jnp.where(own, zero, eb), jnp.where(own, eb, zero)]
                    sink_cols.append(jnp.exp(sink - m))
                e_rows.append(jnp.concatenate(e_cols, axis=1))
                sink_rows.append(jnp.where(lo_q, sink_cols[0], sink_cols[1]))
            o = _dot(jnp.concatenate(e_rows, axis=0), jnp.concatenate([v_bd, ones_bd], axis=1))
            for p in range(2):
                op = o[p * blk:(p + 1) * blk]
                o_ref[rows, c0 + p * LANES:c0 + (p + 1) * LANES] = (
                    op[:, :LANES] / (op[:, LANES:] + sink_rows[p])).astype(BF16)


def _attention(sinks, q, k, v, batch, seq):
    tq = ATTN_TILE
    nt = seq // tq
    per = tq // ATTN_BLOCK
    cur = lambda b, t: (b * nt + t, 0)
    prev = lambda b, t: (jnp.maximum((b * nt + t) * per - 1, 0), 0)
    return pl.pallas_call(
        _attn_kernel,
        grid=(batch, nt),
        in_specs=[
            pl.BlockSpec(memory_space=pltpu.SMEM),
            pl.BlockSpec((tq, ATTN_Q_W), cur),
            pl.BlockSpec((tq, KV_BD_W), cur),
            pl.BlockSpec((ATTN_BLOCK, KV_BD_W), prev),
            pl.BlockSpec((tq, KV_BD_W), cur),
            pl.BlockSpec((ATTN_BLOCK, KV_BD_W), prev),
        ],
        out_specs=pl.BlockSpec((tq, ATTN_Q_W), cur),
        out_shape=jax.ShapeDtypeStruct(q.shape, BF16),
        compiler_params=pltpu.CompilerParams(dimension_semantics=("parallel", "parallel"), vmem_limit_bytes=VMEM_LIMIT),
        name="swa_attention",
    )(sinks, q, k, k, v, v)


def _mlstm_kernel(qk_ref, mv_ref, so_ref, grow_ref, brow_ref, hn_ref, out_ref, state, m_scr):
    t = pl.program_id(1)
    TL = qk_ref.shape[0]
    L = M_CHUNK
    R = 2 * M_HEADS

    @pl.when(t == 0)
    def _():
        state[...] = jnp.zeros(state.shape, F32)
        m_scr[...] = jnp.zeros(m_scr.shape, F32)

    gr = grow_ref[...] + brow_ref[...]
    lf = jax.nn.log_sigmoid(gr)

    ri = lax.broadcasted_iota(jnp.int32, (L, L), 0)
    ci = lax.broadcasted_iota(jnp.int32, (L, L), 1)
    causal = ci <= ri
    triu = (ri <= ci).astype(BF16)
    lane_in_chunk = lax.broadcasted_iota(jnp.int32, (R, TL), 1) & (L - 1)
    lo_b = lax.broadcasted_iota(jnp.int32, (L, LANES), 1) < M_QK_DIM
    top = lax.broadcasted_iota(jnp.int32, (L, LANES), 0) < M_QK_DIM
    top2 = lax.broadcasted_iota(jnp.int32, (L, 2 * M_V_DIM), 0) < M_QK_DIM
    ones_v = jnp.ones((L, M_V_DIM), BF16)
    zero_b = jnp.zeros((), BF16)

    def row_bcast(x8, h, n=L):
        return jnp.broadcast_to(x8[h:h + 1, :], (n, x8.shape[1]))

    chunk_rows = [slice(c * L, (c + 1) * L) for c in range(TL // L)]
    b8 = jnp.concatenate([sum(_dot(part, triu) for part in _split3(lf[:, rows])) for rows in chunk_rows], axis=1)
    b_all = pltpu.roll(b8, M_HEADS, 0)
    r_all = gr - b_all
    cm_all = r_all
    sh = 1
    while sh < L:
        cm_all = jnp.where(lane_in_chunk >= sh, jnp.maximum(cm_all, pltpu.roll(cm_all, sh, 1)), cm_all)
        sh *= 2

    m = m_scr[...]
    gates = []
    for rows in chunk_rows:
        b, r, cm = b_all[:, rows], r_all[:, rows], cm_all[:, rows]
        rmax = jnp.broadcast_to(cm[:, L - 1:L], (R, L))
        g = jnp.broadcast_to(b[:, L - 1:L], (R, L))
        big_m = jnp.maximum(m, cm)
        inter = jnp.exp(m - big_m)
        emt = jnp.exp(-(b + big_m))
        w = jnp.exp(r - rmax)
        m_loc = g + rmax
        m_new = jnp.maximum(g + m, m_loc)
        decay = jnp.exp(g + m - m_new)
        scale = jnp.exp(m_loc - m_new)
        m = m_new
        gates.append((r, w, decay, scale, big_m, inter, emt))
    m_scr[...] = m

    for rows, (r, w, decay, scale, big_m, inter, emt) in zip(chunk_rows, gates):
        for p in range(M_HEADS // 2):
            pc = slice(p * LANES, (p + 1) * LANES)
            q_pair_b = qk_ref[rows, pc]
            q_pair = q_pair_b.astype(F32)
            k_pair_b = qk_ref[rows, M_QK_W + p * LANES:M_QK_W + (p + 1) * LANES]
            c_prev = state[p]
            c_prev_b = c_prev.astype(BF16)
            v_exts = []
            for hh in range(2):
                hd = 2 * p + hh
                cols = slice(hd * M_V_DIM, (hd + 1) * M_V_DIM)
                sel = lo_b if hh == 0 else jnp.logical_not(lo_b)
                v_ext = jnp.concatenate([mv_ref[rows, cols], ones_v], axis=1)
                v_exts.append(v_ext)
                m_col, inter_col, emt_col = (row_bcast(x8, hd).T for x8 in (big_m, inter, emt))

                d_mat = jnp.exp(jnp.where(causal, row_bcast(r, hd), -jnp.inf) - m_col)
                k_h = jnp.where(sel, k_pair_b, zero_b)
                s_mat = (_dot_nt(q_pair_b, k_h) * d_mat).astype(BF16)
                qs = jnp.where(sel, (q_pair * inter_col).astype(BF16), zero_b)
                nd = _dot(jnp.concatenate([s_mat, qs], axis=1), jnp.concatenate([v_ext, c_prev_b], axis=0))
                hcell = nd[:, :M_V_DIM] / jnp.maximum(jnp.abs(nd[:, M_V_DIM:]), emt_col)
                cell = _rms(hcell) * hn_ref[:, cols]
                out_ref[rows, cols] = (so_ref[rows, cols].astype(F32) * cell).astype(BF16)

            k_t = k_pair_b.astype(F32).T
            w_s = jnp.where(top, row_bcast(w, 2 * p), row_bcast(w, 2 * p + 1))
            kw_t = (k_t * w_s).astype(BF16)
            lhs = jnp.concatenate([jnp.where(top, kw_t, zero_b), jnp.where(top, zero_b, kw_t)], axis=1)
            a = _dot(lhs, jnp.concatenate(v_exts, axis=0))
            dec = jnp.where(top2, row_bcast(decay, 2 * p, L)[:, :1], row_bcast(decay, 2 * p + 1, L)[:, :1])
            sc = jnp.where(top2, row_bcast(scale, 2 * p, L)[:, :1], row_bcast(scale, 2 * p + 1, L)[:, :1])
            state[p] = dec * c_prev + sc * a


def _mlstm(qk, mv, so, grow, brow, hn, batch, seq):
    tl = MLSTM_TILE
    nt = seq // tl
    cur = lambda b, t: (b * nt + t, 0)
    const = lambda b, t: (0, 0)
    return pl.pallas_call(
        _mlstm_kernel,
        grid=(batch, nt),
        in_specs=[
            pl.BlockSpec((tl, 2 * M_QK_W), cur),
            pl.BlockSpec((tl, M_V_W), cur),
            pl.BlockSpec((tl, M_V_W), cur),
            pl.BlockSpec((2 * M_HEADS, tl), lambda b, t: (0, b * nt + t)),
            pl.BlockSpec(brow.shape, const),
            pl.BlockSpec(hn.shape, const),
        ],
        out_specs=pl.BlockSpec((tl, M_V_W), cur),
        out_shape=jax.ShapeDtypeStruct(mv.shape, BF16),
        scratch_shapes=[
            pltpu.VMEM((M_HEADS // 2, 2 * M_QK_DIM, 2 * M_V_DIM), F32),
            pltpu.VMEM((2 * M_HEADS, M_CHUNK), F32),
        ],
        compiler_params=pltpu.CompilerParams(dimension_semantics=("parallel", "arbitrary"), vmem_limit_bytes=VMEM_LIMIT),
        name="mlstm",
    )(qk, mv, so, grow, brow, hn)


def _ff_chunks():
    return [(c0, min(c0 + FF_CHUNK, D_FF)) for c0 in range(0, D_FF, FF_CHUNK)]


def _mix_ffn_kernel(x_ref, a_ref, m_ref, gmix_ref, gpost_ref, gpre_ref, gffn_ref, wg_ref, wa_ref, wm_ref, wo_ref,
                    wfi_ref, wfo_ref, o_ref, acc_ref):
    tm = x_ref.shape[0]
    groups = [slice(i * tm // ROW_GROUPS, (i + 1) * tm // ROW_GROUPS) for i in range(ROW_GROUPS)]
    x = [x_ref[r, :] for r in groups]
    h = [(_rms(xi) * gmix_ref[...]).astype(BF16) for xi in x]
    a = [a_ref[r, :] for r in groups]
    m = [m_ref[r, :] for r in groups]
    parts = [[] for _ in groups]
    for c0 in range(0, D_MODEL, MERGE_CHUNK):
        cols = slice(c0, c0 + MERGE_CHUNK)
        gcols = slice(D_MODEL + c0, D_MODEL + c0 + MERGE_CHUNK)
        for i in range(ROW_GROUPS):
            g_attn = jax.nn.sigmoid(_dot(h[i], wg_ref[:, cols]))
            g_mlstm = jax.nn.sigmoid(_dot(h[i], wg_ref[:, gcols]))
            parts[i].append(
                (g_attn * _dot(a[i], wa_ref[:, cols]) + g_mlstm * _dot(m[i], wm_ref[:, cols])).astype(BF16))
    y = [_dot(jnp.concatenate(p, axis=1), wo_ref[...]) for p in parts]
    x1 = [xi + _rms(yi) * gpost_ref[...] for xi, yi in zip(x, y)]

    h2 = [(_rms(xi) * gpre_ref[...]).astype(BF16) for xi in x1]
    for c0, c1 in _ff_chunks():
        for i, r in enumerate(groups):
            gate = _dot(h2[i], wfi_ref[:, c0:c1])
            up = _dot(h2[i], wfi_ref[:, D_FF + c0:D_FF + c1])
            act = (gate * jax.nn.sigmoid(gate) * up).astype(BF16)
            part = _dot(act, wfo_ref[c0:c1, :])
            if c0 == 0:
                acc_ref[r, :] = part
            else:
                acc_ref[r, :] += part
    for i, r in enumerate(groups):
        o_ref[r, :] = x1[i] + _rms(acc_ref[r, :]) * gffn_ref[...]


def _mix_ffn(x2, attn, ml, gmix, gpost, gpre, gffn, wg, wa, wm, wo, wfi, wfo):
    T = x2.shape[0]
    tm = TOK_TILE
    row = lambda i: (i, 0)
    const = lambda i: (0, 0)

    def wspec(w):
        return pl.BlockSpec(w.shape, const, pipeline_mode=pl.Buffered(1))

    gain = pl.BlockSpec((1, D_MODEL), const)
    return pl.pallas_call(
        _mix_ffn_kernel,
        grid=(T // tm,),
        in_specs=[
            pl.BlockSpec((tm, D_MODEL), row),
            pl.BlockSpec((tm, ATTN_Q_W), row),
            pl.BlockSpec((tm, M_V_W), row),
            gain, gain, gain, gain,
            wspec(wg), wspec(wa), wspec(wm), wspec(wo), wspec(wfi), wspec(wfo),
        ],
        out_specs=pl.BlockSpec((tm, D_MODEL), row),
        out_shape=jax.ShapeDtypeStruct(x2.shape, F32),
        scratch_shapes=[pltpu.VMEM((tm, D_MODEL), F32)],
        compiler_params=pltpu.CompilerParams(dimension_semantics=("parallel",), vmem_limit_bytes=VMEM_LIMIT),
        name="mix_ffn",
    )(x2, attn, ml, gmix, gpost, gpre, gffn, wg, wa, wm, wo, wfi, wfo)


def _rope_tables(seq):
    f32 = np.float32
    inv_freq = (f32(ROPE_THETA) ** (-np.arange(0, HEAD_DIM, 2, dtype=f32) / f32(HEAD_DIM))).astype(f32)
    ang = np.arange(seq).astype(f32)[:, None] * inv_freq[None, :]
    emb = np.concatenate([ang, ang], axis=-1)
    cos = np.cos(emb).astype(f32)
    sin = np.sin(emb).astype(f32)
    first_half = np.arange(HEAD_DIM) < HEAD_DIM // 2
    sin_a = np.where(first_half, -sin, f32(0))
    sin_b = np.where(first_half, f32(0), sin)
    rep = LANES // HEAD_DIM
    return tuple(jnp.asarray(np.tile(t, (1, rep))) for t in (cos, sin_a, sin_b))


def _layer(x2, batch, seq, norm_pre_mix, norm_post_mix, norm_pre_ffn, norm_post_ffn, w_in, attn_sinks, conv_w,
           conv_b, b_igate, b_fgate, mlstm_head_norm, w_attn_branch, w_mlstm_branch, w_out, w_ffn_in, w_ffn_out):
    w_mix = w_in[:, :O_GATES].astype(BF16)
    wif = jnp.pad(w_in[:, O_GATES:O_BRANCH], ((0, 0), (0, LANES - 2 * M_HEADS))).astype(BF16)
    wg = w_in[:, O_BRANCH:].astype(BF16)
    cos, sa, sb = _rope_tables(seq)

    q, k, v, qk, mv, so, grow = _inproj(
        x2, norm_pre_mix[None, :], cos, sa, sb, conv_w, conv_b[None, :], w_mix, wif, seq)
    attn = _attention(attn_sinks, q, k, v, batch, seq)
    bias = jnp.concatenate([b_igate, b_fgate])
    ml = _mlstm(qk, mv, so, grow, bias[:, None], mlstm_head_norm[None, :], batch, seq)
    return _mix_ffn(x2, attn, ml, norm_pre_mix[None, :], norm_post_mix[None, :], norm_pre_ffn[None, :],
                    norm_post_ffn[None, :], wg, w_attn_branch.astype(BF16), w_mlstm_branch.astype(BF16),
                    w_out.astype(BF16), w_ffn_in.astype(BF16), w_ffn_out.astype(BF16))


def kernel(x, norm_pre_mix, norm_post_mix, norm_pre_ffn, norm_post_ffn, w_in, attn_sinks, conv_w, conv_b, b_igate,
           b_fgate, mlstm_head_norm, w_attn_branch, w_mlstm_branch, w_out, w_ffn_in, w_ffn_out):
    B, S, D = x.shape
    x2 = x.reshape(B * S, D)
    for l in range(w_in.shape[0]):
        x2 = _layer(x2, B, S, norm_pre_mix[l], norm_post_mix[l], norm_pre_ffn[l], norm_post_ffn[l], w_in[l],
                    attn_sinks[l], conv_w[l], conv_b[l], b_igate[l], b_fgate[l], mlstm_head_norm[l],
                    w_attn_branch[l], w_mlstm_branch[l], w_out[l], w_ffn_in[l], w_ffn_out[l])
    return x2.reshape(B, S, D)
```

```python
import functools

import numpy as np

import jax
import jax.numpy as jnp
from jax import lax
from jax.experimental import pallas as pl
from jax.experimental.pallas import tpu as pltpu

D_MODEL = 1024
HEAD_DIM = 64
N_Q_HEADS = 8
N_KV_HEADS = 2
WINDOW = 128
ATTN_BLOCK = 128
ROPE_THETA = 10000.0
M_HEADS = 4
M_QK_DIM = 64
M_V_DIM = 128
M_CHUNK = 128
CONV_WIDTH = 4
D_FF = 2816
EPS = 1e-6

ATTN_Q_W = N_Q_HEADS * HEAD_DIM
ATTN_KV_W = N_KV_HEADS * HEAD_DIM
M_QK_W = M_HEADS * M_QK_DIM
M_V_W = M_HEADS * M_V_DIM
KV_BD_W = 2 * ATTN_KV_W * 2
O_MQK = ATTN_Q_W + 2 * ATTN_KV_W
O_MV = O_MQK + 2 * M_QK_W
O_GATES = O_MV + 2 * M_V_W
O_BRANCH = O_GATES + 2 * M_HEADS

LANES = 128
SUBLANES = 8
VMEM_LIMIT = 56 * 1024 * 1024
CONV_HALO = SUBLANES

TOK_TILE = 512
ATTN_TILE = 1024
MLSTM_TILE = 1024
FF_CHUNK = 512
MERGE_CHUNK = 256
ROW_GROUPS = 1

F32 = jnp.float32
BF16 = jnp.bfloat16


def _dot(a, b):
    return jnp.dot(a, b, preferred_element_type=F32)


def _dot_nt(a, b):
    return lax.dot_general(a, b, (((1,), (1,)), ((), ())), preferred_element_type=F32)


def _rms(x):
    return x * lax.rsqrt(jnp.mean(x * x, axis=-1, keepdims=True) + EPS)


def _split3(x):
    hi = x.astype(BF16)
    r1 = x - hi.astype(F32)
    mid = r1.astype(BF16)
    lo = (r1 - mid.astype(F32)).astype(BF16)
    return hi, mid, lo


def _inproj_kernel(x_ref, g_ref, cos_ref, sa_ref, sb_ref, cw_ref, cb_ref, w_ref, wif_ref,
                   q_ref, k_ref, v_ref, qk_ref, mv_ref, so_ref, grow_ref, cbuf, *, n_seq_tiles):
    tm = x_ref.shape[0]
    h = (_rms(x_ref[...]) * g_ref[...]).astype(BF16)
    cos = cos_ref[...]
    sa = sa_ref[...]
    sb = sb_ref[...]
    lane = lax.broadcasted_iota(jnp.int32, cos.shape, 1)
    lo = lane < HEAD_DIM

    def rope(t):
        return t * cos + pltpu.roll(t, LANES - HEAD_DIM // 2, 1) * sa + pltpu.roll(t, HEAD_DIM // 2, 1) * sb

    def store_block_diag(ref, t):
        tb = t.astype(BF16)
        rb = pltpu.roll(t, HEAD_DIM, 1).astype(BF16)
        zero = jnp.zeros((), BF16)
        for i, (src, keep_lo) in enumerate(((tb, True), (rb, False), (rb, True), (tb, False))):
            ref[:, i * LANES:(i + 1) * LANES] = jnp.where(lo, src, zero) if keep_lo else jnp.where(lo, zero, src)

    def attn_proj():
        a = _dot(h, w_ref[:, :O_MQK])
        for j in range(ATTN_Q_W // LANES):
            q_ref[:, j * LANES:(j + 1) * LANES] = (
                rope(a[:, j * LANES:(j + 1) * LANES]) * (HEAD_DIM ** -0.5)).astype(BF16)
        store_block_diag(k_ref, rope(a[:, ATTN_Q_W:ATTN_Q_W + LANES]))
        store_block_diag(v_ref, a[:, ATTN_Q_W + LANES:ATTN_Q_W + 2 * LANES])
        grow_ref[...] = _dot(h, wif_ref[...]).T[:2 * M_HEADS, :]

    @pl.when(pl.program_id(0) % n_seq_tiles == 0)
    def _():
        cbuf[0:CONV_HALO, :] = jnp.zeros((CONV_HALO, cbuf.shape[1]), F32)

    def mqk_proj(c0, c1):
        cbuf[CONV_HALO:CONV_HALO + tm, c0:c1] = _dot(h, w_ref[:, O_MQK + c0:O_MQK + c1])

    def conv_chunk(j0):
        cols = slice(j0 * LANES, (j0 + 1) * LANES)
        y = cb_ref[:, cols]
        for j in range(CONV_WIDTH):
            off = CONV_HALO - (CONV_WIDTH - 1) + j
            y = y + cw_ref[j:j + 1, cols] * cbuf[off:off + tm, cols]
        cbuf[0:CONV_HALO, cols] = cbuf[tm:tm + CONV_HALO, cols]
        y = y * jax.nn.sigmoid(y)
        if j0 * LANES >= M_QK_W:
            y = y * (M_QK_DIM ** -0.5)
        qk_ref[:, cols] = y.astype(BF16)

    def mlstm_vo_proj():
        vo = _dot(h, w_ref[:, O_MV:O_GATES])
        mv_ref[...] = vo[:, :M_V_W].astype(BF16)
        so_ref[...] = jax.nn.sigmoid(vo[:, M_V_W:]).astype(BF16)

    mqk_proj(0, M_QK_W)
    conv_chunk(0)
    conv_chunk(1)
    attn_proj()
    mqk_proj(M_QK_W, 2 * M_QK_W)
    conv_chunk(2)
    conv_chunk(3)
    mlstm_vo_proj()


def _inproj(x2, g, cos, sa, sb, cw, cb, w, wif, seq):
    T = x2.shape[0]
    tm = TOK_TILE
    n_seq_tiles = seq // tm
    row = lambda i: (i, 0)
    pos = lambda i: (i % n_seq_tiles, 0)
    const = lambda i: (0, 0)

    def wspec(w):
        return pl.BlockSpec(w.shape, const, pipeline_mode=pl.Buffered(1))

    out_shape = (
        jax.ShapeDtypeStruct((T, ATTN_Q_W), BF16),
        jax.ShapeDtypeStruct((T, KV_BD_W), BF16),
        jax.ShapeDtypeStruct((T, KV_BD_W), BF16),
        jax.ShapeDtypeStruct((T, 2 * M_QK_W), BF16),
        jax.ShapeDtypeStruct((T, M_V_W), BF16),
        jax.ShapeDtypeStruct((T, M_V_W), BF16),
        jax.ShapeDtypeStruct((2 * M_HEADS, T), F32),
    )
    out_specs = (
        pl.BlockSpec((tm, ATTN_Q_W), row),
        pl.BlockSpec((tm, KV_BD_W), row),
        pl.BlockSpec((tm, KV_BD_W), row),
        pl.BlockSpec((tm, 2 * M_QK_W), row),
        pl.BlockSpec((tm, M_V_W), row),
        pl.BlockSpec((tm, M_V_W), row),
        pl.BlockSpec((2 * M_HEADS, tm), lambda i: (0, i)),
    )
    in_specs = [
        pl.BlockSpec((tm, D_MODEL), row),
        pl.BlockSpec((1, D_MODEL), const),
        pl.BlockSpec((tm, LANES), pos),
        pl.BlockSpec((tm, LANES), pos),
        pl.BlockSpec((tm, LANES), pos),
        pl.BlockSpec(cw.shape, const),
        pl.BlockSpec(cb.shape, const),
        wspec(w), wspec(wif),
    ]
    return pl.pallas_call(
        functools.partial(_inproj_kernel, n_seq_tiles=n_seq_tiles),
        grid=(T // tm,),
        in_specs=in_specs,
        out_specs=out_specs,
        out_shape=out_shape,
        scratch_shapes=[pltpu.VMEM((CONV_HALO + tm, 2 * M_QK_W), F32)],
        compiler_params=pltpu.CompilerParams(dimension_semantics=("arbitrary",), vmem_limit_bytes=VMEM_LIMIT),
        name="inproj",
    )(x2, g, cos, sa, sb, cw, cb, w, wif)


def _attn_kernel(sink_ref, q_ref, kc_ref, kp_ref, vc_ref, vp_ref, o_ref):
    t = pl.program_id(1)
    blk = ATTN_BLOCK
    n_blk = q_ref.shape[0] // blk
    lo_q = lax.broadcasted_iota(jnp.int32, (blk, LANES), 1) < HEAD_DIM
    ri = lax.broadcasted_iota(jnp.int32, (blk, blk), 0)
    ci = lax.broadcasted_iota(jnp.int32, (blk, blk), 1)
    own = ci <= ri
    prev_ok = ci > ri + jnp.where(t > 0, 0, blk)
    zero = jnp.zeros((), BF16)
    row_is_a = lax.broadcasted_iota(jnp.int32, (4 * blk, LANES), 0) < 2 * blk
    lane_is_a = lax.broadcasted_iota(jnp.int32, (4 * blk, LANES), 1) < HEAD_DIM
    ones_bd = (row_is_a == lane_is_a).astype(BF16)

    for n in range(n_blk):
        rows = slice(n * blk, (n + 1) * blk)
        k_prev = kp_ref if n == 0 else kc_ref.at[(n - 1) * blk:n * blk, :]
        v_prev = vp_ref if n == 0 else vc_ref.at[(n - 1) * blk:n * blk, :]
        for j in range(N_KV_HEADS):
            kv_cols = [slice((2 * j + i) * LANES, (2 * j + i + 1) * LANES) for i in range(2)]
            k_bd = jnp.concatenate([x for c in kv_cols for x in (k_prev[:, c], kc_ref[rows, c])], axis=0)
            v_bd = jnp.concatenate([x for c in kv_cols for x in (v_prev[:, c], vc_ref[rows, c])], axis=0)
            c0 = 2 * j * LANES
            q2 = jnp.concatenate([q_ref[rows, c0:c0 + LANES], q_ref[rows, c0 + LANES:c0 + 2 * LANES]], axis=0)
            s = _dot_nt(q2, k_bd)
            e_rows, sink_rows = [], []
            for p in range(2):
                e_cols, sink_cols = [], []
                for hh in range(2):
                    sink = sink_ref[4 * j + 2 * p + hh]
                    s_prev = s[p * blk:(p + 1) * blk, 2 * hh * blk:(2 * hh + 1) * blk]
                    s_own = s[p * blk:(p + 1) * blk, (2 * hh + 1) * blk:(2 * hh + 2) * blk]
                    if n == 0:
                        s_prev = jnp.where(prev_ok, s_prev, -jnp.inf)
                    sc = jnp.where(own, s_own, s_prev)
                    m = jnp.maximum(jnp.max(sc, axis=-1, keepdims=True), sink)
                    eb = jnp.exp(sc - m).astype(BF16)
                    e_cols += [jnp.where(own, zero, eb), jnp.where(own, eb, zero)]
                    sink_cols.append(jnp.exp(sink - m))
                e_rows.append(jnp.concatenate(e_cols, axis=1))
                sink_rows.append(jnp.where(lo_q, sink_cols[0], sink_cols[1]))
            o = _dot(jnp.concatenate(e_rows, axis=0), jnp.concatenate([v_bd, ones_bd], axis=1))
            for p in range(2):
                op = o[p * blk:(p + 1) * blk]
                o_ref[rows, c0 + p * LANES:c0 + (p + 1) * LANES] = (
                    op[:, :LANES] / (op[:, LANES:] + sink_rows[p])).astype(BF16)


def _attention(sinks, q, k, v, batch, seq):
    tq = ATTN_TILE
    nt = seq // tq
    per = tq // ATTN_BLOCK
    cur = lambda b, t: (b * nt + t, 0)
    prev = lambda b, t: (jnp.maximum((b * nt + t) * per - 1, 0), 0)
    return pl.pallas_call(
        _attn_kernel,
        grid=(batch, nt),
        in_specs=[
            pl.BlockSpec(memory_space=pltpu.SMEM),
            pl.BlockSpec((tq, ATTN_Q_W), cur),
            pl.BlockSpec((tq, KV_BD_W), cur),
            pl.BlockSpec((ATTN_BLOCK, KV_BD_W), prev),
            pl.BlockSpec((tq, KV_BD_W), cur),
            pl.BlockSpec((ATTN_BLOCK, KV_BD_W), prev),
        ],
        out_specs=pl.BlockSpec((tq, ATTN_Q_W), cur),
        out_shape=jax.ShapeDtypeStruct(q.shape, BF16),
        compiler_params=pltpu.CompilerParams(dimension_semantics=("parallel", "parallel"), vmem_limit_bytes=VMEM_LIMIT),
        name="swa_attention",
    )(sinks, q, k, k, v, v)


def _mlstm_kernel(qk_ref, mv_ref, so_ref, grow_ref, gnext_ref, brow_ref, hn_ref, out_ref, state, m_scr, gate_scr):
    t = pl.program_id(1)
    TL = qk_ref.shape[0]
    L = M_CHUNK
    R = 2 * M_HEADS

    ri = lax.broadcasted_iota(jnp.int32, (L, L), 0)
    ci = lax.broadcasted_iota(jnp.int32, (L, L), 1)
    causal = ci <= ri
    triu = (ri <= ci).astype(BF16)
    lane_in_chunk = lax.broadcasted_iota(jnp.int32, (R, TL), 1) & (L - 1)
    lo_b = lax.broadcasted_iota(jnp.int32, (L, LANES), 1) < M_QK_DIM
    top = lax.broadcasted_iota(jnp.int32, (L, LANES), 0) < M_QK_DIM
    top2 = lax.broadcasted_iota(jnp.int32, (L, 2 * M_V_DIM), 0) < M_QK_DIM
    ones_v = jnp.ones((L, M_V_DIM), BF16)
    zero_b = jnp.zeros((), BF16)
    chunk_rows = [slice(c * L, (c + 1) * L) for c in range(TL // L)]

    def row_bcast(x8, h, n=L):
        return jnp.broadcast_to(x8[h:h + 1, :], (n, x8.shape[1]))

    def gate_scan(g_ref):
        gr = g_ref[...] + brow_ref[...]
        lf = jax.nn.log_sigmoid(gr)
        b8 = jnp.concatenate([sum(_dot(part, triu) for part in _split3(lf[:, rows])) for rows in chunk_rows], axis=1)
        b_all = pltpu.roll(b8, M_HEADS, 0)
        r_all = gr - b_all
        cm_all = r_all
        sh = 1
        while sh < L:
            cm_all = jnp.where(lane_in_chunk >= sh, jnp.maximum(cm_all, pltpu.roll(cm_all, sh, 1)), cm_all)
            sh *= 2
        return b_all, r_all, cm_all

    @pl.when(t == 0)
    def _():
        state[...] = jnp.zeros(state.shape, F32)
        m_scr[...] = jnp.zeros(m_scr.shape, F32)
        for i, x in enumerate(gate_scan(grow_ref)):
            gate_scr[i] = x

    b_all, r_all, cm_all = gate_scr[0], gate_scr[1], gate_scr[2]
    next_gates = gate_scan(gnext_ref)

    m = m_scr[...]
    gates = []
    for rows in chunk_rows:
        b, r, cm = b_all[:, rows], r_all[:, rows], cm_all[:, rows]
        rmax = jnp.broadcast_to(cm[:, L - 1:L], (R, L))
        g = jnp.broadcast_to(b[:, L - 1:L], (R, L))
        big_m = jnp.maximum(m, cm)
        inter = jnp.exp(m - big_m)
        emt = jnp.exp(-(b + big_m))
        w = jnp.exp(r - rmax)
        m_loc = g + rmax
        m_new = jnp.maximum(g + m, m_loc)
        decay = jnp.exp(g + m - m_new)
        scale = jnp.exp(m_loc - m_new)
        m = m_new
        gates.append((r, w, decay, scale, big_m, inter, emt))
    m_scr[...] = m

    for rows, (r, w, decay, scale, big_m, inter, emt) in zip(chunk_rows, gates):
        for p in range(M_HEADS // 2):
            pc = slice(p * LANES, (p + 1) * LANES)
            q_pair_b = qk_ref[rows, pc]
            k_pair_b = qk_ref[rows, M_QK_W + p * LANES:M_QK_W + (p + 1) * LANES]
            c_prev = state[p]
            c_prev_b = c_prev.astype(BF16)
            inter_cols = jnp.where(top, row_bcast(inter, 2 * p), row_bcast(inter, 2 * p + 1)).T
            qs_pair = (q_pair_b.astype(F32) * inter_cols).astype(BF16)
            v_exts = []
            for hh in range(2):
                hd = 2 * p + hh
                cols = slice(hd * M_V_DIM, (hd + 1) * M_V_DIM)
                sel = lo_b if hh == 0 else jnp.logical_not(lo_b)
                v_ext = jnp.concatenate([mv_ref[rows, cols], ones_v], axis=1)
                v_exts.append(v_ext)
                m_col, emt_col = (row_bcast(x8, hd).T for x8 in (big_m, emt))

                d_mat = jnp.exp(jnp.where(causal, row_bcast(r, hd), -jnp.inf) - m_col)
                k_h = jnp.where(sel, k_pair_b, zero_b)
                s_mat = (_dot_nt(q_pair_b, k_h) * d_mat).astype(BF16)
                qs = jnp.where(sel, qs_pair, zero_b)
                nd = _dot(jnp.concatenate([s_mat, qs], axis=1), jnp.concatenate([v_ext, c_prev_b], axis=0))
                hcell = nd[:, :M_V_DIM] / jnp.maximum(jnp.abs(nd[:, M_V_DIM:]), emt_col)
                cell = _rms(hcell) * hn_ref[:, cols]
                out_ref[rows, cols] = (so_ref[rows, cols].astype(F32) * cell).astype(BF16)

            k_t = k_pair_b.astype(F32).T
            w_s = jnp.where(top, row_bcast(w, 2 * p), row_bcast(w, 2 * p + 1))
            kw_t = (k_t * w_s).astype(BF16)
            lhs = jnp.concatenate([jnp.where(top, kw_t, zero_b), jnp.where(top, zero_b, kw_t)], axis=1)
            a = _dot(lhs, jnp.concatenate(v_exts, axis=0))
            dec = jnp.where(top2, row_bcast(decay, 2 * p, L)[:, :1], row_bcast(decay, 2 * p + 1, L)[:, :1])
            sc = jnp.where(top2, row_bcast(scale, 2 * p, L)[:, :1], row_bcast(scale, 2 * p + 1, L)[:, :1])
            state[p] = dec * c_prev + sc * a

    for i, x in enumerate(next_gates):
        gate_scr[i] = x


def _mlstm(qk, mv, so, grow, brow, hn, batch, seq):
    tl = MLSTM_TILE
    nt = seq // tl
    cur = lambda b, t: (b * nt + t, 0)
    const = lambda b, t: (0, 0)
    return pl.pallas_call(
        _mlstm_kernel,
        grid=(batch, nt),
        in_specs=[
            pl.BlockSpec((tl, 2 * M_QK_W), cur),
            pl.BlockSpec((tl, M_V_W), cur),
            pl.BlockSpec((tl, M_V_W), cur),
            pl.BlockSpec((2 * M_HEADS, tl), lambda b, t: (0, b * nt + t)),
            pl.BlockSpec((2 * M_HEADS, tl), lambda b, t: (0, b * nt + jnp.minimum(t + 1, nt - 1))),
            pl.BlockSpec(brow.shape, const),
            pl.BlockSpec(hn.shape, const),
        ],
        out_specs=pl.BlockSpec((tl, M_V_W), cur),
        out_shape=jax.ShapeDtypeStruct(mv.shape, BF16),
        scratch_shapes=[
            pltpu.VMEM((M_HEADS // 2, 2 * M_QK_DIM, 2 * M_V_DIM), F32),
            pltpu.VMEM((2 * M_HEADS, M_CHUNK), F32),
            pltpu.VMEM((3, 2 * M_HEADS, tl), F32),
        ],
        compiler_params=pltpu.CompilerParams(dimension_semantics=("parallel", "arbitrary"), vmem_limit_bytes=VMEM_LIMIT),
        name="mlstm",
    )(qk, mv, so, grow, grow, brow, hn)


def _ff_chunks():
    return [(c0, min(c0 + FF_CHUNK, D_FF)) for c0 in range(0, D_FF, FF_CHUNK)]


def _mix_ffn_kernel(x_ref, a_ref, m_ref, gmix_ref, gpost_ref, gpre_ref, gffn_ref, wg_ref, wa_ref, wm_ref, wo_ref,
                    wfi_ref, wfo_ref, o_ref, acc_ref):
    tm = x_ref.shape[0]
    groups = [slice(i * tm // ROW_GROUPS, (i + 1) * tm // ROW_GROUPS) for i in range(ROW_GROUPS)]
    x = [x_ref[r, :] for r in groups]
    h = [(_rms(xi) * gmix_ref[...]).astype(BF16) for xi in x]
    a = [a_ref[r, :] for r in groups]
    m = [m_ref[r, :] for r in groups]
    parts = [[] for _ in groups]
    for c0 in range(0, D_MODEL, MERGE_CHUNK):
        cols = slice(c0, c0 + MERGE_CHUNK)
        gcols = slice(D_MODEL + c0, D_MODEL + c0 + MERGE_CHUNK)
        for i in range(ROW_GROUPS):
            g_attn = jax.nn.sigmoid(_dot(h[i], wg_ref[:, cols]))
            g_mlstm = jax.nn.sigmoid(_dot(h[i], wg_ref[:, gcols]))
            parts[i].append(
                (g_attn * _dot(a[i], wa_ref[:, cols]) + g_mlstm * _dot(m[i], wm_ref[:, cols])).astype(BF16))
    y = [_dot(jnp.concatenate(p, axis=1), wo_ref[...]) for p in parts]
    x1 = [xi + _rms(yi) * gpost_ref[...] for xi, yi in zip(x, y)]

    h2 = [(_rms(xi) * gpre_ref[...]).astype(BF16) for xi in x1]
    for c0, c1 in _ff_chunks():
        for i, r in enumerate(groups):
            gate = _dot(h2[i], wfi_ref[:, c0:c1])
            up = _dot(h2[i], wfi_ref[:, D_FF + c0:D_FF + c1])
            act = (gate * jax.nn.sigmoid(gate) * up).astype(BF16)
            part = _dot(act, wfo_ref[c0:c1, :])
            if c0 == 0:
                acc_ref[r, :] = part
            else:
                acc_ref[r, :] += part
    for i, r in enumerate(groups):
        o_ref[r, :] = x1[i] + _rms(acc_ref[r, :]) * gffn_ref[...]


def _mix_ffn(x2, attn, ml, gmix, gpost, gpre, gffn, wg, wa, wm, wo, wfi, wfo):
    T = x2.shape[0]
    tm = TOK_TILE
    row = lambda i: (i, 0)
    const = lambda i: (0, 0)

    def wspec(w):
        return pl.BlockSpec(w.shape, const, pipeline_mode=pl.Buffered(1))

    gain = pl.BlockSpec((1, D_MODEL), const)
    return pl.pallas_call(
        _mix_ffn_kernel,
        grid=(T // tm,),
        in_specs=[
            pl.BlockSpec((tm, D_MODEL), row),
            pl.BlockSpec((tm, ATTN_Q_W), row),
            pl.BlockSpec((tm, M_V_W), row),
            gain, gain, gain, gain,
            wspec(wg), wspec(wa), wspec(wm), wspec(wo), wspec(wfi), wspec(wfo),
        ],
        out_specs=pl.BlockSpec((tm, D_MODEL), row),
        out_shape=jax.ShapeDtypeStruct(x2.shape, F32),
        scratch_shapes=[pltpu.VMEM((tm, D_MODEL), F32)],
        compiler_params=pltpu.CompilerParams(dimension_semantics=("parallel",), vmem_limit_bytes=VMEM_LIMIT),
        name="mix_ffn",
    )(x2, attn, ml, gmix, gpost, gpre, gffn, wg, wa, wm, wo, wfi, wfo)


def _rope_tables(seq):
    f32 = np.float32
    inv_freq = (f32(ROPE_THETA) ** (-np.arange(0, HEAD_DIM, 2, dtype=f32) / f32(HEAD_DIM))).astype(f32)
    ang = np.arange(seq).astype(f32)[:, None] * inv_freq[None, :]
    emb = np.concatenate([ang, ang], axis=-1)
    cos = np.cos(emb).astype(f32)
    sin = np.sin(emb).astype(f32)
    first_half = np.arange(HEAD_DIM) < HEAD_DIM // 2
    sin_a = np.where(first_half, -sin, f32(0))
    sin_b = np.where(first_half, f32(0), sin)
    rep = LANES // HEAD_DIM
    return tuple(jnp.asarray(np.tile(t, (1, rep))) for t in (cos, sin_a, sin_b))


def _layer(x2, batch, seq, norm_pre_mix, norm_post_mix, norm_pre_ffn, norm_post_ffn, w_in, attn_sinks, conv_w,
           conv_b, b_igate, b_fgate, mlstm_head_norm, w_attn_branch, w_mlstm_branch, w_out, w_ffn_in, w_ffn_out):
    w_mix = w_in[:, :O_GATES].astype(BF16)
    wif = jnp.pad(w_in[:, O_GATES:O_BRANCH], ((0, 0), (0, LANES - 2 * M_HEADS))).astype(BF16)
    wg = w_in[:, O_BRANCH:].astype(BF16)
    cos, sa, sb = _rope_tables(seq)

    q, k, v, qk, mv, so, grow = _inproj(
        x2, norm_pre_mix[None, :], cos, sa, sb, conv_w, conv_b[None, :], w_mix, wif, seq)
    attn = _attention(attn_sinks, q, k, v, batch, seq)
    bias = jnp.concatenate([b_igate, b_fgate])
    ml = _mlstm(qk, mv, so, grow, bias[:, None], mlstm_head_norm[None, :], batch, seq)
    return _mix_ffn(x2, attn, ml, norm_pre_mix[None, :], norm_post_mix[None, :], norm_pre_ffn[None, :],
                    norm_post_ffn[None, :], wg, w_attn_branch.astype(BF16), w_mlstm_branch.astype(BF16),
                    w_out.astype(BF16), w_ffn_in.astype(BF16), w_ffn_out.astype(BF16))


def kernel(x, norm_pre_mix, norm_post_mix, norm_pre_ffn, norm_post_ffn, w_in, attn_sinks, conv_w, conv_b, b_igate,
           b_fgate, mlstm_head_norm, w_attn_branch, w_mlstm_branch, w_out, w_ffn_in, w_ffn_out):
    B, S, D = x.shape
    x2 = x.reshape(B * S, D)
    for l in range(w_in.shape[0]):
        x2 = _layer(x2, B, S, norm_pre_mix[l], norm_post_mix[l], norm_pre_ffn[l], norm_post_ffn[l], w_in[l],
                    attn_sinks[l], conv_w[l], conv_b[l], b_igate[l], b_fgate[l], mlstm_head_norm[l],
                    w_attn_branch[l], w_mlstm_branch[l], w_out[l], w_ffn_in[l], w_ffn_out[l])
    return x2.reshape(B, S, D)
```

```python
import functools

import numpy as np

import jax
import jax.numpy as jnp
from jax import lax
from jax.experimental import pallas as pl
from jax.experimental.pallas import tpu as pltpu

D_MODEL = 1024
HEAD_DIM = 64
N_Q_HEADS = 8
N_KV_HEADS = 2
WINDOW = 128
ATTN_BLOCK = 128
ROPE_THETA = 10000.0
M_HEADS = 4
M_QK_DIM = 64
M_V_DIM = 128
M_CHUNK = 128
CONV_WIDTH = 4
D_FF = 2816
EPS = 1e-6

ATTN_Q_W = N_Q_HEADS * HEAD_DIM
ATTN_KV_W = N_KV_HEADS * HEAD_DIM
M_QK_W = M_HEADS * M_QK_DIM
M_V_W = M_HEADS * M_V_DIM
KV_BD_W = 2 * ATTN_KV_W * 2
O_MQK = ATTN_Q_W + 2 * ATTN_KV_W
O_MV = O_MQK + 2 * M_QK_W
O_GATES = O_MV + 2 * M_V_W
O_BRANCH = O_GATES + 2 * M_HEADS

LANES = 128
SUBLANES = 8
VMEM_LIMIT = 56 * 1024 * 1024
CONV_HALO = SUBLANES
W_MIX = 2 * D_MODEL
W_IF = W_MIX + O_GATES
W_COLS = W_IF + LANES

TOK_TILE = 512
INPROJ_TILE = 1024
ATTN_TILE = 1024
MLSTM_TILE = 1024
FF_CHUNK = 512
MERGE_CHUNK = 256

F32 = jnp.float32
BF16 = jnp.bfloat16


def _dot(a, b):
    return jnp.dot(a, b, preferred_element_type=F32)


def _dot_nt(a, b):
    return lax.dot_general(a, b, (((1,), (1,)), ((), ())), preferred_element_type=F32)


def _rms(x):
    return x * lax.rsqrt(jnp.mean(x * x, axis=-1, keepdims=True) + EPS)


def _split3(x):
    hi = x.astype(BF16)
    r1 = x - hi.astype(F32)
    mid = r1.astype(BF16)
    lo = (r1 - mid.astype(F32)).astype(BF16)
    return hi, mid, lo


def _inproj_kernel(x_ref, g_ref, cos_ref, sa_ref, sb_ref, cw_ref, cb_ref, w_ref,
                   q_ref, k_ref, v_ref, qk_ref, mv_ref, so_ref, grow_ref, cbuf, *, n_seq_tiles):
    tm = x_ref.shape[0]
    h = (_rms(x_ref[...]) * g_ref[...]).astype(BF16)
    cos = cos_ref[...]
    sa = sa_ref[...]
    sb = sb_ref[...]
    lane = lax.broadcasted_iota(jnp.int32, cos.shape, 1)
    lo = lane < HEAD_DIM

    def rope(t):
        return t * cos + pltpu.roll(t, LANES - HEAD_DIM // 2, 1) * sa + pltpu.roll(t, HEAD_DIM // 2, 1) * sb

    def store_block_diag(ref, t):
        tb = t.astype(BF16)
        rb = pltpu.roll(t, HEAD_DIM, 1).astype(BF16)
        zero = jnp.zeros((), BF16)
        for i, (src, keep_lo) in enumerate(((tb, True), (rb, False), (rb, True), (tb, False))):
            ref[:, i * LANES:(i + 1) * LANES] = jnp.where(lo, src, zero) if keep_lo else jnp.where(lo, zero, src)

    def attn_proj():
        a = _dot(h, w_ref[:, W_MIX:W_MIX + O_MQK])
        for j in range(ATTN_Q_W // LANES):
            q_ref[:, j * LANES:(j + 1) * LANES] = (
                rope(a[:, j * LANES:(j + 1) * LANES]) * (HEAD_DIM ** -0.5)).astype(BF16)
        store_block_diag(k_ref, rope(a[:, ATTN_Q_W:ATTN_Q_W + LANES]))
        store_block_diag(v_ref, a[:, ATTN_Q_W + LANES:ATTN_Q_W + 2 * LANES])
        grow_ref[...] = _dot(h, w_ref[:, W_IF:W_COLS]).T[:2 * M_HEADS, :]

    @pl.when(pl.program_id(0) % n_seq_tiles == 0)
    def _():
        cbuf[0:CONV_HALO, :] = jnp.zeros((CONV_HALO, cbuf.shape[1]), F32)

    def mqk_proj(c0, c1):
        cbuf[CONV_HALO:CONV_HALO + tm, c0:c1] = _dot(h, w_ref[:, W_MIX + O_MQK + c0:W_MIX + O_MQK + c1])

    def conv_chunk(j0):
        cols = slice(j0 * LANES, (j0 + 1) * LANES)
        y = cb_ref[:, cols]
        for j in range(CONV_WIDTH):
            off = CONV_HALO - (CONV_WIDTH - 1) + j
            y = y + cw_ref[j:j + 1, cols] * cbuf[off:off + tm, cols]
        cbuf[0:CONV_HALO, cols] = cbuf[tm:tm + CONV_HALO, cols]
        y = y * jax.nn.sigmoid(y)
        if j0 * LANES >= M_QK_W:
            y = y * (M_QK_DIM ** -0.5)
        qk_ref[:, cols] = y.astype(BF16)

    def mlstm_vo_proj():
        vo = _dot(h, w_ref[:, W_MIX + O_MV:W_IF])
        mv_ref[...] = vo[:, :M_V_W].astype(BF16)
        so_ref[...] = jax.nn.sigmoid(vo[:, M_V_W:]).astype(BF16)

    mqk_proj(0, M_QK_W)
    conv_chunk(0)
    conv_chunk(1)
    attn_proj()
    mqk_proj(M_QK_W, 2 * M_QK_W)
    conv_chunk(2)
    conv_chunk(3)
    mlstm_vo_proj()


def _inproj(x2, g, cos, sa, sb, cw, cb, w, seq):
    T = x2.shape[0]
    tm = INPROJ_TILE
    n_seq_tiles = seq // tm
    row = lambda i: (i, 0)
    pos = lambda i: (i % n_seq_tiles, 0)
    const = lambda i: (0, 0)

    def wspec(w):
        return pl.BlockSpec(w.shape, const, pipeline_mode=pl.Buffered(1))

    out_shape = (
        jax.ShapeDtypeStruct((T, ATTN_Q_W), BF16),
        jax.ShapeDtypeStruct((T, KV_BD_W), BF16),
        jax.ShapeDtypeStruct((T, KV_BD_W), BF16),
        jax.ShapeDtypeStruct((T, 2 * M_QK_W), BF16),
        jax.ShapeDtypeStruct((T, M_V_W), BF16),
        jax.ShapeDtypeStruct((T, M_V_W), BF16),
        jax.ShapeDtypeStruct((2 * M_HEADS, T), F32),
    )
    out_specs = (
        pl.BlockSpec((tm, ATTN_Q_W), row),
        pl.BlockSpec((tm, KV_BD_W), row),
        pl.BlockSpec((tm, KV_BD_W), row),
        pl.BlockSpec((tm, 2 * M_QK_W), row),
        pl.BlockSpec((tm, M_V_W), row),
        pl.BlockSpec((tm, M_V_W), row),
        pl.BlockSpec((2 * M_HEADS, tm), lambda i: (0, i)),
    )
    in_specs = [
        pl.BlockSpec((tm, D_MODEL), row),
        pl.BlockSpec((1, D_MODEL), const),
        pl.BlockSpec((tm, LANES), pos),
        pl.BlockSpec((tm, LANES), pos),
        pl.BlockSpec((tm, LANES), pos),
        pl.BlockSpec(cw.shape, const),
        pl.BlockSpec(cb.shape, const),
        wspec(w),
    ]
    return pl.pallas_call(
        functools.partial(_inproj_kernel, n_seq_tiles=n_seq_tiles),
        grid=(T // tm,),
        in_specs=in_specs,
        out_specs=out_specs,
        out_shape=out_shape,
        scratch_shapes=[pltpu.VMEM((CONV_HALO + tm, 2 * M_QK_W), F32)],
        compiler_params=pltpu.CompilerParams(dimension_semantics=("arbitrary",), vmem_limit_bytes=VMEM_LIMIT),
        name="inproj",
    )(x2, g, cos, sa, sb, cw, cb, w)


def _attn_kernel(sink_ref, q_ref, kc_ref, kp_ref, vc_ref, vp_ref, o_ref):
    t = pl.program_id(1)
    blk = ATTN_BLOCK
    n_blk = q_ref.shape[0] // blk
    lo_q = lax.broadcasted_iota(jnp.int32, (blk, LANES), 1) < HEAD_DIM
    ri = lax.broadcasted_iota(jnp.int32, (blk, blk), 0)
    ci = lax.broadcasted_iota(jnp.int32, (blk, blk), 1)
    own = ci <= ri
    prev_ok = ci > ri + jnp.where(t > 0, 0, blk)
    zero = jnp.zeros((), BF16)
    row_is_a = lax.broadcasted_iota(jnp.int32, (4 * blk, LANES), 0) < 2 * blk
    lane_is_a = lax.broadcasted_iota(jnp.int32, (4 * blk, LANES), 1) < HEAD_DIM
    ones_bd = (row_is_a == lane_is_a).astype(BF16)

    for n in range(n_blk):
        rows = slice(n * blk, (n + 1) * blk)
        k_prev = kp_ref if n == 0 else kc_ref.at[(n - 1) * blk:n * blk, :]
        v_prev = vp_ref if n == 0 else vc_ref.at[(n - 1) * blk:n * blk, :]
        for j in range(N_KV_HEADS):
            kv_cols = [slice((2 * j + i) * LANES, (2 * j + i + 1) * LANES) for i in range(2)]
            k_bd = jnp.concatenate([x for c in kv_cols for x in (k_prev[:, c], kc_ref[rows, c])], axis=0)
            v_bd = jnp.concatenate([x for c in kv_cols for x in (v_prev[:, c], vc_ref[rows, c])], axis=0)
            c0 = 2 * j * LANES
            q2 = jnp.concatenate([q_ref[rows, c0:c0 + LANES], q_ref[rows, c0 + LANES:c0 + 2 * LANES]], axis=0)
            s = _dot_nt(q2, k_bd)
            e_rows, sink_rows = [], []
            for p in range(2):
                e_cols, sink_cols = [], []
                for hh in range(2):
                    sink = sink_ref[4 * j + 2 * p + hh]
                    s_prev = s[p * blk:(p + 1) * blk, 2 * hh * blk:(2 * hh + 1) * blk]
                    s_own = s[p * blk:(p + 1) * blk, (2 * hh + 1) * blk:(2 * hh + 2) * blk]
                    if n == 0:
                        s_prev = jnp.where(prev_ok, s_prev, -jnp.inf)
                    sc = jnp.where(own, s_own, s_prev)
                    m = jnp.maximum(jnp.max(sc, axis=-1, keepdims=True), sink)
                    eb = jnp.exp(sc - m).astype(BF16)
                    e_cols += [jnp.where(own, zero, eb), jnp.where(own, eb, zero)]
                    sink_cols.append(jnp.exp(sink - m))
                e_rows.append(jnp.concatenate(e_cols, axis=1))
                sink_rows.append(jnp.where(lo_q, sink_cols[0], sink_cols[1]))
            o = _dot(jnp.concatenate(e_rows, axis=0), jnp.concatenate([v_bd, ones_bd], axis=1))
            for p in range(2):
                op = o[p * blk:(p + 1) * blk]
                o_ref[rows, c0 + p * LANES:c0 + (p + 1) * LANES] = (
                    op[:, :LANES] / (op[:, LANES:] + sink_rows[p])).astype(BF16)


def _attention(sinks, q, k, v, batch, seq):
    tq = ATTN_TILE
    nt = seq // tq
    per = tq // ATTN_BLOCK
    cur = lambda b, t: (b * nt + t, 0)
    prev = lambda b, t: (jnp.maximum((b * nt + t) * per - 1, 0), 0)
    return pl.pallas_call(
        _attn_kernel,
        grid=(batch, nt),
        in_specs=[
            pl.BlockSpec(memory_space=pltpu.SMEM),
            pl.BlockSpec((tq, ATTN_Q_W), cur),
            pl.BlockSpec((tq, KV_BD_W), cur),
            pl.BlockSpec((ATTN_BLOCK, KV_BD_W), prev),
            pl.BlockSpec((tq, KV_BD_W), cur),
            pl.BlockSpec((ATTN_BLOCK, KV_BD_W), prev),
        ],
        out_specs=pl.BlockSpec((tq, ATTN_Q_W), cur),
        out_shape=jax.ShapeDtypeStruct(q.shape, BF16),
        compiler_params=pltpu.CompilerParams(dimension_semantics=("parallel", "parallel"), vmem_limit_bytes=VMEM_LIMIT),
        name="swa_attention",
    )(sinks, q, k, k, v, v)


def _mlstm_kernel(qk_ref, mv_ref, so_ref, grow_ref, gnext_ref, brow_ref, hn_ref, out_ref, state, m_scr, gate_scr):
    t = pl.program_id(1)
    TL = qk_ref.shape[0]
    L = M_CHUNK
    R = 2 * M_HEADS

    ri = lax.broadcasted_iota(jnp.int32, (L, L), 0)
    ci = lax.broadcasted_iota(jnp.int32, (L, L), 1)
    causal = ci <= ri
    triu = (ri <= ci).astype(BF16)
    lane_in_chunk = lax.broadcasted_iota(jnp.int32, (R, TL), 1) & (L - 1)
    lo_b = lax.broadcasted_iota(jnp.int32, (L, LANES), 1) < M_QK_DIM
    top = lax.broadcasted_iota(jnp.int32, (L, LANES), 0) < M_QK_DIM
    top2 = lax.broadcasted_iota(jnp.int32, (L, 2 * M_V_DIM), 0) < M_QK_DIM
    ones_v = jnp.ones((L, M_V_DIM), BF16)
    zero_b = jnp.zeros((), BF16)
    chunk_rows = [slice(c * L, (c + 1) * L) for c in range(TL // L)]

    def row_bcast(x8, h, n=L):
        return jnp.broadcast_to(x8[h:h + 1, :], (n, x8.shape[1]))

    def gate_scan(g_ref):
        gr = g_ref[...] + brow_ref[...]
        lf = jax.nn.log_sigmoid(gr)
        b8 = jnp.concatenate([sum(_dot(part, triu) for part in _split3(lf[:, rows])) for rows in chunk_rows], axis=1)
        b_all = pltpu.roll(b8, M_HEADS, 0)
        r_all = gr - b_all
        cm_all = r_all
        sh = 1
        while sh < L:
            cm_all = jnp.where(lane_in_chunk >= sh, jnp.maximum(cm_all, pltpu.roll(cm_all, sh, 1)), cm_all)
            sh *= 2
        return b_all, r_all, cm_all

    @pl.when(t == 0)
    def _():
        state[...] = jnp.zeros(state.shape, F32)
        m_scr[...] = jnp.zeros(m_scr.shape, F32)
        for i, x in enumerate(gate_scan(grow_ref)):
            gate_scr[i] = x

    b_all, r_all, cm_all = gate_scr[0], gate_scr[1], gate_scr[2]
    next_gates = gate_scan(gnext_ref)

    m = m_scr[...]
    gates = []
    for rows in chunk_rows:
        b, r, cm = b_all[:, rows], r_all[:, rows], cm_all[:, rows]
        rmax = jnp.broadcast_to(cm[:, L - 1:L], (R, L))
        g = jnp.broadcast_to(b[:, L - 1:L], (R, L))
        big_m = jnp.maximum(m, cm)
        inter = jnp.exp(m - big_m)
        emt = jnp.exp(-(b + big_m))
        w = jnp.exp(r - rmax)
        m_loc = g + rmax
        m_new = jnp.maximum(g + m, m_loc)
        decay = jnp.exp(g + m - m_new)
        scale = jnp.exp(m_loc - m_new)
        m = m_new
        gates.append((r, w, decay, scale, big_m, inter, emt))
    m_scr[...] = m

    for rows, (r, w, decay, scale, big_m, inter, emt) in zip(chunk_rows, gates):
        for p in range(M_HEADS // 2):
            pc = slice(p * LANES, (p + 1) * LANES)
            q_pair_b = qk_ref[rows, pc]
            k_pair_b = qk_ref[rows, M_QK_W + p * LANES:M_QK_W + (p + 1) * LANES]
            c_prev = state[p]
            c_prev_b = c_prev.astype(BF16)
            inter_cols = jnp.where(top, row_bcast(inter, 2 * p), row_bcast(inter, 2 * p + 1)).T
            qs_pair = (q_pair_b.astype(F32) * inter_cols).astype(BF16)
            v_exts = []
            for hh in range(2):
                hd = 2 * p + hh
                cols = slice(hd * M_V_DIM, (hd + 1) * M_V_DIM)
                sel = lo_b if hh == 0 else jnp.logical_not(lo_b)
                v_ext = jnp.concatenate([mv_ref[rows, cols], ones_v], axis=1)
                v_exts.append(v_ext)
                m_col, emt_col = (row_bcast(x8, hd).T for x8 in (big_m, emt))

                d_mat = jnp.exp(jnp.where(causal, row_bcast(r, hd), -jnp.inf) - m_col)
                k_h = jnp.where(sel, k_pair_b, zero_b)
                s_mat = (_dot_nt(q_pair_b, k_h) * d_mat).astype(BF16)
                qs = jnp.where(sel, qs_pair, zero_b)
                nd = _dot(jnp.concatenate([s_mat, qs], axis=1), jnp.concatenate([v_ext, c_prev_b], axis=0))
                hcell = nd[:, :M_V_DIM] / jnp.maximum(jnp.abs(nd[:, M_V_DIM:]), emt_col)
                cell = _rms(hcell) * hn_ref[:, cols]
                out_ref[rows, cols] = (so_ref[rows, cols].astype(F32) * cell).astype(BF16)

            k_t = k_pair_b.astype(F32).T
            w_s = jnp.where(top, row_bcast(w, 2 * p), row_bcast(w, 2 * p + 1))
            kw_t = (k_t * w_s).astype(BF16)
            lhs = jnp.concatenate([jnp.where(top, kw_t, zero_b), jnp.where(top, zero_b, kw_t)], axis=1)
            a = _dot(lhs, jnp.concatenate(v_exts, axis=0))
            dec = jnp.where(top2, row_bcast(decay, 2 * p, L)[:, :1], row_bcast(decay, 2 * p + 1, L)[:, :1])
            sc = jnp.where(top2, row_bcast(scale, 2 * p, L)[:, :1], row_bcast(scale, 2 * p + 1, L)[:, :1])
            state[p] = dec * c_prev + sc * a

    for i, x in enumerate(next_gates):
        gate_scr[i] = x


def _mlstm(qk, mv, so, grow, brow, hn, batch, seq):
    tl = MLSTM_TILE
    nt = seq // tl
    cur = lambda b, t: (b * nt + t, 0)
    const = lambda b, t: (0, 0)
    return pl.pallas_call(
        _mlstm_kernel,
        grid=(batch, nt),
        in_specs=[
            pl.BlockSpec((tl, 2 * M_QK_W), cur),
            pl.BlockSpec((tl, M_V_W), cur),
            pl.BlockSpec((tl, M_V_W), cur),
            pl.BlockSpec((2 * M_HEADS, tl), lambda b, t: (0, b * nt + t)),
            pl.BlockSpec((2 * M_HEADS, tl), lambda b, t: (0, b * nt + jnp.minimum(t + 1, nt - 1))),
            pl.BlockSpec(brow.shape, const),
            pl.BlockSpec(hn.shape, const),
        ],
        out_specs=pl.BlockSpec((tl, M_V_W), cur),
        out_shape=jax.ShapeDtypeStruct(mv.shape, BF16),
        scratch_shapes=[
            pltpu.VMEM((M_HEADS // 2, 2 * M_QK_DIM, 2 * M_V_DIM), F32),
            pltpu.VMEM((2 * M_HEADS, M_CHUNK), F32),
            pltpu.VMEM((3, 2 * M_HEADS, tl), F32),
        ],
        compiler_params=pltpu.CompilerParams(dimension_semantics=("parallel", "arbitrary"), vmem_limit_bytes=VMEM_LIMIT),
        name="mlstm",
    )(qk, mv, so, grow, grow, brow, hn)


def _ff_chunks():
    return [(c0, min(c0 + FF_CHUNK, D_FF)) for c0 in range(0, D_FF, FF_CHUNK)]


def _mix_ffn_kernel(x_ref, a_ref, m_ref, gmix_ref, gpost_ref, gpre_ref, gffn_ref, wg_ref, wa_ref, wm_ref, wo_ref,
                    wfi_ref, wfo_ref, o_ref, acc_ref):
    x = x_ref[...]
    h = (_rms(x) * gmix_ref[...]).astype(BF16)
    a = a_ref[...]
    m = m_ref[...]
    parts = []
    for c0 in range(0, D_MODEL, MERGE_CHUNK):
        cols = slice(c0, c0 + MERGE_CHUNK)
        gcols = slice(D_MODEL + c0, D_MODEL + c0 + MERGE_CHUNK)
        g_attn = jax.nn.sigmoid(_dot(h, wg_ref[:, cols]))
        g_mlstm = jax.nn.sigmoid(_dot(h, wg_ref[:, gcols]))
        parts.append((g_attn * _dot(a, wa_ref[:, cols]) + g_mlstm * _dot(m, wm_ref[:, cols])).astype(BF16))
    y = _dot(jnp.concatenate(parts, axis=1), wo_ref[...])
    x1 = x + _rms(y) * gpost_ref[...]

    h2 = (_rms(x1) * gpre_ref[...]).astype(BF16)
    for c0, c1 in _ff_chunks():
        gate = _dot(h2, wfi_ref[:, c0:c1])
        up = _dot(h2, wfi_ref[:, D_FF + c0:D_FF + c1])
        act = (gate * jax.nn.sigmoid(gate) * up).astype(BF16)
        part = _dot(act, wfo_ref[c0:c1, :])
        if c0 == 0:
            acc_ref[...] = part
        else:
            acc_ref[...] += part
    o_ref[...] = x1 + _rms(acc_ref[...]) * gffn_ref[...]


def _mix_ffn(x2, attn, ml, gmix, gpost, gpre, gffn, wg, wa, wm, wo, wfi, wfo):
    T = x2.shape[0]
    tm = TOK_TILE
    row = lambda i: (i, 0)
    const = lambda i: (0, 0)

    def wspec(w):
        return pl.BlockSpec(w.shape, const, pipeline_mode=pl.Buffered(1))

    gain = pl.BlockSpec((1, D_MODEL), const)
    return pl.pallas_call(
        _mix_ffn_kernel,
        grid=(T // tm,),
        in_specs=[
            pl.BlockSpec((tm, D_MODEL), row),
            pl.BlockSpec((tm, ATTN_Q_W), row),
            pl.BlockSpec((tm, M_V_W), row),
            gain, gain, gain, gain,
            pl.BlockSpec((D_MODEL, W_MIX), const, pipeline_mode=pl.Buffered(1)),
            wspec(wa), wspec(wm), wspec(wo), wspec(wfi), wspec(wfo),
        ],
        out_specs=pl.BlockSpec((tm, D_MODEL), row),
        out_shape=jax.ShapeDtypeStruct(x2.shape, F32),
        scratch_shapes=[pltpu.VMEM((tm, D_MODEL), F32)],
        compiler_params=pltpu.CompilerParams(dimension_semantics=("parallel",), vmem_limit_bytes=VMEM_LIMIT),
        name="mix_ffn",
    )(x2, attn, ml, gmix, gpost, gpre, gffn, wg, wa, wm, wo, wfi, wfo)


def _rope_tables(seq):
    f32 = np.float32
    inv_freq = (f32(ROPE_THETA) ** (-np.arange(0, HEAD_DIM, 2, dtype=f32) / f32(HEAD_DIM))).astype(f32)
    ang = np.arange(seq).astype(f32)[:, None] * inv_freq[None, :]
    emb = np.concatenate([ang, ang], axis=-1)
    cos = np.cos(emb).astype(f32)
    sin = np.sin(emb).astype(f32)
    first_half = np.arange(HEAD_DIM) < HEAD_DIM // 2
    sin_a = np.where(first_half, -sin, f32(0))
    sin_b = np.where(first_half, f32(0), sin)
    rep = LANES // HEAD_DIM
    return tuple(jnp.asarray(np.tile(t, (1, rep))) for t in (cos, sin_a, sin_b))


def _layer(x2, batch, seq, norm_pre_mix, norm_post_mix, norm_pre_ffn, norm_post_ffn, w_in, attn_sinks, conv_w,
           conv_b, b_igate, b_fgate, mlstm_head_norm, w_attn_branch, w_mlstm_branch, w_out, w_ffn_in, w_ffn_out):
    w_all = jnp.concatenate([w_in[:, O_BRANCH:], w_in[:, :O_BRANCH],
                             jnp.zeros((D_MODEL, LANES - 2 * M_HEADS), w_in.dtype)], axis=1).astype(BF16)
    cos, sa, sb = _rope_tables(seq)

    q, k, v, qk, mv, so, grow = _inproj(
        x2, norm_pre_mix[None, :], cos, sa, sb, conv_w, conv_b[None, :], w_all, seq)
    attn = _attention(attn_sinks, q, k, v, batch, seq)
    bias = jnp.concatenate([b_igate, b_fgate])
    ml = _mlstm(qk, mv, so, grow, bias[:, None], mlstm_head_norm[None, :], batch, seq)
    return _mix_ffn(x2, attn, ml, norm_pre_mix[None, :], norm_post_mix[None, :], norm_pre_ffn[None, :],
                    norm_post_ffn[None, :], w_all, w_attn_branch.astype(BF16), w_mlstm_branch.astype(BF16),
                    w_out.astype(BF16), w_ffn_in.astype(BF16), w_ffn_out.astype(BF16))


def kernel(x, norm_pre_mix, norm_post_mix, norm_pre_ffn, norm_post_ffn, w_in, attn_sinks, conv_w, conv_b, b_igate,
           b_fgate, mlstm_head_norm, w_attn_branch, w_mlstm_branch, w_out, w_ffn_in, w_ffn_out):
    B, S, D = x.shape
    x2 = x.reshape(B * S, D)
    for l in range(w_in.shape[0]):
        x2 = _layer(x2, B, S, norm_pre_mix[l], norm_post_mix[l], norm_pre_ffn[l], norm_post_ffn[l], w_in[l],
                    attn_sinks[l], conv_w[l], conv_b[l], b_igate[l], b_fgate[l], mlstm_head_norm[l],
                    w_attn_branch[l], w_mlstm_branch[l], w_out[l], w_ffn_in[l], w_ffn_out[l])
    return x2.reshape(B, S, D)
```

```python
import functools

import numpy as np

import jax
import jax.numpy as jnp
from jax import lax
from jax.experimental import pallas as pl
from jax.experimental.pallas import tpu as pltpu

D_MODEL = 1024
HEAD_DIM = 64
N_Q_HEADS = 8
N_KV_HEADS = 2
WINDOW = 128
ATTN_BLOCK = 128
ROPE_THETA = 10000.0
M_HEADS = 4
M_QK_DIM = 64
M_V_DIM = 128
M_CHUNK = 128
CONV_WIDTH = 4
D_FF = 2816
EPS = 1e-6

ATTN_Q_W = N_Q_HEADS * HEAD_DIM
ATTN_KV_W = N_KV_HEADS * HEAD_DIM
M_QK_W = M_HEADS * M_QK_DIM
M_V_W = M_HEADS * M_V_DIM
KV_BD_W = 2 * ATTN_KV_W * 2
O_MQK = ATTN_Q_W + 2 * ATTN_KV_W
O_MV = O_MQK + 2 * M_QK_W
O_GATES = O_MV + 2 * M_V_W
O_BRANCH = O_GATES + 2 * M_HEADS

LANES = 128
SUBLANES = 8
VMEM_LIMIT = 56 * 1024 * 1024
CONV_HALO = SUBLANES

TOK_TILE = 512
INPROJ_TILE = 1024
ATTN_TILE = 2048
MLSTM_TILE = 2048
FF_CHUNK = 256
MERGE_CHUNK = 256

F32 = jnp.float32
BF16 = jnp.bfloat16


def _dot(a, b):
    return jnp.dot(a, b, preferred_element_type=F32)


def _dot_nt(a, b):
    return lax.dot_general(a, b, (((1,), (1,)), ((), ())), preferred_element_type=F32)


def _rms(x):
    return x * lax.rsqrt(jnp.mean(x * x, axis=-1, keepdims=True) + EPS)


def _split3(x):
    hi = x.astype(BF16)
    r1 = x - hi.astype(F32)
    mid = r1.astype(BF16)
    lo = (r1 - mid.astype(F32)).astype(BF16)
    return hi, mid, lo


def _inproj_kernel(x_ref, g_ref, cos_ref, sa_ref, sb_ref, cw_ref, cb_ref, w_ref, wif_ref,
                   h_ref, q_ref, k_ref, v_ref, qk_ref, mv_ref, so_ref, grow_ref, cbuf, *, n_seq_tiles):
    tm = x_ref.shape[0]
    h = (_rms(x_ref[...]) * g_ref[...]).astype(BF16)
    h_ref[...] = h
    cos = cos_ref[...]
    sa = sa_ref[...]
    sb = sb_ref[...]
    lane = lax.broadcasted_iota(jnp.int32, cos.shape, 1)
    lo = lane < HEAD_DIM

    def rope(t):
        return t * cos + pltpu.roll(t, LANES - HEAD_DIM // 2, 1) * sa + pltpu.roll(t, HEAD_DIM // 2, 1) * sb

    def store_block_diag(ref, t):
        tb = t.astype(BF16)
        rb = pltpu.roll(t, HEAD_DIM, 1).astype(BF16)
        zero = jnp.zeros((), BF16)
        for i, (src, keep_lo) in enumerate(((tb, True), (rb, False), (rb, True), (tb, False))):
            ref[:, i * LANES:(i + 1) * LANES] = jnp.where(lo, src, zero) if keep_lo else jnp.where(lo, zero, src)

    def attn_proj():
        a = _dot(h, w_ref[:, :O_MQK])
        for j in range(ATTN_Q_W // LANES):
            q_ref[:, j * LANES:(j + 1) * LANES] = (
                rope(a[:, j * LANES:(j + 1) * LANES]) * (HEAD_DIM ** -0.5)).astype(BF16)
        store_block_diag(k_ref, rope(a[:, ATTN_Q_W:ATTN_Q_W + LANES]))
        store_block_diag(v_ref, a[:, ATTN_Q_W + LANES:ATTN_Q_W + 2 * LANES])
        grow_ref[...] = _dot(h, wif_ref[...]).T[:2 * M_HEADS, :]

    @pl.when(pl.program_id(0) % n_seq_tiles == 0)
    def _():
        cbuf[0:CONV_HALO, :] = jnp.zeros((CONV_HALO, cbuf.shape[1]), F32)

    def mqk_proj(c0, c1):
        cbuf[CONV_HALO:CONV_HALO + tm, c0:c1] = _dot(h, w_ref[:, O_MQK + c0:O_MQK + c1])

    def conv_chunk(j0):
        cols = slice(j0 * LANES, (j0 + 1) * LANES)
        y = cb_ref[:, cols]
        for j in range(CONV_WIDTH):
            off = CONV_HALO - (CONV_WIDTH - 1) + j
            y = y + cw_ref[j:j + 1, cols] * cbuf[off:off + tm, cols]
        cbuf[0:CONV_HALO, cols] = cbuf[tm:tm + CONV_HALO, cols]
        y = y * jax.nn.sigmoid(y)
        if j0 * LANES >= M_QK_W:
            y = y * (M_QK_DIM ** -0.5)
        qk_ref[:, cols] = y.astype(BF16)

    def mlstm_vo_proj():
        vo = _dot(h, w_ref[:, O_MV:O_GATES])
        mv_ref[...] = vo[:, :M_V_W].astype(BF16)
        so_ref[...] = jax.nn.sigmoid(vo[:, M_V_W:]).astype(BF16)

    mqk_proj(0, M_QK_W)
    conv_chunk(0)
    conv_chunk(1)
    attn_proj()
    mqk_proj(M_QK_W, 2 * M_QK_W)
    conv_chunk(2)
    conv_chunk(3)
    mlstm_vo_proj()


def _inproj(x2, g, cos, sa, sb, cw, cb, w, wif, seq):
    T = x2.shape[0]
    tm = INPROJ_TILE
    n_seq_tiles = seq // tm
    row = lambda i: (i, 0)
    pos = lambda i: (i % n_seq_tiles, 0)
    const = lambda i: (0, 0)

    def wspec(w):
        return pl.BlockSpec(w.shape, const, pipeline_mode=pl.Buffered(1))

    out_shape = (
        jax.ShapeDtypeStruct((T, D_MODEL), BF16),
        jax.ShapeDtypeStruct((T, ATTN_Q_W), BF16),
        jax.ShapeDtypeStruct((T, KV_BD_W), BF16),
        jax.ShapeDtypeStruct((T, KV_BD_W), BF16),
        jax.ShapeDtypeStruct((T, 2 * M_QK_W), BF16),
        jax.ShapeDtypeStruct((T, M_V_W), BF16),
        jax.ShapeDtypeStruct((T, M_V_W), BF16),
        jax.ShapeDtypeStruct((2 * M_HEADS, T), F32),
    )
    out_specs = (
        pl.BlockSpec((tm, D_MODEL), row),
        pl.BlockSpec((tm, ATTN_Q_W), row),
        pl.BlockSpec((tm, KV_BD_W), row),
        pl.BlockSpec((tm, KV_BD_W), row),
        pl.BlockSpec((tm, 2 * M_QK_W), row),
        pl.BlockSpec((tm, M_V_W), row),
        pl.BlockSpec((tm, M_V_W), row),
        pl.BlockSpec((2 * M_HEADS, tm), lambda i: (0, i)),
    )
    in_specs = [
        pl.BlockSpec((tm, D_MODEL), row),
        pl.BlockSpec((1, D_MODEL), const),
        pl.BlockSpec((tm, LANES), pos),
        pl.BlockSpec((tm, LANES), pos),
        pl.BlockSpec((tm, LANES), pos),
        pl.BlockSpec(cw.shape, const),
        pl.BlockSpec(cb.shape, const),
        wspec(w), wspec(wif),
    ]
    return pl.pallas_call(
        functools.partial(_inproj_kernel, n_seq_tiles=n_seq_tiles),
        grid=(T // tm,),
        in_specs=in_specs,
        out_specs=out_specs,
        out_shape=out_shape,
        scratch_shapes=[pltpu.VMEM((CONV_HALO + tm, 2 * M_QK_W), F32)],
        compiler_params=pltpu.CompilerParams(dimension_semantics=("arbitrary",), vmem_limit_bytes=VMEM_LIMIT),
        name="inproj",
    )(x2, g, cos, sa, sb, cw, cb, w, wif)


def _attn_kernel(sink_ref, q_ref, kc_ref, kp_ref, vc_ref, vp_ref, o_ref):
    t = pl.program_id(1)
    blk = ATTN_BLOCK
    n_blk = q_ref.shape[0] // blk
    lo_q = lax.broadcasted_iota(jnp.int32, (blk, LANES), 1) < HEAD_DIM
    ri = lax.broadcasted_iota(jnp.int32, (blk, blk), 0)
    ci = lax.broadcasted_iota(jnp.int32, (blk, blk), 1)
    own = ci <= ri
    prev_ok = ci > ri + jnp.where(t > 0, 0, blk)
    zero = jnp.zeros((), BF16)
    row_is_a = lax.broadcasted_iota(jnp.int32, (4 * blk, LANES), 0) < 2 * blk
    lane_is_a = lax.broadcasted_iota(jnp.int32, (4 * blk, LANES), 1) < HEAD_DIM
    ones_bd = (row_is_a == lane_is_a).astype(BF16)

    for n in range(n_blk):
        rows = slice(n * blk, (n + 1) * blk)
        k_prev = kp_ref if n == 0 else kc_ref.at[(n - 1) * blk:n * blk, :]
        v_prev = vp_ref if n == 0 else vc_ref.at[(n - 1) * blk:n * blk, :]
        for j in range(N_KV_HEADS):
            kv_cols = [slice((2 * j + i) * LANES, (2 * j + i + 1) * LANES) for i in range(2)]
            k_bd = jnp.concatenate([x for c in kv_cols for x in (k_prev[:, c], kc_ref[rows, c])], axis=0)
            v_bd = jnp.concatenate([x for c in kv_cols for x in (v_prev[:, c], vc_ref[rows, c])], axis=0)
            c0 = 2 * j * LANES
            q2 = jnp.concatenate([q_ref[rows, c0:c0 + LANES], q_ref[rows, c0 + LANES:c0 + 2 * LANES]], axis=0)
            s = _dot_nt(q2, k_bd)
            e_rows, sink_rows = [], []
            for p in range(2):
                e_cols, sink_cols = [], []
                for hh in range(2):
                    sink = sink_ref[4 * j + 2 * p + hh]
                    s_prev = s[p * blk:(p + 1) * blk, 2 * hh * blk:(2 * hh + 1) * blk]
                    s_own = s[p * blk:(p + 1) * blk, (2 * hh + 1) * blk:(2 * hh + 2) * blk]
                    if n == 0:
                        s_prev = jnp.where(prev_ok, s_prev, -jnp.inf)
                    sc = jnp.where(own, s_own, s_prev)
                    m = jnp.maximum(jnp.max(sc, axis=-1, keepdims=True), sink)
                    eb = jnp.exp(sc - m).astype(BF16)
                    e_cols += [jnp.where(own, zero, eb), jnp.where(own, eb, zero)]
                    sink_cols.append(jnp.exp(sink - m))
                e_rows.append(jnp.concatenate(e_cols, axis=1))
                sink_rows.append(jnp.where(lo_q, sink_cols[0], sink_cols[1]))
            o = _dot(jnp.concatenate(e_rows, axis=0), jnp.concatenate([v_bd, ones_bd], axis=1))
            for p in range(2):
                op = o[p * blk:(p + 1) * blk]
                o_ref[rows, c0 + p * LANES:c0 + (p + 1) * LANES] = (
                    op[:, :LANES] / (op[:, LANES:] + sink_rows[p])).astype(BF16)


def _attention(sinks, q, k, v, batch, seq):
    tq = ATTN_TILE
    nt = seq // tq
    per = tq // ATTN_BLOCK
    cur = lambda b, t: (b * nt + t, 0)
    prev = lambda b, t: (jnp.maximum((b * nt + t) * per - 1, 0), 0)
    return pl.pallas_call(
        _attn_kernel,
        grid=(batch, nt),
        in_specs=[
            pl.BlockSpec(memory_space=pltpu.SMEM),
            pl.BlockSpec((tq, ATTN_Q_W), cur),
            pl.BlockSpec((tq, KV_BD_W), cur),
            pl.BlockSpec((ATTN_BLOCK, KV_BD_W), prev),
            pl.BlockSpec((tq, KV_BD_W), cur),
            pl.BlockSpec((ATTN_BLOCK, KV_BD_W), prev),
        ],
        out_specs=pl.BlockSpec((tq, ATTN_Q_W), cur),
        out_shape=jax.ShapeDtypeStruct(q.shape, BF16),
        compiler_params=pltpu.CompilerParams(dimension_semantics=("parallel", "parallel"), vmem_limit_bytes=VMEM_LIMIT),
        name="swa_attention",
    )(sinks, q, k, k, v, v)


def _mlstm_kernel(qk_ref, mv_ref, so_ref, grow_ref, gnext_ref, brow_ref, hn_ref, out_ref, state, m_scr, gate_scr):
    t = pl.program_id(1)
    TL = qk_ref.shape[0]
    L = M_CHUNK
    R = 2 * M_HEADS

    ri = lax.broadcasted_iota(jnp.int32, (L, L), 0)
    ci = lax.broadcasted_iota(jnp.int32, (L, L), 1)
    causal = ci <= ri
    triu = (ri <= ci).astype(BF16)
    lane_in_chunk = lax.broadcasted_iota(jnp.int32, (R, TL), 1) & (L - 1)
    lo_b = lax.broadcasted_iota(jnp.int32, (L, LANES), 1) < M_QK_DIM
    top = lax.broadcasted_iota(jnp.int32, (L, LANES), 0) < M_QK_DIM
    top2 = lax.broadcasted_iota(jnp.int32, (L, 2 * M_V_DIM), 0) < M_QK_DIM
    ones_v = jnp.ones((L, M_V_DIM), BF16)
    zero_b = jnp.zeros((), BF16)
    chunk_rows = [slice(c * L, (c + 1) * L) for c in range(TL // L)]

    def row_bcast(x8, h, n=L):
        return jnp.broadcast_to(x8[h:h + 1, :], (n, x8.shape[1]))

    def gate_scan(g_ref):
        gr = g_ref[...] + brow_ref[...]
        lf = jax.nn.log_sigmoid(gr)
        b8 = jnp.concatenate([sum(_dot(part, triu) for part in _split3(lf[:, rows])) for rows in chunk_rows], axis=1)
        b_all = pltpu.roll(b8, M_HEADS, 0)
        r_all = gr - b_all
        cm_all = r_all
        sh = 1
        while sh < L:
            cm_all = jnp.where(lane_in_chunk >= sh, jnp.maximum(cm_all, pltpu.roll(cm_all, sh, 1)), cm_all)
            sh *= 2
        return b_all, r_all, cm_all

    @pl.when(t == 0)
    def _():
        state[...] = jnp.zeros(state.shape, F32)
        m_scr[...] = jnp.zeros(m_scr.shape, F32)
        for i, x in enumerate(gate_scan(grow_ref)):
            gate_scr[i] = x

    b_all, r_all, cm_all = gate_scr[0], gate_scr[1], gate_scr[2]
    next_gates = gate_scan(gnext_ref)

    m = m_scr[...]
    gates = []
    for rows in chunk_rows:
        b, r, cm = b_all[:, rows], r_all[:, rows], cm_all[:, rows]
        rmax = jnp.broadcast_to(cm[:, L - 1:L], (R, L))
        g = jnp.broadcast_to(b[:, L - 1:L], (R, L))
        big_m = jnp.maximum(m, cm)
        inter = jnp.exp(m - big_m)
        emt = jnp.exp(-(b + big_m))
        w = jnp.exp(r - rmax)
        m_loc = g + rmax
        m_new = jnp.maximum(g + m, m_loc)
        decay = jnp.exp(g + m - m_new)
        scale = jnp.exp(m_loc - m_new)
        m = m_new
        gates.append((r, w, decay, scale, big_m, inter, emt))
    m_scr[...] = m

    for rows, (r, w, decay, scale, big_m, inter, emt) in zip(chunk_rows, gates):
        for p in range(M_HEADS // 2):
            pc = slice(p * LANES, (p + 1) * LANES)
            q_pair_b = qk_ref[rows, pc]
            k_pair_b = qk_ref[rows, M_QK_W + p * LANES:M_QK_W + (p + 1) * LANES]
            c_prev = state[p]
            c_prev_b = c_prev.astype(BF16)
            inter_cols = jnp.where(top, row_bcast(inter, 2 * p), row_bcast(inter, 2 * p + 1)).T
            qs_pair = (q_pair_b.astype(F32) * inter_cols).astype(BF16)
            v_exts = []
            for hh in range(2):
                hd = 2 * p + hh
                cols = slice(hd * M_V_DIM, (hd + 1) * M_V_DIM)
                sel = lo_b if hh == 0 else jnp.logical_not(lo_b)
                v_ext = jnp.concatenate([mv_ref[rows, cols], ones_v], axis=1)
                v_exts.append(v_ext)
                m_col, emt_col = (row_bcast(x8, hd).T for x8 in (big_m, emt))

                d_mat = jnp.exp(jnp.where(causal, row_bcast(r, hd), -jnp.inf) - m_col)
                k_h = jnp.where(sel, k_pair_b, zero_b)
                s_mat = (_dot_nt(q_pair_b, k_h) * d_mat).astype(BF16)
                qs = jnp.where(sel, qs_pair, zero_b)
                nd = _dot(jnp.concatenate([s_mat, qs], axis=1), jnp.concatenate([v_ext, c_prev_b], axis=0))
                hcell = nd[:, :M_V_DIM] / jnp.maximum(jnp.abs(nd[:, M_V_DIM:]), emt_col)
                cell = _rms(hcell) * hn_ref[:, cols]
                out_ref[rows, cols] = (so_ref[rows, cols].astype(F32) * cell).astype(BF16)

            k_t = k_pair_b.astype(F32).T
            w_s = jnp.where(top, row_bcast(w, 2 * p), row_bcast(w, 2 * p + 1))
            kw_t = (k_t * w_s).astype(BF16)
            lhs = jnp.concatenate([jnp.where(top, kw_t, zero_b), jnp.where(top, zero_b, kw_t)], axis=1)
            a = _dot(lhs, jnp.concatenate(v_exts, axis=0))
            dec = jnp.where(top2, row_bcast(decay, 2 * p, L)[:, :1], row_bcast(decay, 2 * p + 1, L)[:, :1])
            sc = jnp.where(top2, row_bcast(scale, 2 * p, L)[:, :1], row_bcast(scale, 2 * p + 1, L)[:, :1])
            state[p] = dec * c_prev + sc * a

    for i, x in enumerate(next_gates):
        gate_scr[i] = x


def _mlstm(qk, mv, so, grow, brow, hn, batch, seq):
    tl = MLSTM_TILE
    nt = seq // tl
    cur = lambda b, t: (b * nt + t, 0)
    const = lambda b, t: (0, 0)
    return pl.pallas_call(
        _mlstm_kernel,
        grid=(batch, nt),
        in_specs=[
            pl.BlockSpec((tl, 2 * M_QK_W), cur),
            pl.BlockSpec((tl, M_V_W), cur),
            pl.BlockSpec((tl, M_V_W), cur),
            pl.BlockSpec((2 * M_HEADS, tl), lambda b, t: (0, b * nt + t)),
            pl.BlockSpec((2 * M_HEADS, tl), lambda b, t: (0, b * nt + jnp.minimum(t + 1, nt - 1))),
            pl.BlockSpec(brow.shape, const),
            pl.BlockSpec(hn.shape, const),
        ],
        out_specs=pl.BlockSpec((tl, M_V_W), cur),
        out_shape=jax.ShapeDtypeStruct(mv.shape, BF16),
        scratch_shapes=[
            pltpu.VMEM((M_HEADS // 2, 2 * M_QK_DIM, 2 * M_V_DIM), F32),
            pltpu.VMEM((2 * M_HEADS, M_CHUNK), F32),
            pltpu.VMEM((3, 2 * M_HEADS, tl), F32),
        ],
        compiler_params=pltpu.CompilerParams(dimension_semantics=("parallel", "arbitrary"), vmem_limit_bytes=VMEM_LIMIT),
        name="mlstm",
    )(qk, mv, so, grow, grow, brow, hn)


def _ff_chunks():
    return [(c0, min(c0 + FF_CHUNK, D_FF)) for c0 in range(0, D_FF, FF_CHUNK)]


def _mix_ffn_kernel(x_ref, h_ref, a_ref, m_ref, gpost_ref, gpre_ref, gffn_ref, wg_ref, wa_ref, wm_ref, wo_ref,
                    wfi_ref, wfo_ref, o_ref, acc_ref):
    x = x_ref[...]
    h = h_ref[...]
    a = a_ref[...]
    m = m_ref[...]
    parts = []
    for c0 in range(0, D_MODEL, MERGE_CHUNK):
        cols = slice(c0, c0 + MERGE_CHUNK)
        gcols = slice(D_MODEL + c0, D_MODEL + c0 + MERGE_CHUNK)
        g_attn = jax.nn.sigmoid(_dot(h, wg_ref[:, cols]))
        g_mlstm = jax.nn.sigmoid(_dot(h, wg_ref[:, gcols]))
        parts.append((g_attn * _dot(a, wa_ref[:, cols]) + g_mlstm * _dot(m, wm_ref[:, cols])).astype(BF16))
    y = _dot(jnp.concatenate(parts, axis=1), wo_ref[...])
    x1 = x + _rms(y) * gpost_ref[...]

    h2 = (_rms(x1) * gpre_ref[...]).astype(BF16)
    for c0, c1 in _ff_chunks():
        gate = _dot(h2, wfi_ref[:, c0:c1])
        up = _dot(h2, wfi_ref[:, D_FF + c0:D_FF + c1])
        act = (gate * jax.nn.sigmoid(gate) * up).astype(BF16)
        part = _dot(act, wfo_ref[c0:c1, :])
        if c0 == 0:
            acc_ref[...] = part
        else:
            acc_ref[...] += part
    o_ref[...] = x1 + _rms(acc_ref[...]) * gffn_ref[...]


def _mix_ffn(x2, h, attn, ml, gpost, gpre, gffn, wg, wa, wm, wo, wfi, wfo):
    T = x2.shape[0]
    tm = TOK_TILE
    row = lambda i: (i, 0)
    const = lambda i: (0, 0)

    def wspec(w):
        return pl.BlockSpec(w.shape, const, pipeline_mode=pl.Buffered(1))

    gain = pl.BlockSpec((1, D_MODEL), const)
    return pl.pallas_call(
        _mix_ffn_kernel,
        grid=(T // tm,),
        in_specs=[
            pl.BlockSpec((tm, D_MODEL), row),
            pl.BlockSpec((tm, D_MODEL), row),
            pl.BlockSpec((tm, ATTN_Q_W), row),
            pl.BlockSpec((tm, M_V_W), row),
            gain, gain, gain,
            wspec(wg), wspec(wa), wspec(wm), wspec(wo), wspec(wfi), wspec(wfo),
        ],
        out_specs=pl.BlockSpec((tm, D_MODEL), row),
        out_shape=jax.ShapeDtypeStruct(x2.shape, F32),
        scratch_shapes=[pltpu.VMEM((tm, D_MODEL), F32)],
        compiler_params=pltpu.CompilerParams(dimension_semantics=("parallel",), vmem_limit_bytes=VMEM_LIMIT),
        name="mix_ffn",
    )(x2, h, attn, ml, gpost, gpre, gffn, wg, wa, wm, wo, wfi, wfo)


def _rope_tables(seq):
    f32 = np.float32
    inv_freq = (f32(ROPE_THETA) ** (-np.arange(0, HEAD_DIM, 2, dtype=f32) / f32(HEAD_DIM))).astype(f32)
    ang = np.arange(seq).astype(f32)[:, None] * inv_freq[None, :]
    emb = np.concatenate([ang, ang], axis=-1)
    cos = np.cos(emb).astype(f32)
    sin = np.sin(emb).astype(f32)
    first_half = np.arange(HEAD_DIM) < HEAD_DIM // 2
    sin_a = np.where(first_half, -sin, f32(0))
    sin_b = np.where(first_half, f32(0), sin)
    rep = LANES // HEAD_DIM
    return tuple(jnp.asarray(np.tile(t, (1, rep))) for t in (cos, sin_a, sin_b))


def _layer(x2, batch, seq, norm_pre_mix, norm_post_mix, norm_pre_ffn, norm_post_ffn, w_in, attn_sinks, conv_w,
           conv_b, b_igate, b_fgate, mlstm_head_norm, w_attn_branch, w_mlstm_branch, w_out, w_ffn_in, w_ffn_out):
    w_mix = w_in[:, :O_GATES].astype(BF16)
    wif = jnp.pad(w_in[:, O_GATES:O_BRANCH], ((0, 0), (0, LANES - 2 * M_HEADS))).astype(BF16)
    wg = w_in[:, O_BRANCH:].astype(BF16)
    cos, sa, sb = _rope_tables(seq)

    h, q, k, v, qk, mv, so, grow = _inproj(
        x2, norm_pre_mix[None, :], cos, sa, sb, conv_w, conv_b[None, :], w_mix, wif, seq)
    attn = _attention(attn_sinks, q, k, v, batch, seq)
    bias = jnp.concatenate([b_igate, b_fgate])
    ml = _mlstm(qk, mv, so, grow, bias[:, None], mlstm_head_norm[None, :], batch, seq)
    return _mix_ffn(x2, h, attn, ml, norm_post_mix[None, :], norm_pre_ffn[None, :],
                    norm_post_ffn[None, :], wg, w_attn_branch.astype(BF16), w_mlstm_branch.astype(BF16),
                    w_out.astype(BF16), w_ffn_in.astype(BF16), w_ffn_out.astype(BF16))


def kernel(x, norm_pre_mix, norm_post_mix, norm_pre_ffn, norm_post_ffn, w_in, attn_sinks, conv_w, conv_b, b_igate,
           b_fgate, mlstm_head_norm, w_attn_branch, w_mlstm_branch, w_out, w_ffn_in, w_ffn_out):
    B, S, D = x.shape
    assert D == D_MODEL and WINDOW == ATTN_BLOCK == M_CHUNK
    assert S % max(INPROJ_TILE, ATTN_TILE, MLSTM_TILE) == 0 and (B * S) % TOK_TILE == 0
    x2 = x.reshape(B * S, D)
    for l in range(w_in.shape[0]):
        x2 = _layer(x2, B, S, norm_pre_mix[l], norm_post_mix[l], norm_pre_ffn[l], norm_post_ffn[l], w_in[l],
                    attn_sinks[l], conv_w[l], conv_b[l], b_igate[l], b_fgate[l], mlstm_head_norm[l],
                    w_attn_branch[l], w_mlstm_branch[l], w_out[l], w_ffn_in[l], w_ffn_out[l])
    return x2.reshape(B, S, D)
```

```python
import functools

import numpy as np

import jax
import jax.numpy as jnp
from jax import lax
from jax.experimental import pallas as pl
from jax.experimental.pallas import tpu as pltpu

D_MODEL = 1024
HEAD_DIM = 64
N_Q_HEADS = 8
N_KV_HEADS = 2
WINDOW = 128
ATTN_BLOCK = 128
ROPE_THETA = 10000.0
M_HEADS = 4
M_QK_DIM = 64
M_V_DIM = 128
M_CHUNK = 128
CONV_WIDTH = 4
D_FF = 2816
EPS = 1e-6

ATTN_Q_W = N_Q_HEADS * HEAD_DIM
ATTN_KV_W = N_KV_HEADS * HEAD_DIM
M_QK_W = M_HEADS * M_QK_DIM
M_V_W = M_HEADS * M_V_DIM
KV_BD_W = 2 * ATTN_KV_W * 2
O_MQK = ATTN_Q_W + 2 * ATTN_KV_W
O_MV = O_MQK + 2 * M_QK_W
O_GATES = O_MV + 2 * M_V_W
O_BRANCH = O_GATES + 2 * M_HEADS

LANES = 128
SUBLANES = 8
VMEM_LIMIT = 56 * 1024 * 1024
CONV_HALO = SUBLANES

TOK_TILE = 512
INPROJ_TILE = 1024
ATTN_TILE = 2048
MLSTM_TILE = 2048
FF_CHUNK = 256
MERGE_CHUNK = 256
W_REGROUP_ROWS = 128

F32 = jnp.float32
BF16 = jnp.bfloat16


def _dot(a, b):
    return jnp.dot(a, b, preferred_element_type=F32)


def _dot_nt(a, b):
    return lax.dot_general(a, b, (((1,), (1,)), ((), ())), preferred_element_type=F32)


def _rms(x):
    return x * lax.rsqrt(jnp.mean(x * x, axis=-1, keepdims=True) + EPS)


def _split3(x):
    hi = x.astype(BF16)
    r1 = x - hi.astype(F32)
    mid = r1.astype(BF16)
    lo = (r1 - mid.astype(F32)).astype(BF16)
    return hi, mid, lo


def _regroup_kernel(w_ref, mix_ref, if_ref, g_ref):
    w = w_ref[...]
    mix_ref[...] = w[:, :O_GATES].astype(BF16)
    lane = lax.broadcasted_iota(jnp.int32, if_ref.shape, 1)
    if_ref[...] = jnp.where(lane < 2 * M_HEADS, w[:, O_GATES:O_GATES + LANES], 0.0).astype(BF16)
    g_ref[...] = w[:, O_BRANCH:].astype(BF16)


def _regroup_w_in(w_in):
    rows = W_REGROUP_ROWS
    block = lambda width: pl.BlockSpec((rows, width), lambda i: (i, 0))
    return pl.pallas_call(
        _regroup_kernel,
        grid=(D_MODEL // rows,),
        in_specs=[block(w_in.shape[1])],
        out_specs=(block(O_GATES), block(LANES), block(2 * D_MODEL)),
        out_shape=(jax.ShapeDtypeStruct((D_MODEL, O_GATES), BF16),
                   jax.ShapeDtypeStruct((D_MODEL, LANES), BF16),
                   jax.ShapeDtypeStruct((D_MODEL, 2 * D_MODEL), BF16)),
        compiler_params=pltpu.CompilerParams(dimension_semantics=("parallel",), vmem_limit_bytes=VMEM_LIMIT),
        name="regroup_w_in",
    )(w_in)


def _inproj_kernel(x_ref, g_ref, cos_ref, sa_ref, sb_ref, cw_ref, cb_ref, w_ref, wif_ref,
                   h_ref, q_ref, k_ref, v_ref, qk_ref, mv_ref, so_ref, grow_ref, cbuf, *, n_seq_tiles):
    tm = x_ref.shape[0]
    h = (_rms(x_ref[...]) * g_ref[...]).astype(BF16)
    h_ref[...] = h
    cos = cos_ref[...]
    sa = sa_ref[...]
    sb = sb_ref[...]
    lane = lax.broadcasted_iota(jnp.int32, cos.shape, 1)
    lo = lane < HEAD_DIM

    def rope(t):
        return t * cos + pltpu.roll(t, LANES - HEAD_DIM // 2, 1) * sa + pltpu.roll(t, HEAD_DIM // 2, 1) * sb

    def store_block_diag(ref, t):
        tb = t.astype(BF16)
        rb = pltpu.roll(t, HEAD_DIM, 1).astype(BF16)
        zero = jnp.zeros((), BF16)
        for i, (src, keep_lo) in enumerate(((tb, True), (rb, False), (rb, True), (tb, False))):
            ref[:, i * LANES:(i + 1) * LANES] = jnp.where(lo, src, zero) if keep_lo else jnp.where(lo, zero, src)

    def attn_proj():
        a = _dot(h, w_ref[:, :O_MQK])
        for j in range(ATTN_Q_W // LANES):
            q_ref[:, j * LANES:(j + 1) * LANES] = (
                rope(a[:, j * LANES:(j + 1) * LANES]) * (HEAD_DIM ** -0.5)).astype(BF16)
        store_block_diag(k_ref, rope(a[:, ATTN_Q_W:ATTN_Q_W + LANES]))
        store_block_diag(v_ref, a[:, ATTN_Q_W + LANES:ATTN_Q_W + 2 * LANES])
        grow_ref[...] = _dot(h, wif_ref[...]).T[:2 * M_HEADS, :]

    @pl.when(pl.program_id(0) % n_seq_tiles == 0)
    def _():
        cbuf[0:CONV_HALO, :] = jnp.zeros((CONV_HALO, cbuf.shape[1]), F32)

    def mqk_proj(c0, c1):
        cbuf[CONV_HALO:CONV_HALO + tm, c0:c1] = _dot(h, w_ref[:, O_MQK + c0:O_MQK + c1])

    def conv_chunk(j0):
        cols = slice(j0 * LANES, (j0 + 1) * LANES)
        y = cb_ref[:, cols]
        for j in range(CONV_WIDTH):
            off = CONV_HALO - (CONV_WIDTH - 1) + j
            y = y + cw_ref[j:j + 1, cols] * cbuf[off:off + tm, cols]
        cbuf[0:CONV_HALO, cols] = cbuf[tm:tm + CONV_HALO, cols]
        y = y * jax.nn.sigmoid(y)
        if j0 * LANES >= M_QK_W:
            y = y * (M_QK_DIM ** -0.5)
        qk_ref[:, cols] = y.astype(BF16)

    def mlstm_vo_proj():
        vo = _dot(h, w_ref[:, O_MV:O_GATES])
        mv_ref[...] = vo[:, :M_V_W].astype(BF16)
        so_ref[...] = jax.nn.sigmoid(vo[:, M_V_W:]).astype(BF16)

    mqk_proj(0, M_QK_W)
    conv_chunk(0)
    conv_chunk(1)
    attn_proj()
    mqk_proj(M_QK_W, 2 * M_QK_W)
    conv_chunk(2)
    conv_chunk(3)
    mlstm_vo_proj()


def _inproj(x2, g, cos, sa, sb, cw, cb, w, wif, seq):
    T = x2.shape[0]
    tm = INPROJ_TILE
    n_seq_tiles = seq // tm
    row = lambda i: (i, 0)
    pos = lambda i: (i % n_seq_tiles, 0)
    const = lambda i: (0, 0)

    def wspec(w):
        return pl.BlockSpec(w.shape, const, pipeline_mode=pl.Buffered(1))

    out_shape = (
        jax.ShapeDtypeStruct((T, D_MODEL), BF16),
        jax.ShapeDtypeStruct((T, ATTN_Q_W), BF16),
        jax.ShapeDtypeStruct((T, KV_BD_W), BF16),
        jax.ShapeDtypeStruct((T, KV_BD_W), BF16),
        jax.ShapeDtypeStruct((T, 2 * M_QK_W), BF16),
        jax.ShapeDtypeStruct((T, M_V_W), BF16),
        jax.ShapeDtypeStruct((T, M_V_W), BF16),
        jax.ShapeDtypeStruct((2 * M_HEADS, T), F32),
    )
    out_specs = (
        pl.BlockSpec((tm, D_MODEL), row),
        pl.BlockSpec((tm, ATTN_Q_W), row),
        pl.BlockSpec((tm, KV_BD_W), row),
        pl.BlockSpec((tm, KV_BD_W), row),
        pl.BlockSpec((tm, 2 * M_QK_W), row),
        pl.BlockSpec((tm, M_V_W), row),
        pl.BlockSpec((tm, M_V_W), row),
        pl.BlockSpec((2 * M_HEADS, tm), lambda i: (0, i)),
    )
    in_specs = [
        pl.BlockSpec((tm, D_MODEL), row),
        pl.BlockSpec((1, D_MODEL), const),
        pl.BlockSpec((tm, LANES), pos),
        pl.BlockSpec((tm, LANES), pos),
        pl.BlockSpec((tm, LANES), pos),
        pl.BlockSpec(cw.shape, const),
        pl.BlockSpec(cb.shape, const),
        wspec(w), wspec(wif),
    ]
    return pl.pallas_call(
        functools.partial(_inproj_kernel, n_seq_tiles=n_seq_tiles),
        grid=(T // tm,),
        in_specs=in_specs,
        out_specs=out_specs,
        out_shape=out_shape,
        scratch_shapes=[pltpu.VMEM((CONV_HALO + tm, 2 * M_QK_W), F32)],
        compiler_params=pltpu.CompilerParams(dimension_semantics=("arbitrary",), vmem_limit_bytes=VMEM_LIMIT),
        name="inproj",
    )(x2, g, cos, sa, sb, cw, cb, w, wif)


def _attn_kernel(sink_ref, q_ref, kc_ref, kp_ref, vc_ref, vp_ref, o_ref):
    t = pl.program_id(1)
    blk = ATTN_BLOCK
    n_blk = q_ref.shape[0] // blk
    lo_q = lax.broadcasted_iota(jnp.int32, (blk, LANES), 1) < HEAD_DIM
    ri = lax.broadcasted_iota(jnp.int32, (blk, blk), 0)
    ci = lax.broadcasted_iota(jnp.int32, (blk, blk), 1)
    own = ci <= ri
    prev_ok = ci > ri + jnp.where(t > 0, 0, blk)
    zero = jnp.zeros((), BF16)
    row_is_a = lax.broadcasted_iota(jnp.int32, (4 * blk, LANES), 0) < 2 * blk
    lane_is_a = lax.broadcasted_iota(jnp.int32, (4 * blk, LANES), 1) < HEAD_DIM
    ones_bd = (row_is_a == lane_is_a).astype(BF16)

    for n in range(n_blk):
        rows = slice(n * blk, (n + 1) * blk)
        k_prev = kp_ref if n == 0 else kc_ref.at[(n - 1) * blk:n * blk, :]
        v_prev = vp_ref if n == 0 else vc_ref.at[(n - 1) * blk:n * blk, :]
        for j in range(N_KV_HEADS):
            kv_cols = [slice((2 * j + i) * LANES, (2 * j + i + 1) * LANES) for i in range(2)]
            k_bd = jnp.concatenate([x for c in kv_cols for x in (k_prev[:, c], kc_ref[rows, c])], axis=0)
            v_bd = jnp.concatenate([x for c in kv_cols for x in (v_prev[:, c], vc_ref[rows, c])], axis=0)
            c0 = 2 * j * LANES
            q2 = jnp.concatenate([q_ref[rows, c0:c0 + LANES], q_ref[rows, c0 + LANES:c0 + 2 * LANES]], axis=0)
            s = _dot_nt(q2, k_bd)
            e_rows, sink_rows = [], []
            for p in range(2):
                e_cols, sink_cols = [], []
                for hh in range(2):
                    sink = sink_ref[4 * j + 2 * p + hh]
                    s_prev = s[p * blk:(p + 1) * blk, 2 * hh * blk:(2 * hh + 1) * blk]
                    s_own = s[p * blk:(p + 1) * blk, (2 * hh + 1) * blk:(2 * hh + 2) * blk]
                    if n == 0:
                        s_prev = jnp.where(prev_ok, s_prev, -jnp.inf)
                    sc = jnp.where(own, s_own, s_prev)
                    m = jnp.maximum(jnp.max(sc, axis=-1, keepdims=True), sink)
                    eb = jnp.exp(sc - m).astype(BF16)
                    e_cols += [jnp.where(own, zero, eb), jnp.where(own, eb, zero)]
                    sink_cols.append(jnp.exp(sink - m))
                e_rows.append(jnp.concatenate(e_cols, axis=1))
                sink_rows.append(jnp.where(lo_q, sink_cols[0], sink_cols[1]))
            o = _dot(jnp.concatenate(e_rows, axis=0), jnp.concatenate([v_bd, ones_bd], axis=1))
            for p in range(2):
                op = o[p * blk:(p + 1) * blk]
                o_ref[rows, c0 + p * LANES:c0 + (p + 1) * LANES] = (
                    op[:, :LANES] / (op[:, LANES:] + sink_rows[p])).astype(BF16)


def _attention(sinks, q, k, v, batch, seq):
    tq = ATTN_TILE
    nt = seq // tq
    per = tq // ATTN_BLOCK
    cur = lambda b, t: (b * nt + t, 0)
    prev = lambda b, t: (jnp.maximum((b * nt + t) * per - 1, 0), 0)
    return pl.pallas_call(
        _attn_kernel,
        grid=(batch, nt),
        in_specs=[
            pl.BlockSpec(memory_space=pltpu.SMEM),
            pl.BlockSpec((tq, ATTN_Q_W), cur),
            pl.BlockSpec((tq, KV_BD_W), cur),
            pl.BlockSpec((ATTN_BLOCK, KV_BD_W), prev),
            pl.BlockSpec((tq, KV_BD_W), cur),
            pl.BlockSpec((ATTN_BLOCK, KV_BD_W), prev),
        ],
        out_specs=pl.BlockSpec((tq, ATTN_Q_W), cur),
        out_shape=jax.ShapeDtypeStruct(q.shape, BF16),
        compiler_params=pltpu.CompilerParams(dimension_semantics=("parallel", "parallel"), vmem_limit_bytes=VMEM_LIMIT),
        name="swa_attention",
    )(sinks, q, k, k, v, v)


def _mlstm_kernel(qk_ref, mv_ref, so_ref, grow_ref, gnext_ref, brow_ref, hn_ref, out_ref, state, m_scr, gate_scr):
    t = pl.program_id(1)
    TL = qk_ref.shape[0]
    L = M_CHUNK
    R = 2 * M_HEADS

    ri = lax.broadcasted_iota(jnp.int32, (L, L), 0)
    ci = lax.broadcasted_iota(jnp.int32, (L, L), 1)
    causal = ci <= ri
    triu = (ri <= ci).astype(BF16)
    lane_in_chunk = lax.broadcasted_iota(jnp.int32, (R, TL), 1) & (L - 1)
    lo_b = lax.broadcasted_iota(jnp.int32, (L, LANES), 1) < M_QK_DIM
    top = lax.broadcasted_iota(jnp.int32, (L, LANES), 0) < M_QK_DIM
    top2 = lax.broadcasted_iota(jnp.int32, (L, 2 * M_V_DIM), 0) < M_QK_DIM
    ones_v = jnp.ones((L, M_V_DIM), BF16)
    zero_b = jnp.zeros((), BF16)
    chunk_rows = [slice(c * L, (c + 1) * L) for c in range(TL // L)]

    def row_bcast(x8, h, n=L):
        return jnp.broadcast_to(x8[h:h + 1, :], (n, x8.shape[1]))

    def gate_scan(g_ref):
        gr = g_ref[...] + brow_ref[...]
        lf = jax.nn.log_sigmoid(gr)
        b8 = jnp.concatenate([sum(_dot(part, triu) for part in _split3(lf[:, rows])) for rows in chunk_rows], axis=1)
        b_all = pltpu.roll(b8, M_HEADS, 0)
        r_all = gr - b_all
        cm_all = r_all
        sh = 1
        while sh < L:
            cm_all = jnp.where(lane_in_chunk >= sh, jnp.maximum(cm_all, pltpu.roll(cm_all, sh, 1)), cm_all)
            sh *= 2
        return b_all, r_all, cm_all

    @pl.when(t == 0)
    def _():
        state[...] = jnp.zeros(state.shape, F32)
        m_scr[...] = jnp.zeros(m_scr.shape, F32)
        for i, x in enumerate(gate_scan(grow_ref)):
            gate_scr[i] = x

    b_all, r_all, cm_all = gate_scr[0], gate_scr[1], gate_scr[2]
    next_gates = gate_scan(gnext_ref)

    m = m_scr[...]
    gates = []
    for rows in chunk_rows:
        b, r, cm = b_all[:, rows], r_all[:, rows], cm_all[:, rows]
        rmax = jnp.broadcast_to(cm[:, L - 1:L], (R, L))
        g = jnp.broadcast_to(b[:, L - 1:L], (R, L))
        big_m = jnp.maximum(m, cm)
        inter = jnp.exp(m - big_m)
        emt = jnp.exp(-(b + big_m))
        w = jnp.exp(r - rmax)
        m_loc = g + rmax
        m_new = jnp.maximum(g + m, m_loc)
        decay = jnp.exp(g + m - m_new)
        scale = jnp.exp(m_loc - m_new)
        m = m_new
        gates.append((r, w, decay, scale, big_m, inter, emt))
    m_scr[...] = m

    for rows, (r, w, decay, scale, big_m, inter, emt) in zip(chunk_rows, gates):
        for p in range(M_HEADS // 2):
            pc = slice(p * LANES, (p + 1) * LANES)
            q_pair_b = qk_ref[rows, pc]
            k_pair_b = qk_ref[rows, M_QK_W + p * LANES:M_QK_W + (p + 1) * LANES]
            c_prev = state[p]
            c_prev_b = c_prev.astype(BF16)
            inter_cols = jnp.where(top, row_bcast(inter, 2 * p), row_bcast(inter, 2 * p + 1)).T
            qs_pair = (q_pair_b.astype(F32) * inter_cols).astype(BF16)
            v_exts = []
            for hh in range(2):
                hd = 2 * p + hh
                cols = slice(hd * M_V_DIM, (hd + 1) * M_V_DIM)
                sel = lo_b if hh == 0 else jnp.logical_not(lo_b)
                v_ext = jnp.concatenate([mv_ref[rows, cols], ones_v], axis=1)
                v_exts.append(v_ext)
                m_col, emt_col = (row_bcast(x8, hd).T for x8 in (big_m, emt))

                d_mat = jnp.exp(jnp.where(causal, row_bcast(r, hd), -jnp.inf) - m_col)
                k_h = jnp.where(sel, k_pair_b, zero_b)
                s_mat = (_dot_nt(q_pair_b, k_h) * d_mat).astype(BF16)
                qs = jnp.where(sel, qs_pair, zero_b)
                nd = _dot(jnp.concatenate([s_mat, qs], axis=1), jnp.concatenate([v_ext, c_prev_b], axis=0))
                hcell = nd[:, :M_V_DIM] / jnp.maximum(jnp.abs(nd[:, M_V_DIM:]), emt_col)
                cell = _rms(hcell) * hn_ref[:, cols]
                out_ref[rows, cols] = (so_ref[rows, cols].astype(F32) * cell).astype(BF16)

            k_t = k_pair_b.astype(F32).T
            w_s = jnp.where(top, row_bcast(w, 2 * p), row_bcast(w, 2 * p + 1))
            kw_t = (k_t * w_s).astype(BF16)
            lhs = jnp.concatenate([jnp.where(top, kw_t, zero_b), jnp.where(top, zero_b, kw_t)], axis=1)
            a = _dot(lhs, jnp.concatenate(v_exts, axis=0))
            dec = jnp.where(top2, row_bcast(decay, 2 * p, L)[:, :1], row_bcast(decay, 2 * p + 1, L)[:, :1])
            sc = jnp.where(top2, row_bcast(scale, 2 * p, L)[:, :1], row_bcast(scale, 2 * p + 1, L)[:, :1])
            state[p] = dec * c_prev + sc * a

    for i, x in enumerate(next_gates):
        gate_scr[i] = x


def _mlstm(qk, mv, so, grow, brow, hn, batch, seq):
    tl = MLSTM_TILE
    nt = seq // tl
    cur = lambda b, t: (b * nt + t, 0)
    const = lambda b, t: (0, 0)
    return pl.pallas_call(
        _mlstm_kernel,
        grid=(batch, nt),
        in_specs=[
            pl.BlockSpec((tl, 2 * M_QK_W), cur),
            pl.BlockSpec((tl, M_V_W), cur),
            pl.BlockSpec((tl, M_V_W), cur),
            pl.BlockSpec((2 * M_HEADS, tl), lambda b, t: (0, b * nt + t)),
            pl.BlockSpec((2 * M_HEADS, tl), lambda b, t: (0, b * nt + jnp.minimum(t + 1, nt - 1))),
            pl.BlockSpec(brow.shape, const),
            pl.BlockSpec(hn.shape, const),
        ],
        out_specs=pl.BlockSpec((tl, M_V_W), cur),
        out_shape=jax.ShapeDtypeStruct(mv.shape, BF16),
        scratch_shapes=[
            pltpu.VMEM((M_HEADS // 2, 2 * M_QK_DIM, 2 * M_V_DIM), F32),
            pltpu.VMEM((2 * M_HEADS, M_CHUNK), F32),
            pltpu.VMEM((3, 2 * M_HEADS, tl), F32),
        ],
        compiler_params=pltpu.CompilerParams(dimension_semantics=("parallel", "arbitrary"), vmem_limit_bytes=VMEM_LIMIT),
        name="mlstm",
    )(qk, mv, so, grow, grow, brow, hn)


def _ff_chunks():
    return [(c0, min(c0 + FF_CHUNK, D_FF)) for c0 in range(0, D_FF, FF_CHUNK)]


def _mix_ffn_kernel(x_ref, h_ref, a_ref, m_ref, gpost_ref, gpre_ref, gffn_ref, wg_ref, wa_ref, wm_ref, wo_ref,
                    wfi_ref, wfo_ref, o_ref, acc_ref):
    x = x_ref[...]
    h = h_ref[...]
    a = a_ref[...]
    m = m_ref[...]
    parts = []
    for c0 in range(0, D_MODEL, MERGE_CHUNK):
        cols = slice(c0, c0 + MERGE_CHUNK)
        gcols = slice(D_MODEL + c0, D_MODEL + c0 + MERGE_CHUNK)
        g_attn = jax.nn.sigmoid(_dot(h, wg_ref[:, cols]))
        g_mlstm = jax.nn.sigmoid(_dot(h, wg_ref[:, gcols]))
        parts.append((g_attn * _dot(a, wa_ref[:, cols]) + g_mlstm * _dot(m, wm_ref[:, cols])).astype(BF16))
    y = _dot(jnp.concatenate(parts, axis=1), wo_ref[...])
    x1 = x + _rms(y) * gpost_ref[...]

    h2 = (_rms(x1) * gpre_ref[...]).astype(BF16)
    for c0, c1 in _ff_chunks():
        gate = _dot(h2, wfi_ref[:, c0:c1])
        up = _dot(h2, wfi_ref[:, D_FF + c0:D_FF + c1])
        act = (gate * jax.nn.sigmoid(gate) * up).astype(BF16)
        part = _dot(act, wfo_ref[c0:c1, :])
        if c0 == 0:
            acc_ref[...] = part
        else:
            acc_ref[...] += part
    o_ref[...] = x1 + _rms(acc_ref[...]) * gffn_ref[...]


def _mix_ffn(x2, h, attn, ml, gpost, gpre, gffn, wg, wa, wm, wo, wfi, wfo):
    T = x2.shape[0]
    tm = TOK_TILE
    row = lambda i: (i, 0)
    const = lambda i: (0, 0)

    def wspec(w):
        return pl.BlockSpec(w.shape, const, pipeline_mode=pl.Buffered(1))

    gain = pl.BlockSpec((1, D_MODEL), const)
    return pl.pallas_call(
        _mix_ffn_kernel,
        grid=(T // tm,),
        in_specs=[
            pl.BlockSpec((tm, D_MODEL), row),
            pl.BlockSpec((tm, D_MODEL), row),
            pl.BlockSpec((tm, ATTN_Q_W), row),
            pl.BlockSpec((tm, M_V_W), row),
            gain, gain, gain,
            wspec(wg), wspec(wa), wspec(wm), wspec(wo), wspec(wfi), wspec(wfo),
        ],
        out_specs=pl.BlockSpec((tm, D_MODEL), row),
        out_shape=jax.ShapeDtypeStruct(x2.shape, F32),
        scratch_shapes=[pltpu.VMEM((tm, D_MODEL), F32)],
        compiler_params=pltpu.CompilerParams(dimension_semantics=("parallel",), vmem_limit_bytes=VMEM_LIMIT),
        name="mix_ffn",
    )(x2, h, attn, ml, gpost, gpre, gffn, wg, wa, wm, wo, wfi, wfo)


def _rope_tables(seq):
    f32 = np.float32
    inv_freq = (f32(ROPE_THETA) ** (-np.arange(0, HEAD_DIM, 2, dtype=f32) / f32(HEAD_DIM))).astype(f32)
    ang = np.arange(seq).astype(f32)[:, None] * inv_freq[None, :]
    emb = np.concatenate([ang, ang], axis=-1)
    cos = np.cos(emb).astype(f32)
    sin = np.sin(emb).astype(f32)
    first_half = np.arange(HEAD_DIM) < HEAD_DIM // 2
    sin_a = np.where(first_half, -sin, f32(0))
    sin_b = np.where(first_half, f32(0), sin)
    rep = LANES // HEAD_DIM
    return tuple(jnp.asarray(np.tile(t, (1, rep))) for t in (cos, sin_a, sin_b))


def _layer(x2, batch, seq, norm_pre_mix, norm_post_mix, norm_pre_ffn, norm_post_ffn, w_in, attn_sinks, conv_w,
           conv_b, b_igate, b_fgate, mlstm_head_norm, w_attn_branch, w_mlstm_branch, w_out, w_ffn_in, w_ffn_out):
    w_mix, wif, wg = _regroup_w_in(w_in)
    cos, sa, sb = _rope_tables(seq)

    h, q, k, v, qk, mv, so, grow = _inproj(
        x2, norm_pre_mix[None, :], cos, sa, sb, conv_w, conv_b[None, :], w_mix, wif, seq)
    attn = _attention(attn_sinks, q, k, v, batch, seq)
    bias = jnp.concatenate([b_igate, b_fgate])
    ml = _mlstm(qk, mv, so, grow, bias[:, None], mlstm_head_norm[None, :], batch, seq)
    return _mix_ffn(x2, h, attn, ml, norm_post_mix[None, :], norm_pre_ffn[None, :],
                    norm_post_ffn[None, :], wg, w_attn_branch.astype(BF16), w_mlstm_branch.astype(BF16),
                    w_out.astype(BF16), w_ffn_in.astype(BF16), w_ffn_out.astype(BF16))


def kernel(x, norm_pre_mix, norm_post_mix, norm_pre_ffn, norm_post_ffn, w_in, attn_sinks, conv_w, conv_b, b_igate,
           b_fgate, mlstm_head_norm, w_attn_branch, w_mlstm_branch, w_out, w_ffn_in, w_ffn_out):
    B, S, D = x.shape
    assert D == D_MODEL and WINDOW == ATTN_BLOCK == M_CHUNK
    assert S % max(INPROJ_TILE, ATTN_TILE, MLSTM_TILE) == 0 and (B * S) % TOK_TILE == 0
    x2 = x.reshape(B * S, D)
    for l in range(w_in.shape[0]):
        x2 = _layer(x2, B, S, norm_pre_mix[l], norm_post_mix[l], norm_pre_ffn[l], norm_post_ffn[l], w_in[l],
                    attn_sinks[l], conv_w[l], conv_b[l], b_igate[l], b_fgate[l], mlstm_head_norm[l],
                    w_attn_branch[l], w_mlstm_branch[l], w_out[l], w_ffn_in[l], w_ffn_out[l])
    return x2.reshape(B, S, D)
```

```python
import functools

import numpy as np

import jax
import jax.numpy as jnp
from jax import lax
from jax.experimental import pallas as pl
from jax.experimental.pallas import tpu as pltpu

D_MODEL = 1024
HEAD_DIM = 64
N_Q_HEADS = 8
N_KV_HEADS = 2
WINDOW = 128
ATTN_BLOCK = 128
ROPE_THETA = 10000.0
M_HEADS = 4
M_QK_DIM = 64
M_V_DIM = 128
M_CHUNK = 128
CONV_WIDTH = 4
D_FF = 2816
EPS = 1e-6

ATTN_Q_W = N_Q_HEADS * HEAD_DIM
ATTN_KV_W = N_KV_HEADS * HEAD_DIM
M_QK_W = M_HEADS * M_QK_DIM
M_V_W = M_HEADS * M_V_DIM
KV_BD_W = 2 * ATTN_KV_W * 2
O_MQK = ATTN_Q_W + 2 * ATTN_KV_W
O_MV = O_MQK + 2 * M_QK_W
O_GATES = O_MV + 2 * M_V_W
O_BRANCH = O_GATES + 2 * M_HEADS

LANES = 128
SUBLANES = 8
VMEM_LIMIT = 56 * 1024 * 1024
CONV_HALO = SUBLANES

TOK_TILE = 512
INPROJ_TILE = 1024
ATTN_TILE = 2048
MLSTM_TILE = 2048
FF_CHUNK = 256
MERGE_CHUNK = 256

F32 = jnp.float32
BF16 = jnp.bfloat16


def _dot(a, b):
    return jnp.dot(a, b, preferred_element_type=F32)


def _dot_nt(a, b):
    return lax.dot_general(a, b, (((1,), (1,)), ((), ())), preferred_element_type=F32)


def _rms(x):
    return x * lax.rsqrt(jnp.mean(x * x, axis=-1, keepdims=True) + EPS)


def _split3(x):
    hi = x.astype(BF16)
    r1 = x - hi.astype(F32)
    mid = r1.astype(BF16)
    lo = (r1 - mid.astype(F32)).astype(BF16)
    return hi, mid, lo


def _transpose_cast_kernel(wt_ref, o_ref):
    o_ref[...] = wt_ref[...].T.astype(BF16)


def _w_in_columns(w_in_t, col0, n_cols):
    k = w_in_t.shape[1]
    return pl.pallas_call(
        _transpose_cast_kernel,
        grid=(n_cols // LANES,),
        in_specs=[pl.BlockSpec((pl.Element(LANES), pl.Element(k)),
                               lambda j: (pl.multiple_of(col0 + j * LANES, SUBLANES), 0))],
        out_specs=pl.BlockSpec((k, LANES), lambda j: (0, j)),
        out_shape=jax.ShapeDtypeStruct((k, n_cols), BF16),
        compiler_params=pltpu.CompilerParams(dimension_semantics=("parallel",), vmem_limit_bytes=VMEM_LIMIT),
        name="w_in_columns",
    )(w_in_t)


def _inproj_kernel(x_ref, g_ref, cos_ref, sa_ref, sb_ref, cw_ref, cb_ref, w_ref, wif_ref,
                   h_ref, q_ref, k_ref, v_ref, qk_ref, mv_ref, so_ref, grow_ref, cbuf, *, n_seq_tiles):
    tm = x_ref.shape[0]
    h = (_rms(x_ref[...]) * g_ref[...]).astype(BF16)
    h_ref[...] = h
    cos = cos_ref[...]
    sa = sa_ref[...]
    sb = sb_ref[...]
    lane = lax.broadcasted_iota(jnp.int32, cos.shape, 1)
    lo = lane < HEAD_DIM

    def rope(t):
        return t * cos + pltpu.roll(t, LANES - HEAD_DIM // 2, 1) * sa + pltpu.roll(t, HEAD_DIM // 2, 1) * sb

    def store_block_diag(ref, t):
        tb = t.astype(BF16)
        rb = pltpu.roll(t, HEAD_DIM, 1).astype(BF16)
        zero = jnp.zeros((), BF16)
        for i, (src, keep_lo) in enumerate(((tb, True), (rb, False), (rb, True), (tb, False))):
            ref[:, i * LANES:(i + 1) * LANES] = jnp.where(lo, src, zero) if keep_lo else jnp.where(lo, zero, src)

    def attn_proj():
        a = _dot(h, w_ref[:, :O_MQK])
        for j in range(ATTN_Q_W // LANES):
            q_ref[:, j * LANES:(j + 1) * LANES] = (
                rope(a[:, j * LANES:(j + 1) * LANES]) * (HEAD_DIM ** -0.5)).astype(BF16)
        store_block_diag(k_ref, rope(a[:, ATTN_Q_W:ATTN_Q_W + LANES]))
        store_block_diag(v_ref, a[:, ATTN_Q_W + LANES:ATTN_Q_W + 2 * LANES])
        grow_ref[...] = _dot(h, wif_ref[...]).T[:2 * M_HEADS, :]

    @pl.when(pl.program_id(0) % n_seq_tiles == 0)
    def _():
        cbuf[0:CONV_HALO, :] = jnp.zeros((CONV_HALO, cbuf.shape[1]), F32)

    def mqk_proj(c0, c1):
        cbuf[CONV_HALO:CONV_HALO + tm, c0:c1] = _dot(h, w_ref[:, O_MQK + c0:O_MQK + c1])

    def conv_chunk(j0):
        cols = slice(j0 * LANES, (j0 + 1) * LANES)
        y = cb_ref[:, cols]
        for j in range(CONV_WIDTH):
            off = CONV_HALO - (CONV_WIDTH - 1) + j
            y = y + cw_ref[j:j + 1, cols] * cbuf[off:off + tm, cols]
        cbuf[0:CONV_HALO, cols] = cbuf[tm:tm + CONV_HALO, cols]
        y = y * jax.nn.sigmoid(y)
        if j0 * LANES >= M_QK_W:
            y = y * (M_QK_DIM ** -0.5)
        qk_ref[:, cols] = y.astype(BF16)

    def mlstm_vo_proj():
        vo = _dot(h, w_ref[:, O_MV:O_GATES])
        mv_ref[...] = vo[:, :M_V_W].astype(BF16)
        so_ref[...] = jax.nn.sigmoid(vo[:, M_V_W:]).astype(BF16)

    mqk_proj(0, M_QK_W)
    conv_chunk(0)
    conv_chunk(1)
    attn_proj()
    mqk_proj(M_QK_W, 2 * M_QK_W)
    conv_chunk(2)
    conv_chunk(3)
    mlstm_vo_proj()


def _inproj(x2, g, cos, sa, sb, cw, cb, w, wif, seq):
    T = x2.shape[0]
    tm = INPROJ_TILE
    n_seq_tiles = seq // tm
    row = lambda i: (i, 0)
    pos = lambda i: (i % n_seq_tiles, 0)
    const = lambda i: (0, 0)

    def wspec(w):
        return pl.BlockSpec(w.shape, const, pipeline_mode=pl.Buffered(1))

    out_shape = (
        jax.ShapeDtypeStruct((T, D_MODEL), BF16),
        jax.ShapeDtypeStruct((T, ATTN_Q_W), BF16),
        jax.ShapeDtypeStruct((T, KV_BD_W), BF16),
        jax.ShapeDtypeStruct((T, KV_BD_W), BF16),
        jax.ShapeDtypeStruct((T, 2 * M_QK_W), BF16),
        jax.ShapeDtypeStruct((T, M_V_W), BF16),
        jax.ShapeDtypeStruct((T, M_V_W), BF16),
        jax.ShapeDtypeStruct((2 * M_HEADS, T), F32),
    )
    out_specs = (
        pl.BlockSpec((tm, D_MODEL), row),
        pl.BlockSpec((tm, ATTN_Q_W), row),
        pl.BlockSpec((tm, KV_BD_W), row),
        pl.BlockSpec((tm, KV_BD_W), row),
        pl.BlockSpec((tm, 2 * M_QK_W), row),
        pl.BlockSpec((tm, M_V_W), row),
        pl.BlockSpec((tm, M_V_W), row),
        pl.BlockSpec((2 * M_HEADS, tm), lambda i: (0, i)),
    )
    in_specs = [
        pl.BlockSpec((tm, D_MODEL), row),
        pl.BlockSpec((1, D_MODEL), const),
        pl.BlockSpec((tm, LANES), pos),
        pl.BlockSpec((tm, LANES), pos),
        pl.BlockSpec((tm, LANES), pos),
        pl.BlockSpec(cw.shape, const),
        pl.BlockSpec(cb.shape, const),
        wspec(w), wspec(wif),
    ]
    return pl.pallas_call(
        functools.partial(_inproj_kernel, n_seq_tiles=n_seq_tiles),
        grid=(T // tm,),
        in_specs=in_specs,
        out_specs=out_specs,
        out_shape=out_shape,
        scratch_shapes=[pltpu.VMEM((CONV_HALO + tm, 2 * M_QK_W), F32)],
        compiler_params=pltpu.CompilerParams(dimension_semantics=("arbitrary",), vmem_limit_bytes=VMEM_LIMIT),
        name="inproj",
    )(x2, g, cos, sa, sb, cw, cb, w, wif)


def _attn_kernel(sink_ref, q_ref, kc_ref, kp_ref, vc_ref, vp_ref, o_ref):
    t = pl.program_id(1)
    blk = ATTN_BLOCK
    n_blk = q_ref.shape[0] // blk
    lo_q = lax.broadcasted_iota(jnp.int32, (blk, LANES), 1) < HEAD_DIM
    ri = lax.broadcasted_iota(jnp.int32, (blk, blk), 0)
    ci = lax.broadcasted_iota(jnp.int32, (blk, blk), 1)
    own = ci <= ri
    prev_ok = ci > ri + jnp.where(t > 0, 0, blk)
    zero = jnp.zeros((), BF16)
    row_is_a = lax.broadcasted_iota(jnp.int32, (4 * blk, LANES), 0) < 2 * blk
    lane_is_a = lax.broadcasted_iota(jnp.int32, (4 * blk, LANES), 1) < HEAD_DIM
    ones_bd = (row_is_a == lane_is_a).astype(BF16)

    for n in range(n_blk):
        rows = slice(n * blk, (n + 1) * blk)
        k_prev = kp_ref if n == 0 else kc_ref.at[(n - 1) * blk:n * blk, :]
        v_prev = vp_ref if n == 0 else vc_ref.at[(n - 1) * blk:n * blk, :]
        for j in range(N_KV_HEADS):
            kv_cols = [slice((2 * j + i) * LANES, (2 * j + i + 1) * LANES) for i in range(2)]
            k_bd = jnp.concatenate([x for c in kv_cols for x in (k_prev[:, c], kc_ref[rows, c])], axis=0)
            v_bd = jnp.concatenate([x for c in kv_cols for x in (v_prev[:, c], vc_ref[rows, c])], axis=0)
            c0 = 2 * j * LANES
            q2 = jnp.concatenate([q_ref[rows, c0:c0 + LANES], q_ref[rows, c0 + LANES:c0 + 2 * LANES]], axis=0)
            s = _dot_nt(q2, k_bd)
            e_rows, sink_rows = [], []
            for p in range(2):
                e_cols, sink_cols = [], []
                for hh in range(2):
                    sink = sink_ref[4 * j + 2 * p + hh]
                    s_prev = s[p * blk:(p + 1) * blk, 2 * hh * blk:(2 * hh + 1) * blk]
                    s_own = s[p * blk:(p + 1) * blk, (2 * hh + 1) * blk:(2 * hh + 2) * blk]
                    if n == 0:
                        s_prev = jnp.where(prev_ok, s_prev, -jnp.inf)
                    sc = jnp.where(own, s_own, s_prev)
                    m = jnp.maximum(jnp.max(sc, axis=-1, keepdims=True), sink)
                    eb = jnp.exp(sc - m).astype(BF16)
                    e_cols += [jnp.where(own, zero, eb), jnp.where(own, eb, zero)]
                    sink_cols.append(jnp.exp(sink - m))
                e_rows.append(jnp.concatenate(e_cols, axis=1))
                sink_rows.append(jnp.where(lo_q, sink_cols[0], sink_cols[1]))
            o = _dot(jnp.concatenate(e_rows, axis=0), jnp.concatenate([v_bd, ones_bd], axis=1))
            for p in range(2):
                op = o[p * blk:(p + 1) * blk]
                o_ref[rows, c0 + p * LANES:c0 + (p + 1) * LANES] = (
                    op[:, :LANES] / (op[:, LANES:] + sink_rows[p])).astype(BF16)


def _attention(sinks, q, k, v, batch, seq):
    tq = ATTN_TILE
    nt = seq // tq
    per = tq // ATTN_BLOCK
    cur = lambda b, t: (b * nt + t, 0)
    prev = lambda b, t: (jnp.maximum((b * nt + t) * per - 1, 0), 0)
    return pl.pallas_call(
        _attn_kernel,
        grid=(batch, nt),
        in_specs=[
            pl.BlockSpec(memory_space=pltpu.SMEM),
            pl.BlockSpec((tq, ATTN_Q_W), cur),
            pl.BlockSpec((tq, KV_BD_W), cur),
            pl.BlockSpec((ATTN_BLOCK, KV_BD_W), prev),
            pl.BlockSpec((tq, KV_BD_W), cur),
            pl.BlockSpec((ATTN_BLOCK, KV_BD_W), prev),
        ],
        out_specs=pl.BlockSpec((tq, ATTN_Q_W), cur),
        out_shape=jax.ShapeDtypeStruct(q.shape, BF16),
        compiler_params=pltpu.CompilerParams(dimension_semantics=("parallel", "parallel"), vmem_limit_bytes=VMEM_LIMIT),
        name="swa_attention",
    )(sinks, q, k, k, v, v)


def _mlstm_kernel(qk_ref, mv_ref, so_ref, grow_ref, gnext_ref, brow_ref, hn_ref, out_ref, state, m_scr, gate_scr):
    t = pl.program_id(1)
    TL = qk_ref.shape[0]
    L = M_CHUNK
    R = 2 * M_HEADS

    ri = lax.broadcasted_iota(jnp.int32, (L, L), 0)
    ci = lax.broadcasted_iota(jnp.int32, (L, L), 1)
    causal = ci <= ri
    triu = (ri <= ci).astype(BF16)
    lane_in_chunk = lax.broadcasted_iota(jnp.int32, (R, TL), 1) & (L - 1)
    lo_b = lax.broadcasted_iota(jnp.int32, (L, LANES), 1) < M_QK_DIM
    top = lax.broadcasted_iota(jnp.int32, (L, LANES), 0) < M_QK_DIM
    top2 = lax.broadcasted_iota(jnp.int32, (L, 2 * M_V_DIM), 0) < M_QK_DIM
    ones_v = jnp.ones((L, M_V_DIM), BF16)
    zero_b = jnp.zeros((), BF16)
    chunk_rows = [slice(c * L, (c + 1) * L) for c in range(TL // L)]

    def row_bcast(x8, h, n=L):
        return jnp.broadcast_to(x8[h:h + 1, :], (n, x8.shape[1]))

    def gate_scan(g_ref):
        gr = g_ref[...] + brow_ref[...]
        lf = jax.nn.log_sigmoid(gr)
        b8 = jnp.concatenate([sum(_dot(part, triu) for part in _split3(lf[:, rows])) for rows in chunk_rows], axis=1)
        b_all = pltpu.roll(b8, M_HEADS, 0)
        r_all = gr - b_all
        cm_all = r_all
        sh = 1
        while sh < L:
            cm_all = jnp.where(lane_in_chunk >= sh, jnp.maximum(cm_all, pltpu.roll(cm_all, sh, 1)), cm_all)
            sh *= 2
        return b_all, r_all, cm_all

    @pl.when(t == 0)
    def _():
        state[...] = jnp.zeros(state.shape, F32)
        m_scr[...] = jnp.zeros(m_scr.shape, F32)
        for i, x in enumerate(gate_scan(grow_ref)):
            gate_scr[i] = x

    b_all, r_all, cm_all = gate_scr[0], gate_scr[1], gate_scr[2]
    next_gates = gate_scan(gnext_ref)

    m = m_scr[...]
    gates = []
    for rows in chunk_rows:
        b, r, cm = b_all[:, rows], r_all[:, rows], cm_all[:, rows]
        rmax = jnp.broadcast_to(cm[:, L - 1:L], (R, L))
        g = jnp.broadcast_to(b[:, L - 1:L], (R, L))
        big_m = jnp.maximum(m, cm)
        inter = jnp.exp(m - big_m)
        emt = jnp.exp(-(b + big_m))
        w = jnp.exp(r - rmax)
        m_loc = g + rmax
        m_new = jnp.maximum(g + m, m_loc)
        decay = jnp.exp(g + m - m_new)
        scale = jnp.exp(m_loc - m_new)
        m = m_new
        gates.append((r, w, decay, scale, big_m, inter, emt))
    m_scr[...] = m

    for rows, (r, w, decay, scale, big_m, inter, emt) in zip(chunk_rows, gates):
        for p in range(M_HEADS // 2):
            pc = slice(p * LANES, (p + 1) * LANES)
            q_pair_b = qk_ref[rows, pc]
            k_pair_b = qk_ref[rows, M_QK_W + p * LANES:M_QK_W + (p + 1) * LANES]
            c_prev = state[p]
            c_prev_b = c_prev.astype(BF16)
            inter_cols = jnp.where(top, row_bcast(inter, 2 * p), row_bcast(inter, 2 * p + 1)).T
            qs_pair = (q_pair_b.astype(F32) * inter_cols).astype(BF16)
            v_exts = []
            for hh in range(2):
                hd = 2 * p + hh
                cols = slice(hd * M_V_DIM, (hd + 1) * M_V_DIM)
                sel = lo_b if hh == 0 else jnp.logical_not(lo_b)
                v_ext = jnp.concatenate([mv_ref[rows, cols], ones_v], axis=1)
                v_exts.append(v_ext)
                m_col, emt_col = (row_bcast(x8, hd).T for x8 in (big_m, emt))

                d_mat = jnp.exp(jnp.where(causal, row_bcast(r, hd), -jnp.inf) - m_col)
                k_h = jnp.where(sel, k_pair_b, zero_b)
                s_mat = (_dot_nt(q_pair_b, k_h) * d_mat).astype(BF16)
                qs = jnp.where(sel, qs_pair, zero_b)
                nd = _dot(jnp.concatenate([s_mat, qs], axis=1), jnp.concatenate([v_ext, c_prev_b], axis=0))
                hcell = nd[:, :M_V_DIM] / jnp.maximum(jnp.abs(nd[:, M_V_DIM:]), emt_col)
                cell = _rms(hcell) * hn_ref[:, cols]
                out_ref[rows, cols] = (so_ref[rows, cols].astype(F32) * cell).astype(BF16)

            k_t = k_pair_b.astype(F32).T
            w_s = jnp.where(top, row_bcast(w, 2 * p), row_bcast(w, 2 * p + 1))
            kw_t = (k_t * w_s).astype(BF16)
            lhs = jnp.concatenate([jnp.where(top, kw_t, zero_b), jnp.where(top, zero_b, kw_t)], axis=1)
            a = _dot(lhs, jnp.concatenate(v_exts, axis=0))
            dec = jnp.where(top2, row_bcast(decay, 2 * p, L)[:, :1], row_bcast(decay, 2 * p + 1, L)[:, :1])
            sc = jnp.where(top2, row_bcast(scale, 2 * p, L)[:, :1], row_bcast(scale, 2 * p + 1, L)[:, :1])
            state[p] = dec * c_prev + sc * a

    for i, x in enumerate(next_gates):
        gate_scr[i] = x


def _mlstm(qk, mv, so, grow, brow, hn, batch, seq):
    tl = MLSTM_TILE
    nt = seq // tl
    cur = lambda b, t: (b * nt + t, 0)
    const = lambda b, t: (0, 0)
    return pl.pallas_call(
        _mlstm_kernel,
        grid=(batch, nt),
        in_specs=[
            pl.BlockSpec((tl, 2 * M_QK_W), cur),
            pl.BlockSpec((tl, M_V_W), cur),
            pl.BlockSpec((tl, M_V_W), cur),
            pl.BlockSpec((2 * M_HEADS, tl), lambda b, t: (0, b * nt + t)),
            pl.BlockSpec((2 * M_HEADS, tl), lambda b, t: (0, b * nt + jnp.minimum(t + 1, nt - 1))),
            pl.BlockSpec(brow.shape, const),
            pl.BlockSpec(hn.shape, const),
        ],
        out_specs=pl.BlockSpec((tl, M_V_W), cur),
        out_shape=jax.ShapeDtypeStruct(mv.shape, BF16),
        scratch_shapes=[
            pltpu.VMEM((M_HEADS // 2, 2 * M_QK_DIM, 2 * M_V_DIM), F32),
            pltpu.VMEM((2 * M_HEADS, M_CHUNK), F32),
            pltpu.VMEM((3, 2 * M_HEADS, tl), F32),
        ],
        compiler_params=pltpu.CompilerParams(dimension_semantics=("parallel", "arbitrary"), vmem_limit_bytes=VMEM_LIMIT),
        name="mlstm",
    )(qk, mv, so, grow, grow, brow, hn)


def _ff_chunks():
    return [(c0, min(c0 + FF_CHUNK, D_FF)) for c0 in range(0, D_FF, FF_CHUNK)]


def _mix_ffn_kernel(x_ref, h_ref, a_ref, m_ref, gpost_ref, gpre_ref, gffn_ref, wg_ref, wa_ref, wm_ref, wo_ref,
                    wfi_ref, wfo_ref, o_ref, acc_ref):
    x = x_ref[...]
    h = h_ref[...]
    a = a_ref[...]
    m = m_ref[...]
    parts = []
    for c0 in range(0, D_MODEL, MERGE_CHUNK):
        cols = slice(c0, c0 + MERGE_CHUNK)
        gcols = slice(D_MODEL + c0, D_MODEL + c0 + MERGE_CHUNK)
        g_attn = jax.nn.sigmoid(_dot(h, wg_ref[:, cols]))
        g_mlstm = jax.nn.sigmoid(_dot(h, wg_ref[:, gcols]))
        parts.append((g_attn * _dot(a, wa_ref[:, cols]) + g_mlstm * _dot(m, wm_ref[:, cols])).astype(BF16))
    y = _dot(jnp.concatenate(parts, axis=1), wo_ref[...])
    x1 = x + _rms(y) * gpost_ref[...]

    h2 = (_rms(x1) * gpre_ref[...]).astype(BF16)
    for c0, c1 in _ff_chunks():
        gate = _dot(h2, wfi_ref[:, c0:c1])
        up = _dot(h2, wfi_ref[:, D_FF + c0:D_FF + c1])
        act = (gate * jax.nn.sigmoid(gate) * up).astype(BF16)
        part = _dot(act, wfo_ref[c0:c1, :])
        if c0 == 0:
            acc_ref[...] = part
        else:
            acc_ref[...] += part
    o_ref[...] = x1 + _rms(acc_ref[...]) * gffn_ref[...]


def _mix_ffn(x2, h, attn, ml, gpost, gpre, gffn, wg, wa, wm, wo, wfi, wfo):
    T = x2.shape[0]
    tm = TOK_TILE
    row = lambda i: (i, 0)
    const = lambda i: (0, 0)

    def wspec(w):
        return pl.BlockSpec(w.shape, const, pipeline_mode=pl.Buffered(1))

    gain = pl.BlockSpec((1, D_MODEL), const)
    return pl.pallas_call(
        _mix_ffn_kernel,
        grid=(T // tm,),
        in_specs=[
            pl.BlockSpec((tm, D_MODEL), row),
            pl.BlockSpec((tm, D_MODEL), row),
            pl.BlockSpec((tm, ATTN_Q_W), row),
            pl.BlockSpec((tm, M_V_W), row),
            gain, gain, gain,
            wspec(wg), wspec(wa), wspec(wm), wspec(wo), wspec(wfi), wspec(wfo),
        ],
        out_specs=pl.BlockSpec((tm, D_MODEL), row),
        out_shape=jax.ShapeDtypeStruct(x2.shape, F32),
        scratch_shapes=[pltpu.VMEM((tm, D_MODEL), F32)],
        compiler_params=pltpu.CompilerParams(dimension_semantics=("parallel",), vmem_limit_bytes=VMEM_LIMIT),
        name="mix_ffn",
    )(x2, h, attn, ml, gpost, gpre, gffn, wg, wa, wm, wo, wfi, wfo)


def _rope_tables(seq):
    f32 = np.float32
    inv_freq = (f32(ROPE_THETA) ** (-np.arange(0, HEAD_DIM, 2, dtype=f32) / f32(HEAD_DIM))).astype(f32)
    ang = np.arange(seq).astype(f32)[:, None] * inv_freq[None, :]
    emb = np.concatenate([ang, ang], axis=-1)
    cos = np.cos(emb).astype(f32)
    sin = np.sin(emb).astype(f32)
    first_half = np.arange(HEAD_DIM) < HEAD_DIM // 2
    sin_a = np.where(first_half, -sin, f32(0))
    sin_b = np.where(first_half, f32(0), sin)
    rep = LANES // HEAD_DIM
    return tuple(jnp.asarray(np.tile(t, (1, rep))) for t in (cos, sin_a, sin_b))


def _layer(x2, batch, seq, norm_pre_mix, norm_post_mix, norm_pre_ffn, norm_post_ffn, w_in, attn_sinks, conv_w,
           conv_b, b_igate, b_fgate, mlstm_head_norm, w_attn_branch, w_mlstm_branch, w_out, w_ffn_in, w_ffn_out):
    w_in_t = jnp.transpose(w_in)
    w_mix = _w_in_columns(w_in_t, 0, O_GATES)
    wif = jnp.pad(w_in[:, O_GATES:O_BRANCH], ((0, 0), (0, LANES - 2 * M_HEADS))).astype(BF16)
    wg = _w_in_columns(w_in_t, O_BRANCH, 2 * D_MODEL)
    cos, sa, sb = _rope_tables(seq)

    h, q, k, v, qk, mv, so, grow = _inproj(
        x2, norm_pre_mix[None, :], cos, sa, sb, conv_w, conv_b[None, :], w_mix, wif, seq)
    attn = _attention(attn_sinks, q, k, v, batch, seq)
    bias = jnp.concatenate([b_igate, b_fgate])
    ml = _mlstm(qk, mv, so, grow, bias[:, None], mlstm_head_norm[None, :], batch, seq)
    return _mix_ffn(x2, h, attn, ml, norm_post_mix[None, :], norm_pre_ffn[None, :],
                    norm_post_ffn[None, :], wg, w_attn_branch.astype(BF16), w_mlstm_branch.astype(BF16),
                    w_out.astype(BF16), w_ffn_in.astype(BF16), w_ffn_out.astype(BF16))


def kernel(x, norm_pre_mix, norm_post_mix, norm_pre_ffn, norm_post_ffn, w_in, attn_sinks, conv_w, conv_b, b_igate,
           b_fgate, mlstm_head_norm, w_attn_branch, w_mlstm_branch, w_out, w_ffn_in, w_ffn_out):
    B, S, D = x.shape
    assert D == D_MODEL and WINDOW == ATTN_BLOCK == M_CHUNK
    assert S % max(INPROJ_TILE, ATTN_TILE, MLSTM_TILE) == 0 and (B * S) % TOK_TILE == 0
    x2 = x.reshape(B * S, D)
    for l in range(w_in.shape[0]):
        x2 = _layer(x2, B, S, norm_pre_mix[l], norm_post_mix[l], norm_pre_ffn[l], norm_post_ffn[l], w_in[l],
                    attn_sinks[l], conv_w[l], conv_b[l], b_igate[l], b_fgate[l], mlstm_head_norm[l],
                    w_attn_branch[l], w_mlstm_branch[l], w_out[l], w_ffn_in[l], w_ffn_out[l])
    return x2.reshape(B, S, D)
```

```python
import functools

import numpy as np

import jax
import jax.numpy as jnp
from jax import lax
from jax.experimental import pallas as pl
from jax.experimental.pallas import tpu as pltpu

D_MODEL = 1024
HEAD_DIM = 64
N_Q_HEADS = 8
N_KV_HEADS = 2
WINDOW = 128
ATTN_BLOCK = 128
ROPE_THETA = 10000.0
M_HEADS = 4
M_QK_DIM = 64
M_V_DIM = 128
M_CHUNK = 128
CONV_WIDTH = 4
D_FF = 2816
EPS = 1e-6

ATTN_Q_W = N_Q_HEADS * HEAD_DIM
ATTN_KV_W = N_KV_HEADS * HEAD_DIM
M_QK_W = M_HEADS * M_QK_DIM
M_V_W = M_HEADS * M_V_DIM
KV_BD_W = 2 * ATTN_KV_W * 2
O_MQK = ATTN_Q_W + 2 * ATTN_KV_W
O_MV = O_MQK + 2 * M_QK_W
O_GATES = O_MV + 2 * M_V_W
O_BRANCH = O_GATES + 2 * M_HEADS

LANES = 128
SUBLANES = 8
VMEM_LIMIT = 56 * 1024 * 1024
CONV_HALO = SUBLANES

TOK_TILE = 512
INPROJ_TILE = 1024
ATTN_TILE = 2048
MLSTM_TILE = 2048
FF_CHUNK = 256
MERGE_CHUNK = 256
W_COLS_PER_STEP = 768

F32 = jnp.float32
BF16 = jnp.bfloat16


def _dot(a, b):
    return jnp.dot(a, b, preferred_element_type=F32)


def _dot_nt(a, b):
    return lax.dot_general(a, b, (((1,), (1,)), ((), ())), preferred_element_type=F32)


def _rms(x):
    return x * lax.rsqrt(jnp.mean(x * x, axis=-1, keepdims=True) + EPS)


def _split3(x):
    hi = x.astype(BF16)
    r1 = x - hi.astype(F32)
    mid = r1.astype(BF16)
    lo = (r1 - mid.astype(F32)).astype(BF16)
    return hi, mid, lo


def _transpose_cast_kernel(wt_ref, o_ref):
    o_ref[...] = wt_ref[...].T.astype(BF16)


def _w_in_columns(w_in_t, col0, n_cols):
    k = w_in_t.shape[1]
    step = max(c for c in range(LANES, W_COLS_PER_STEP + 1, LANES) if n_cols % c == 0)
    return pl.pallas_call(
        _transpose_cast_kernel,
        grid=(n_cols // step,),
        in_specs=[pl.BlockSpec((pl.Element(step), pl.Element(k)),
                               lambda j: (pl.multiple_of(col0 + j * step, SUBLANES), 0))],
        out_specs=pl.BlockSpec((k, step), lambda j: (0, j)),
        out_shape=jax.ShapeDtypeStruct((k, n_cols), BF16),
        compiler_params=pltpu.CompilerParams(dimension_semantics=("parallel",), vmem_limit_bytes=VMEM_LIMIT),
        name="w_in_columns",
    )(w_in_t)


def _inproj_kernel(x_ref, g_ref, cos_ref, sa_ref, sb_ref, cw_ref, cb_ref, w_ref, wif_ref,
                   h_ref, q_ref, k_ref, v_ref, qk_ref, mv_ref, so_ref, grow_ref, cbuf, *, n_seq_tiles):
    tm = x_ref.shape[0]
    h = (_rms(x_ref[...]) * g_ref[...]).astype(BF16)
    h_ref[...] = h
    cos = cos_ref[...]
    sa = sa_ref[...]
    sb = sb_ref[...]
    lane = lax.broadcasted_iota(jnp.int32, cos.shape, 1)
    lo = lane < HEAD_DIM

    def rope(t):
        return t * cos + pltpu.roll(t, LANES - HEAD_DIM // 2, 1) * sa + pltpu.roll(t, HEAD_DIM // 2, 1) * sb

    def store_block_diag(ref, t):
        tb = t.astype(BF16)
        rb = pltpu.roll(t, HEAD_DIM, 1).astype(BF16)
        zero = jnp.zeros((), BF16)
        for i, (src, keep_lo) in enumerate(((tb, True), (rb, False), (rb, True), (tb, False))):
            ref[:, i * LANES:(i + 1) * LANES] = jnp.where(lo, src, zero) if keep_lo else jnp.where(lo, zero, src)

    def attn_proj():
        a = _dot(h, w_ref[:, :O_MQK])
        for j in range(ATTN_Q_W // LANES):
            q_ref[:, j * LANES:(j + 1) * LANES] = (
                rope(a[:, j * LANES:(j + 1) * LANES]) * (HEAD_DIM ** -0.5)).astype(BF16)
        store_block_diag(k_ref, rope(a[:, ATTN_Q_W:ATTN_Q_W + LANES]))
        store_block_diag(v_ref, a[:, ATTN_Q_W + LANES:ATTN_Q_W + 2 * LANES])
        grow_ref[...] = _dot(h, wif_ref[...]).T[:2 * M_HEADS, :]

    @pl.when(pl.program_id(0) % n_seq_tiles == 0)
    def _():
        cbuf[0:CONV_HALO, :] = jnp.zeros((CONV_HALO, cbuf.shape[1]), F32)

    def mqk_proj(c0, c1):
        cbuf[CONV_HALO:CONV_HALO + tm, c0:c1] = _dot(h, w_ref[:, O_MQK + c0:O_MQK + c1])

    def conv_chunk(j0):
        cols = slice(j0 * LANES, (j0 + 1) * LANES)
        y = cb_ref[:, cols]
        for j in range(CONV_WIDTH):
            off = CONV_HALO - (CONV_WIDTH - 1) + j
            y = y + cw_ref[j:j + 1, cols] * cbuf[off:off + tm, cols]
        cbuf[0:CONV_HALO, cols] = cbuf[tm:tm + CONV_HALO, cols]
        y = y * jax.nn.sigmoid(y)
        if j0 * LANES >= M_QK_W:
            y = y * (M_QK_DIM ** -0.5)
        qk_ref[:, cols] = y.astype(BF16)

    def mlstm_vo_proj():
        vo = _dot(h, w_ref[:, O_MV:O_GATES])
        mv_ref[...] = vo[:, :M_V_W].astype(BF16)
        so_ref[...] = jax.nn.sigmoid(vo[:, M_V_W:]).astype(BF16)

    mqk_proj(0, M_QK_W)
    conv_chunk(0)
    conv_chunk(1)
    attn_proj()
    mqk_proj(M_QK_W, 2 * M_QK_W)
    conv_chunk(2)
    conv_chunk(3)
    mlstm_vo_proj()


def _inproj(x2, g, cos, sa, sb, cw, cb, w, wif, seq):
    T = x2.shape[0]
    tm = INPROJ_TILE
    n_seq_tiles = seq // tm
    row = lambda i: (i, 0)
    pos = lambda i: (i % n_seq_tiles, 0)
    const = lambda i: (0, 0)

    def wspec(w):
        return pl.BlockSpec(w.shape, const, pipeline_mode=pl.Buffered(1))

    out_shape = (
        jax.ShapeDtypeStruct((T, D_MODEL), BF16),
        jax.ShapeDtypeStruct((T, ATTN_Q_W), BF16),
        jax.ShapeDtypeStruct((T, KV_BD_W), BF16),
        jax.ShapeDtypeStruct((T, KV_BD_W), BF16),
        jax.ShapeDtypeStruct((T, 2 * M_QK_W), BF16),
        jax.ShapeDtypeStruct((T, M_V_W), BF16),
        jax.ShapeDtypeStruct((T, M_V_W), BF16),
        jax.ShapeDtypeStruct((2 * M_HEADS, T), F32),
    )
    out_specs = (
        pl.BlockSpec((tm, D_MODEL), row),
        pl.BlockSpec((tm, ATTN_Q_W), row),
        pl.BlockSpec((tm, KV_BD_W), row),
        pl.BlockSpec((tm, KV_BD_W), row),
        pl.BlockSpec((tm, 2 * M_QK_W), row),
        pl.BlockSpec((tm, M_V_W), row),
        pl.BlockSpec((tm, M_V_W), row),
        pl.BlockSpec((2 * M_HEADS, tm), lambda i: (0, i)),
    )
    in_specs = [
        pl.BlockSpec((tm, D_MODEL), row),
        pl.BlockSpec((1, D_MODEL), const),
        pl.BlockSpec((tm, LANES), pos),
        pl.BlockSpec((tm, LANES), pos),
        pl.BlockSpec((tm, LANES), pos),
        pl.BlockSpec(cw.shape, const),
        pl.BlockSpec(cb.shape, const),
        wspec(w), wspec(wif),
    ]
    return pl.pallas_call(
        functools.partial(_inproj_kernel, n_seq_tiles=n_seq_tiles),
        grid=(T // tm,),
        in_specs=in_specs,
        out_specs=out_specs,
        out_shape=out_shape,
        scratch_shapes=[pltpu.VMEM((CONV_HALO + tm, 2 * M_QK_W), F32)],
        compiler_params=pltpu.CompilerParams(dimension_semantics=("arbitrary",), vmem_limit_bytes=VMEM_LIMIT),
        name="inproj",
    )(x2, g, cos, sa, sb, cw, cb, w, wif)


def _attn_kernel(sink_ref, q_ref, kc_ref, kp_ref, vc_ref, vp_ref, o_ref):
    t = pl.program_id(1)
    blk = ATTN_BLOCK
    n_blk = q_ref.shape[0] // blk
    lo_q = lax.broadcasted_iota(jnp.int32, (blk, LANES), 1) < HEAD_DIM
    ri = lax.broadcasted_iota(jnp.int32, (blk, blk), 0)
    ci = lax.broadcasted_iota(jnp.int32, (blk, blk), 1)
    own = ci <= ri
    prev_ok = ci > ri + jnp.where(t > 0, 0, blk)
    zero = jnp.zeros((), BF16)
    row_is_a = lax.broadcasted_iota(jnp.int32, (4 * blk, LANES), 0) < 2 * blk
    lane_is_a = lax.broadcasted_iota(jnp.int32, (4 * blk, LANES), 1) < HEAD_DIM
    ones_bd = (row_is_a == lane_is_a).astype(BF16)

    for n in range(n_blk):
        rows = slice(n * blk, (n + 1) * blk)
        k_prev = kp_ref if n == 0 else kc_ref.at[(n - 1) * blk:n * blk, :]
        v_prev = vp_ref if n == 0 else vc_ref.at[(n - 1) * blk:n * blk, :]
        for j in range(N_KV_HEADS):
            kv_cols = [slice((2 * j + i) * LANES, (2 * j + i + 1) * LANES) for i in range(2)]
            k_bd = jnp.concatenate([x for c in kv_cols for x in (k_prev[:, c], kc_ref[rows, c])], axis=0)
            v_bd = jnp.concatenate([x for c in kv_cols for x in (v_prev[:, c], vc_ref[rows, c])], axis=0)
            c0 = 2 * j * LANES
            q2 = jnp.concatenate([q_ref[rows, c0:c0 + LANES], q_ref[rows, c0 + LANES:c0 + 2 * LANES]], axis=0)
            s = _dot_nt(q2, k_bd)
            e_rows, sink_rows = [], []
            for p in range(2):
                e_cols, sink_cols = [], []
                for hh in range(2):
                    sink = sink_ref[4 * j + 2 * p + hh]
                    s_prev = s[p * blk:(p + 1) * blk, 2 * hh * blk:(2 * hh + 1) * blk]
                    s_own = s[p * blk:(p + 1) * blk, (2 * hh + 1) * blk:(2 * hh + 2) * blk]
                    if n == 0:
                        s_prev = jnp.where(prev_ok, s_prev, -jnp.inf)
                    sc = jnp.where(own, s_own, s_prev)
                    m = jnp.maximum(jnp.max(sc, axis=-1, keepdims=True), sink)
                    eb = jnp.exp(sc - m).astype(BF16)
                    e_cols += [jnp.where(own, zero, eb), jnp.where(own, eb, zero)]
                    sink_cols.append(jnp.exp(sink - m))
                e_rows.append(jnp.concatenate(e_cols, axis=1))
                sink_rows.append(jnp.where(lo_q, sink_cols[0], sink_cols[1]))
            o = _dot(jnp.concatenate(e_rows, axis=0), jnp.concatenate([v_bd, ones_bd], axis=1))
            for p in range(2):
                op = o[p * blk:(p + 1) * blk]
                o_ref[rows, c0 + p * LANES:c0 + (p + 1) * LANES] = (
                    op[:, :LANES] / (op[:, LANES:] + sink_rows[p])).astype(BF16)


def _attention(sinks, q, k, v, batch, seq):
    tq = ATTN_TILE
    nt = seq // tq
    per = tq // ATTN_BLOCK
    cur = lambda b, t: (b * nt + t, 0)
    prev = lambda b, t: (jnp.maximum((b * nt + t) * per - 1, 0), 0)
    return pl.pallas_call(
        _attn_kernel,
        grid=(batch, nt),
        in_specs=[
            pl.BlockSpec(memory_space=pltpu.SMEM),
            pl.BlockSpec((tq, ATTN_Q_W), cur),
            pl.BlockSpec((tq, KV_BD_W), cur),
            pl.BlockSpec((ATTN_BLOCK, KV_BD_W), prev),
            pl.BlockSpec((tq, KV_BD_W), cur),
            pl.BlockSpec((ATTN_BLOCK, KV_BD_W), prev),
        ],
        out_specs=pl.BlockSpec((tq, ATTN_Q_W), cur),
        out_shape=jax.ShapeDtypeStruct(q.shape, BF16),
        compiler_params=pltpu.CompilerParams(dimension_semantics=("parallel", "parallel"), vmem_limit_bytes=VMEM_LIMIT),
        name="swa_attention",
    )(sinks, q, k, k, v, v)


def _mlstm_kernel(qk_ref, mv_ref, so_ref, grow_ref, gnext_ref, brow_ref, hn_ref, out_ref, state, m_scr, gate_scr):
    t = pl.program_id(1)
    TL = qk_ref.shape[0]
    L = M_CHUNK
    R = 2 * M_HEADS

    ri = lax.broadcasted_iota(jnp.int32, (L, L), 0)
    ci = lax.broadcasted_iota(jnp.int32, (L, L), 1)
    causal = ci <= ri
    triu = (ri <= ci).astype(BF16)
    lane_in_chunk = lax.broadcasted_iota(jnp.int32, (R, TL), 1) & (L - 1)
    lo_b = lax.broadcasted_iota(jnp.int32, (L, LANES), 1) < M_QK_DIM
    top = lax.broadcasted_iota(jnp.int32, (L, LANES), 0) < M_QK_DIM
    top2 = lax.broadcasted_iota(jnp.int32, (L, 2 * M_V_DIM), 0) < M_QK_DIM
    ones_v = jnp.ones((L, M_V_DIM), BF16)
    zero_b = jnp.zeros((), BF16)
    chunk_rows = [slice(c * L, (c + 1) * L) for c in range(TL // L)]

    def row_bcast(x8, h, n=L):
        return jnp.broadcast_to(x8[h:h + 1, :], (n, x8.shape[1]))

    def gate_scan(g_ref):
        gr = g_ref[...] + brow_ref[...]
        lf = jax.nn.log_sigmoid(gr)
        b8 = jnp.concatenate([sum(_dot(part, triu) for part in _split3(lf[:, rows])) for rows in chunk_rows], axis=1)
        b_all = pltpu.roll(b8, M_HEADS, 0)
        r_all = gr - b_all
        cm_all = r_all
        sh = 1
        while sh < L:
            cm_all = jnp.where(lane_in_chunk >= sh, jnp.maximum(cm_all, pltpu.roll(cm_all, sh, 1)), cm_all)
            sh *= 2
        return b_all, r_all, cm_all

    @pl.when(t == 0)
    def _():
        state[...] = jnp.zeros(state.shape, F32)
        m_scr[...] = jnp.zeros(m_scr.shape, F32)
        for i, x in enumerate(gate_scan(grow_ref)):
            gate_scr[i] = x

    b_all, r_all, cm_all = gate_scr[0], gate_scr[1], gate_scr[2]
    next_gates = gate_scan(gnext_ref)

    m = m_scr[...]
    gates = []
    for rows in chunk_rows:
        b, r, cm = b_all[:, rows], r_all[:, rows], cm_all[:, rows]
        rmax = jnp.broadcast_to(cm[:, L - 1:L], (R, L))
        g = jnp.broadcast_to(b[:, L - 1:L], (R, L))
        big_m = jnp.maximum(m, cm)
        inter = jnp.exp(m - big_m)
        emt = jnp.exp(-(b + big_m))
        w = jnp.exp(r - rmax)
        m_loc = g + rmax
        m_new = jnp.maximum(g + m, m_loc)
        decay = jnp.exp(g + m - m_new)
        scale = jnp.exp(m_loc - m_new)
        m = m_new
        gates.append((r, w, decay, scale, big_m, inter, emt))
    m_scr[...] = m

    for rows, (r, w, decay, scale, big_m, inter, emt) in zip(chunk_rows, gates):
        for p in range(M_HEADS // 2):
            pc = slice(p * LANES, (p + 1) * LANES)
            q_pair_b = qk_ref[rows, pc]
            k_pair_b = qk_ref[rows, M_QK_W + p * LANES:M_QK_W + (p + 1) * LANES]
            c_prev = state[p]
            c_prev_b = c_prev.astype(BF16)
            inter_cols = jnp.where(top, row_bcast(inter, 2 * p), row_bcast(inter, 2 * p + 1)).T
            qs_pair = (q_pair_b.astype(F32) * inter_cols).astype(BF16)
            v_exts = []
            for hh in range(2):
                hd = 2 * p + hh
                cols = slice(hd * M_V_DIM, (hd + 1) * M_V_DIM)
                sel = lo_b if hh == 0 else jnp.logical_not(lo_b)
                v_ext = jnp.concatenate([mv_ref[rows, cols], ones_v], axis=1)
                v_exts.append(v_ext)
                m_col, emt_col = (row_bcast(x8, hd).T for x8 in (big_m, emt))

                d_mat = jnp.exp(jnp.where(causal, row_bcast(r, hd), -jnp.inf) - m_col)
                k_h = jnp.where(sel, k_pair_b, zero_b)
                s_mat = (_dot_nt(q_pair_b, k_h) * d_mat).astype(BF16)
                qs = jnp.where(sel, qs_pair, zero_b)
                nd = _dot(jnp.concatenate([s_mat, qs], axis=1), jnp.concatenate([v_ext, c_prev_b], axis=0))
                hcell = nd[:, :M_V_DIM] / jnp.maximum(jnp.abs(nd[:, M_V_DIM:]), emt_col)
                cell = _rms(hcell) * hn_ref[:, cols]
                out_ref[rows, cols] = (so_ref[rows, cols].astype(F32) * cell).astype(BF16)

            k_t = k_pair_b.astype(F32).T
            w_s = jnp.where(top, row_bcast(w, 2 * p), row_bcast(w, 2 * p + 1))
            kw_t = (k_t * w_s).astype(BF16)
            lhs = jnp.concatenate([jnp.where(top, kw_t, zero_b), jnp.where(top, zero_b, kw_t)], axis=1)
            a = _dot(lhs, jnp.concatenate(v_exts, axis=0))
            dec = jnp.where(top2, row_bcast(decay, 2 * p, L)[:, :1], row_bcast(decay, 2 * p + 1, L)[:, :1])
            sc = jnp.where(top2, row_bcast(scale, 2 * p, L)[:, :1], row_bcast(scale, 2 * p + 1, L)[:, :1])
            state[p] = dec * c_prev + sc * a

    for i, x in enumerate(next_gates):
        gate_scr[i] = x


def _mlstm(qk, mv, so, grow, brow, hn, batch, seq):
    tl = MLSTM_TILE
    nt = seq // tl
    cur = lambda b, t: (b * nt + t, 0)
    const = lambda b, t: (0, 0)
    return pl.pallas_call(
        _mlstm_kernel,
        grid=(batch, nt),
        in_specs=[
            pl.BlockSpec((tl, 2 * M_QK_W), cur),
            pl.BlockSpec((tl, M_V_W), cur),
            pl.BlockSpec((tl, M_V_W), cur),
            pl.BlockSpec((2 * M_HEADS, tl), lambda b, t: (0, b * nt + t)),
            pl.BlockSpec((2 * M_HEADS, tl), lambda b, t: (0, b * nt + jnp.minimum(t + 1, nt - 1))),
            pl.BlockSpec(brow.shape, const),
            pl.BlockSpec(hn.shape, const),
        ],
        out_specs=pl.BlockSpec((tl, M_V_W), cur),
        out_shape=jax.ShapeDtypeStruct(mv.shape, BF16),
        scratch_shapes=[
            pltpu.VMEM((M_HEADS // 2, 2 * M_QK_DIM, 2 * M_V_DIM), F32),
            pltpu.VMEM((2 * M_HEADS, M_CHUNK), F32),
            pltpu.VMEM((3, 2 * M_HEADS, tl), F32),
        ],
        compiler_params=pltpu.CompilerParams(dimension_semantics=("parallel", "arbitrary"), vmem_limit_bytes=VMEM_LIMIT),
        name="mlstm",
    )(qk, mv, so, grow, grow, brow, hn)


def _ff_chunks():
    return [(c0, min(c0 + FF_CHUNK, D_FF)) for c0 in range(0, D_FF, FF_CHUNK)]


def _mix_ffn_kernel(x_ref, h_ref, a_ref, m_ref, gpost_ref, gpre_ref, gffn_ref, wg_ref, wa_ref, wm_ref, wo_ref,
                    wfi_ref, wfo_ref, o_ref, acc_ref):
    x = x_ref[...]
    h = h_ref[...]
    a = a_ref[...]
    m = m_ref[...]
    parts = []
    for c0 in range(0, D_MODEL, MERGE_CHUNK):
        cols = slice(c0, c0 + MERGE_CHUNK)
        gcols = slice(D_MODEL + c0, D_MODEL + c0 + MERGE_CHUNK)
        g_attn = jax.nn.sigmoid(_dot(h, wg_ref[:, cols]))
        g_mlstm = jax.nn.sigmoid(_dot(h, wg_ref[:, gcols]))
        parts.append((g_attn * _dot(a, wa_ref[:, cols]) + g_mlstm * _dot(m, wm_ref[:, cols])).astype(BF16))
    y = _dot(jnp.concatenate(parts, axis=1), wo_ref[...])
    x1 = x + _rms(y) * gpost_ref[...]

    h2 = (_rms(x1) * gpre_ref[...]).astype(BF16)
    for c0, c1 in _ff_chunks():
        gate = _dot(h2, wfi_ref[:, c0:c1])
        up = _dot(h2, wfi_ref[:, D_FF + c0:D_FF + c1])
        act = (gate * jax.nn.sigmoid(gate) * up).astype(BF16)
        part = _dot(act, wfo_ref[c0:c1, :])
        if c0 == 0:
            acc_ref[...] = part
        else:
            acc_ref[...] += part
    o_ref[...] = x1 + _rms(acc_ref[...]) * gffn_ref[...]


def _mix_ffn(x2, h, attn, ml, gpost, gpre, gffn, wg, wa, wm, wo, wfi, wfo):
    T = x2.shape[0]
    tm = TOK_TILE
    row = lambda i: (i, 0)
    const = lambda i: (0, 0)

    def wspec(w):
        return pl.BlockSpec(w.shape, const, pipeline_mode=pl.Buffered(1))

    gain = pl.BlockSpec((1, D_MODEL), const)
    return pl.pallas_call(
        _mix_ffn_kernel,
        grid=(T // tm,),
        in_specs=[
            pl.BlockSpec((tm, D_MODEL), row),
            pl.BlockSpec((tm, D_MODEL), row),
            pl.BlockSpec((tm, ATTN_Q_W), row),
            pl.BlockSpec((tm, M_V_W), row),
            gain, gain, gain,
            wspec(wg), wspec(wa), wspec(wm), wspec(wo), wspec(wfi), wspec(wfo),
        ],
        out_specs=pl.BlockSpec((tm, D_MODEL), row),
        out_shape=jax.ShapeDtypeStruct(x2.shape, F32),
        scratch_shapes=[pltpu.VMEM((tm, D_MODEL), F32)],
        compiler_params=pltpu.CompilerParams(dimension_semantics=("parallel",), vmem_limit_bytes=VMEM_LIMIT),
        name="mix_ffn",
    )(x2, h, attn, ml, gpost, gpre, gffn, wg, wa, wm, wo, wfi, wfo)


def _rope_tables(seq):
    f32 = np.float32
    inv_freq = (f32(ROPE_THETA) ** (-np.arange(0, HEAD_DIM, 2, dtype=f32) / f32(HEAD_DIM))).astype(f32)
    ang = np.arange(seq).astype(f32)[:, None] * inv_freq[None, :]
    emb = np.concatenate([ang, ang], axis=-1)
    cos = np.cos(emb).astype(f32)
    sin = np.sin(emb).astype(f32)
    first_half = np.arange(HEAD_DIM) < HEAD_DIM // 2
    sin_a = np.where(first_half, -sin, f32(0))
    sin_b = np.where(first_half, f32(0), sin)
    rep = LANES // HEAD_DIM
    return tuple(jnp.asarray(np.tile(t, (1, rep))) for t in (cos, sin_a, sin_b))


def _layer(x2, batch, seq, norm_pre_mix, norm_post_mix, norm_pre_ffn, norm_post_ffn, w_in, attn_sinks, conv_w,
           conv_b, b_igate, b_fgate, mlstm_head_norm, w_attn_branch, w_mlstm_branch, w_out, w_ffn_in, w_ffn_out):
    w_in_t = jnp.transpose(w_in)
    w_mix = _w_in_columns(w_in_t, 0, O_GATES)
    wif = jnp.pad(w_in[:, O_GATES:O_BRANCH], ((0, 0), (0, LANES - 2 * M_HEADS))).astype(BF16)
    wg = _w_in_columns(w_in_t, O_BRANCH, 2 * D_MODEL)
    cos, sa, sb = _rope_tables(seq)

    h, q, k, v, qk, mv, so, grow = _inproj(
        x2, norm_pre_mix[None, :], cos, sa, sb, conv_w, conv_b[None, :], w_mix, wif, seq)
    attn = _attention(attn_sinks, q, k, v, batch, seq)
    bias = jnp.concatenate([b_igate, b_fgate])
    ml = _mlstm(qk, mv, so, grow, bias[:, None], mlstm_head_norm[None, :], batch, seq)
    return _mix_ffn(x2, h, attn, ml, norm_post_mix[None, :], norm_pre_ffn[None, :],
                    norm_post_ffn[None, :], wg, w_attn_branch.astype(BF16), w_mlstm_branch.astype(BF16),
                    w_out.astype(BF16), w_ffn_in.astype(BF16), w_ffn_out.astype(BF16))


def kernel(x, norm_pre_mix, norm_post_mix, norm_pre_ffn, norm_post_ffn, w_in, attn_sinks, conv_w, conv_b, b_igate,
           b_fgate, mlstm_head_norm, w_attn_branch, w_mlstm_branch, w_out, w_ffn_in, w_ffn_out):
    B, S, D = x.shape
    assert D == D_MODEL and WINDOW == ATTN_BLOCK == M_CHUNK
    assert S % max(INPROJ_TILE, ATTN_TILE, MLSTM_TILE) == 0 and (B * S) % TOK_TILE == 0
    x2 = x.reshape(B * S, D)
    for l in range(w_in.shape[0]):
        x2 = _layer(x2, B, S, norm_pre_mix[l], norm_post_mix[l], norm_pre_ffn[l], norm_post_ffn[l], w_in[l],
                    attn_sinks[l], conv_w[l], conv_b[l], b_igate[l], b_fgate[l], mlstm_head_norm[l],
                    w_attn_branch[l], w_mlstm_branch[l], w_out[l], w_ffn_in[l], w_ffn_out[l])
    return x2.reshape(B, S, D)
```

```python
import functools

import numpy as np

import jax
import jax.numpy as jnp
from jax import lax
from jax.experimental import pallas as pl
from jax.experimental.pallas import tpu as pltpu

D_MODEL = 1024
HEAD_DIM = 64
N_Q_HEADS = 8
N_KV_HEADS = 2
WINDOW = 128
ATTN_BLOCK = 128
ROPE_THETA = 10000.0
M_HEADS = 4
M_QK_DIM = 64
M_V_DIM = 128
M_CHUNK = 128
CONV_WIDTH = 4
D_FF = 2816
EPS = 1e-6

ATTN_Q_W = N_Q_HEADS * HEAD_DIM
ATTN_KV_W = N_KV_HEADS * HEAD_DIM
M_QK_W = M_HEADS * M_QK_DIM
M_V_W = M_HEADS * M_V_DIM
KV_BD_W = 2 * ATTN_KV_W * 2
O_MQK = ATTN_Q_W + 2 * ATTN_KV_W
O_MV = O_MQK + 2 * M_QK_W
O_GATES = O_MV + 2 * M_V_W
O_BRANCH = O_GATES + 2 * M_HEADS

LANES = 128
SUBLANES = 8
VMEM_LIMIT = 56 * 1024 * 1024
CONV_HALO = SUBLANES

TOK_TILE = 512
INPROJ_TILE = 1024
ATTN_TILE = 4096
MLSTM_TILE = 2048
FF_CHUNK = 256
MERGE_CHUNK = 256
W_COLS_PER_STEP = 768

F32 = jnp.float32
BF16 = jnp.bfloat16


def _dot(a, b):
    return jnp.dot(a, b, preferred_element_type=F32)


def _dot_nt(a, b):
    return lax.dot_general(a, b, (((1,), (1,)), ((), ())), preferred_element_type=F32)


def _rms(x):
    return x * lax.rsqrt(jnp.mean(x * x, axis=-1, keepdims=True) + EPS)


def _split3(x):
    hi = x.astype(BF16)
    r1 = x - hi.astype(F32)
    mid = r1.astype(BF16)
    lo = (r1 - mid.astype(F32)).astype(BF16)
    return hi, mid, lo


def _transpose_cast_kernel(wt_ref, o_ref):
    o_ref[...] = wt_ref[...].T.astype(BF16)


def _w_in_columns(w_in_t, col0, n_cols):
    k = w_in_t.shape[1]
    step = max(c for c in range(LANES, W_COLS_PER_STEP + 1, LANES) if n_cols % c == 0)
    return pl.pallas_call(
        _transpose_cast_kernel,
        grid=(n_cols // step,),
        in_specs=[pl.BlockSpec((pl.Element(step), pl.Element(k)),
                               lambda j: (pl.multiple_of(col0 + j * step, SUBLANES), 0))],
        out_specs=pl.BlockSpec((k, step), lambda j: (0, j)),
        out_shape=jax.ShapeDtypeStruct((k, n_cols), BF16),
        compiler_params=pltpu.CompilerParams(dimension_semantics=("parallel",), vmem_limit_bytes=VMEM_LIMIT),
        name="w_in_columns",
    )(w_in_t)


def _inproj_kernel(x_ref, g_ref, cos_ref, sa_ref, sb_ref, cw_ref, cb_ref, w_ref, wif_ref,
                   h_ref, q_ref, k_ref, v_ref, qk_ref, mv_ref, so_ref, grow_ref, cbuf, *, n_seq_tiles):
    tm = x_ref.shape[0]
    h = (_rms(x_ref[...]) * g_ref[...]).astype(BF16)
    h_ref[...] = h
    cos = cos_ref[...]
    sa = sa_ref[...]
    sb = sb_ref[...]
    lane = lax.broadcasted_iota(jnp.int32, cos.shape, 1)
    lo = lane < HEAD_DIM

    def rope(t):
        return t * cos + pltpu.roll(t, LANES - HEAD_DIM // 2, 1) * sa + pltpu.roll(t, HEAD_DIM // 2, 1) * sb

    def store_block_diag(ref, t):
        tb = t.astype(BF16)
        rb = pltpu.roll(t, HEAD_DIM, 1).astype(BF16)
        zero = jnp.zeros((), BF16)
        for i, (src, keep_lo) in enumerate(((tb, True), (rb, False), (rb, True), (tb, False))):
            ref[:, i * LANES:(i + 1) * LANES] = jnp.where(lo, src, zero) if keep_lo else jnp.where(lo, zero, src)

    def attn_proj():
        a = _dot(h, w_ref[:, :O_MQK])
        for j in range(ATTN_Q_W // LANES):
            q_ref[:, j * LANES:(j + 1) * LANES] = (
                rope(a[:, j * LANES:(j + 1) * LANES]) * (HEAD_DIM ** -0.5)).astype(BF16)
        store_block_diag(k_ref, rope(a[:, ATTN_Q_W:ATTN_Q_W + LANES]))
        store_block_diag(v_ref, a[:, ATTN_Q_W + LANES:ATTN_Q_W + 2 * LANES])
        grow_ref[...] = _dot(h, wif_ref[...]).T[:2 * M_HEADS, :]

    @pl.when(pl.program_id(0) % n_seq_tiles == 0)
    def _():
        cbuf[0:CONV_HALO, :] = jnp.zeros((CONV_HALO, cbuf.shape[1]), F32)

    def mqk_proj(c0, c1):
        cbuf[CONV_HALO:CONV_HALO + tm, c0:c1] = _dot(h, w_ref[:, O_MQK + c0:O_MQK + c1])

    def conv_chunk(j0):
        cols = slice(j0 * LANES, (j0 + 1) * LANES)
        y = cb_ref[:, cols]
        for j in range(CONV_WIDTH):
            off = CONV_HALO - (CONV_WIDTH - 1) + j
            y = y + cw_ref[j:j + 1, cols] * cbuf[off:off + tm, cols]
        cbuf[0:CONV_HALO, cols] = cbuf[tm:tm + CONV_HALO, cols]
        y = y * jax.nn.sigmoid(y)
        if j0 * LANES >= M_QK_W:
            y = y * (M_QK_DIM ** -0.5)
        qk_ref[:, cols] = y.astype(BF16)

    def mlstm_vo_proj():
        vo = _dot(h, w_ref[:, O_MV:O_GATES])
        mv_ref[...] = vo[:, :M_V_W].astype(BF16)
        so_ref[...] = jax.nn.sigmoid(vo[:, M_V_W:]).astype(BF16)

    mqk_proj(0, M_QK_W)
    conv_chunk(0)
    conv_chunk(1)
    attn_proj()
    mqk_proj(M_QK_W, 2 * M_QK_W)
    conv_chunk(2)
    conv_chunk(3)
    mlstm_vo_proj()


def _inproj(x2, g, cos, sa, sb, cw, cb, w, wif, seq):
    T = x2.shape[0]
    tm = INPROJ_TILE
    n_seq_tiles = seq // tm
    row = lambda i: (i, 0)
    pos = lambda i: (i % n_seq_tiles, 0)
    const = lambda i: (0, 0)

    def wspec(w):
        return pl.BlockSpec(w.shape, const, pipeline_mode=pl.Buffered(1))

    out_shape = (
        jax.ShapeDtypeStruct((T, D_MODEL), BF16),
        jax.ShapeDtypeStruct((T, ATTN_Q_W), BF16),
        jax.ShapeDtypeStruct((T, KV_BD_W), BF16),
        jax.ShapeDtypeStruct((T, KV_BD_W), BF16),
        jax.ShapeDtypeStruct((T, 2 * M_QK_W), BF16),
        jax.ShapeDtypeStruct((T, M_V_W), BF16),
        jax.ShapeDtypeStruct((T, M_V_W), BF16),
        jax.ShapeDtypeStruct((2 * M_HEADS, T), F32),
    )
    out_specs = (
        pl.BlockSpec((tm, D_MODEL), row),
        pl.BlockSpec((tm, ATTN_Q_W), row),
        pl.BlockSpec((tm, KV_BD_W), row),
        pl.BlockSpec((tm, KV_BD_W), row),
        pl.BlockSpec((tm, 2 * M_QK_W), row),
        pl.BlockSpec((tm, M_V_W), row),
        pl.BlockSpec((tm, M_V_W), row),
        pl.BlockSpec((2 * M_HEADS, tm), lambda i: (0, i)),
    )
    in_specs = [
        pl.BlockSpec((tm, D_MODEL), row),
        pl.BlockSpec((1, D_MODEL), const),
        pl.BlockSpec((tm, LANES), pos),
        pl.BlockSpec((tm, LANES), pos),
        pl.BlockSpec((tm, LANES), pos),
        pl.BlockSpec(cw.shape, const),
        pl.BlockSpec(cb.shape, const),
        wspec(w), wspec(wif),
    ]
    return pl.pallas_call(
        functools.partial(_inproj_kernel, n_seq_tiles=n_seq_tiles),
        grid=(T // tm,),
        in_specs=in_specs,
        out_specs=out_specs,
        out_shape=out_shape,
        scratch_shapes=[pltpu.VMEM((CONV_HALO + tm, 2 * M_QK_W), F32)],
        compiler_params=pltpu.CompilerParams(dimension_semantics=("arbitrary",), vmem_limit_bytes=VMEM_LIMIT),
        name="inproj",
    )(x2, g, cos, sa, sb, cw, cb, w, wif)


def _attn_kernel(sink_ref, q_ref, kc_ref, kp_ref, vc_ref, vp_ref, o_ref):
    t = pl.program_id(1)
    blk = ATTN_BLOCK
    n_blk = q_ref.shape[0] // blk
    lo_q = lax.broadcasted_iota(jnp.int32, (blk, LANES), 1) < HEAD_DIM
    ri = lax.broadcasted_iota(jnp.int32, (blk, blk), 0)
    ci = lax.broadcasted_iota(jnp.int32, (blk, blk), 1)
    own = ci <= ri
    prev_ok = ci > ri + jnp.where(t > 0, 0, blk)
    zero = jnp.zeros((), BF16)
    row_is_a = lax.broadcasted_iota(jnp.int32, (4 * blk, LANES), 0) < 2 * blk
    lane_is_a = lax.broadcasted_iota(jnp.int32, (4 * blk, LANES), 1) < HEAD_DIM
    ones_bd = (row_is_a == lane_is_a).astype(BF16)

    for n in range(n_blk):
        rows = slice(n * blk, (n + 1) * blk)
        k_prev = kp_ref if n == 0 else kc_ref.at[(n - 1) * blk:n * blk, :]
        v_prev = vp_ref if n == 0 else vc_ref.at[(n - 1) * blk:n * blk, :]
        for j in range(N_KV_HEADS):
            kv_cols = [slice((2 * j + i) * LANES, (2 * j + i + 1) * LANES) for i in range(2)]
            k_bd = jnp.concatenate([x for c in kv_cols for x in (k_prev[:, c], kc_ref[rows, c])], axis=0)
            v_bd = jnp.concatenate([x for c in kv_cols for x in (v_prev[:, c], vc_ref[rows, c])], axis=0)
            c0 = 2 * j * LANES
            q2 = jnp.concatenate([q_ref[rows, c0:c0 + LANES], q_ref[rows, c0 + LANES:c0 + 2 * LANES]], axis=0)
            s = _dot_nt(q2, k_bd)
            e_rows, sink_rows = [], []
            for p in range(2):
                e_cols, sink_cols = [], []
                for hh in range(2):
                    sink = sink_ref[4 * j + 2 * p + hh]
                    s_prev = s[p * blk:(p + 1) * blk, 2 * hh * blk:(2 * hh + 1) * blk]
                    s_own = s[p * blk:(p + 1) * blk, (2 * hh + 1) * blk:(2 * hh + 2) * blk]
                    if n == 0:
                        s_prev = jnp.where(prev_ok, s_prev, -jnp.inf)
                    sc = jnp.where(own, s_own, s_prev)
                    m = jnp.maximum(jnp.max(sc, axis=-1, keepdims=True), sink)
                    eb = jnp.exp(sc - m).astype(BF16)
                    e_cols += [jnp.where(own, zero, eb), jnp.where(own, eb, zero)]
                    sink_cols.append(jnp.exp(sink - m))
                e_rows.append(jnp.concatenate(e_cols, axis=1))
                sink_rows.append(jnp.where(lo_q, sink_cols[0], sink_cols[1]))
            o = _dot(jnp.concatenate(e_rows, axis=0), jnp.concatenate([v_bd, ones_bd], axis=1))
            for p in range(2):
                op = o[p * blk:(p + 1) * blk]
                o_ref[rows, c0 + p * LANES:c0 + (p + 1) * LANES] = (
                    op[:, :LANES] / (op[:, LANES:] + sink_rows[p])).astype(BF16)


def _attention(sinks, q, k, v, batch, seq):
    tq = ATTN_TILE
    nt = seq // tq
    per = tq // ATTN_BLOCK
    cur = lambda b, t: (b * nt + t, 0)
    prev = lambda b, t: (jnp.maximum((b * nt + t) * per - 1, 0), 0)
    return pl.pallas_call(
        _attn_kernel,
        grid=(batch, nt),
        in_specs=[
            pl.BlockSpec(memory_space=pltpu.SMEM),
            pl.BlockSpec((tq, ATTN_Q_W), cur),
            pl.BlockSpec((tq, KV_BD_W), cur),
            pl.BlockSpec((ATTN_BLOCK, KV_BD_W), prev),
            pl.BlockSpec((tq, KV_BD_W), cur),
            pl.BlockSpec((ATTN_BLOCK, KV_BD_W), prev),
        ],
        out_specs=pl.BlockSpec((tq, ATTN_Q_W), cur),
        out_shape=jax.ShapeDtypeStruct(q.shape, BF16),
        compiler_params=pltpu.CompilerParams(dimension_semantics=("parallel", "parallel"), vmem_limit_bytes=VMEM_LIMIT),
        name="swa_attention",
    )(sinks, q, k, k, v, v)


def _mlstm_kernel(qk_ref, mv_ref, so_ref, grow_ref, gnext_ref, brow_ref, hn_ref, out_ref, state, m_scr, gate_scr):
    t = pl.program_id(1)
    TL = qk_ref.shape[0]
    L = M_CHUNK
    R = 2 * M_HEADS

    ri = lax.broadcasted_iota(jnp.int32, (L, L), 0)
    ci = lax.broadcasted_iota(jnp.int32, (L, L), 1)
    causal = ci <= ri
    triu = (ri <= ci).astype(BF16)
    lane_in_chunk = lax.broadcasted_iota(jnp.int32, (R, TL), 1) & (L - 1)
    lo_b = lax.broadcasted_iota(jnp.int32, (L, LANES), 1) < M_QK_DIM
    top = lax.broadcasted_iota(jnp.int32, (L, LANES), 0) < M_QK_DIM
    top2 = lax.broadcasted_iota(jnp.int32, (L, 2 * M_V_DIM), 0) < M_QK_DIM
    ones_v = jnp.ones((L, M_V_DIM), BF16)
    zero_b = jnp.zeros((), BF16)
    chunk_rows = [slice(c * L, (c + 1) * L) for c in range(TL // L)]

    def row_bcast(x8, h, n=L):
        return jnp.broadcast_to(x8[h:h + 1, :], (n, x8.shape[1]))

    def gate_scan(g_ref):
        gr = g_ref[...] + brow_ref[...]
        lf = jax.nn.log_sigmoid(gr)
        b8 = jnp.concatenate([sum(_dot(part, triu) for part in _split3(lf[:, rows])) for rows in chunk_rows], axis=1)
        b_all = pltpu.roll(b8, M_HEADS, 0)
        r_all = gr - b_all
        cm_all = r_all
        sh = 1
        while sh < L:
            cm_all = jnp.where(lane_in_chunk >= sh, jnp.maximum(cm_all, pltpu.roll(cm_all, sh, 1)), cm_all)
            sh *= 2
        return b_all, r_all, cm_all

    @pl.when(t == 0)
    def _():
        state[...] = jnp.zeros(state.shape, F32)
        m_scr[...] = jnp.zeros(m_scr.shape, F32)
        for i, x in enumerate(gate_scan(grow_ref)):
            gate_scr[i] = x

    b_all, r_all, cm_all = gate_scr[0], gate_scr[1], gate_scr[2]
    next_gates = gate_scan(gnext_ref)

    m = m_scr[...]
    gates = []
    for rows in chunk_rows:
        b, r, cm = b_all[:, rows], r_all[:, rows], cm_all[:, rows]
        rmax = jnp.broadcast_to(cm[:, L - 1:L], (R, L))
        g = jnp.broadcast_to(b[:, L - 1:L], (R, L))
        big_m = jnp.maximum(m, cm)
        inter = jnp.exp(m - big_m)
        emt = jnp.exp(-(b + big_m))
        w = jnp.exp(r - rmax)
        m_loc = g + rmax
        m_new = jnp.maximum(g + m, m_loc)
        decay = jnp.exp(g + m - m_new)
        scale = jnp.exp(m_loc - m_new)
        m = m_new
        gates.append((r, w, decay, scale, big_m, inter, emt))
    m_scr[...] = m

    for rows, (r, w, decay, scale, big_m, inter, emt) in zip(chunk_rows, gates):
        for p in range(M_HEADS // 2):
            pc = slice(p * LANES, (p + 1) * LANES)
            q_pair_b = qk_ref[rows, pc]
            k_pair_b = qk_ref[rows, M_QK_W + p * LANES:M_QK_W + (p + 1) * LANES]
            c_prev = state[p]
            c_prev_b = c_prev.astype(BF16)
            inter_cols = jnp.where(top, row_bcast(inter, 2 * p), row_bcast(inter, 2 * p + 1)).T
            qs_pair = (q_pair_b.astype(F32) * inter_cols).astype(BF16)
            v_exts = []
            for hh in range(2):
                hd = 2 * p + hh
                cols = slice(hd * M_V_DIM, (hd + 1) * M_V_DIM)
                sel = lo_b if hh == 0 else jnp.logical_not(lo_b)
                v_ext = jnp.concatenate([mv_ref[rows, cols], ones_v], axis=1)
                v_exts.append(v_ext)
                m_col, emt_col = (row_bcast(x8, hd).T for x8 in (big_m, emt))

                d_mat = jnp.exp(jnp.where(causal, row_bcast(r, hd), -jnp.inf) - m_col)
                k_h = jnp.where(sel, k_pair_b, zero_b)
                s_mat = (_dot_nt(q_pair_b, k_h) * d_mat).astype(BF16)
                qs = jnp.where(sel, qs_pair, zero_b)
                nd = _dot(jnp.concatenate([s_mat, qs], axis=1), jnp.concatenate([v_ext, c_prev_b], axis=0))
                hcell = nd[:, :M_V_DIM] / jnp.maximum(jnp.abs(nd[:, M_V_DIM:]), emt_col)
                cell = _rms(hcell) * hn_ref[:, cols]
                out_ref[rows, cols] = (so_ref[rows, cols].astype(F32) * cell).astype(BF16)

            k_t = k_pair_b.astype(F32).T
            w_s = jnp.where(top, row_bcast(w, 2 * p), row_bcast(w, 2 * p + 1))
            kw_t = (k_t * w_s).astype(BF16)
            lhs = jnp.concatenate([jnp.where(top, kw_t, zero_b), jnp.where(top, zero_b, kw_t)], axis=1)
            a = _dot(lhs, jnp.concatenate(v_exts, axis=0))
            dec = jnp.where(top2, row_bcast(decay, 2 * p, L)[:, :1], row_bcast(decay, 2 * p + 1, L)[:, :1])
            sc = jnp.where(top2, row_bcast(scale, 2 * p, L)[:, :1], row_bcast(scale, 2 * p + 1, L)[:, :1])
            state[p] = dec * c_prev + sc * a

    for i, x in enumerate(next_gates):
        gate_scr[i] = x


def _mlstm(qk, mv, so, grow, brow, hn, batch, seq):
    tl = MLSTM_TILE
    nt = seq // tl
    cur = lambda b, t: (b * nt + t, 0)
    const = lambda b, t: (0, 0)
    return pl.pallas_call(
        _mlstm_kernel,
        grid=(batch, nt),
        in_specs=[
            pl.BlockSpec((tl, 2 * M_QK_W), cur),
            pl.BlockSpec((tl, M_V_W), cur),
            pl.BlockSpec((tl, M_V_W), cur),
            pl.BlockSpec((2 * M_HEADS, tl), lambda b, t: (0, b * nt + t)),
            pl.BlockSpec((2 * M_HEADS, tl), lambda b, t: (0, b * nt + jnp.minimum(t + 1, nt - 1))),
            pl.BlockSpec(brow.shape, const),
            pl.BlockSpec(hn.shape, const),
        ],
        out_specs=pl.BlockSpec((tl, M_V_W), cur),
        out_shape=jax.ShapeDtypeStruct(mv.shape, BF16),
        scratch_shapes=[
            pltpu.VMEM((M_HEADS // 2, 2 * M_QK_DIM, 2 * M_V_DIM), F32),
            pltpu.VMEM((2 * M_HEADS, M_CHUNK), F32),
            pltpu.VMEM((3, 2 * M_HEADS, tl), F32),
        ],
        compiler_params=pltpu.CompilerParams(dimension_semantics=("parallel", "arbitrary"), vmem_limit_bytes=VMEM_LIMIT),
        name="mlstm",
    )(qk, mv, so, grow, grow, brow, hn)


def _ff_chunks():
    return [(c0, min(c0 + FF_CHUNK, D_FF)) for c0 in range(0, D_FF, FF_CHUNK)]


def _mix_ffn_kernel(x_ref, h_ref, a_ref, m_ref, gpost_ref, gpre_ref, gffn_ref, wg_ref, wa_ref, wm_ref, wo_ref,
                    wfi_ref, wfo_ref, o_ref, acc_ref):
    x = x_ref[...]
    h = h_ref[...]
    a = a_ref[...]
    m = m_ref[...]
    parts = []
    for c0 in range(0, D_MODEL, MERGE_CHUNK):
        cols = slice(c0, c0 + MERGE_CHUNK)
        gcols = slice(D_MODEL + c0, D_MODEL + c0 + MERGE_CHUNK)
        g_attn = jax.nn.sigmoid(_dot(h, wg_ref[:, cols]))
        g_mlstm = jax.nn.sigmoid(_dot(h, wg_ref[:, gcols]))
        parts.append((g_attn * _dot(a, wa_ref[:, cols]) + g_mlstm * _dot(m, wm_ref[:, cols])).astype(BF16))
    y = _dot(jnp.concatenate(parts, axis=1), wo_ref[...])
    x1 = x + _rms(y) * gpost_ref[...]

    h2 = (_rms(x1) * gpre_ref[...]).astype(BF16)
    for c0, c1 in _ff_chunks():
        gate = _dot(h2, wfi_ref[:, c0:c1])
        up = _dot(h2, wfi_ref[:, D_FF + c0:D_FF + c1])
        act = (gate * jax.nn.sigmoid(gate) * up).astype(BF16)
        part = _dot(act, wfo_ref[c0:c1, :])
        if c0 == 0:
            acc_ref[...] = part
        else:
            acc_ref[...] += part
    o_ref[...] = x1 + _rms(acc_ref[...]) * gffn_ref[...]


def _mix_ffn(x2, h, attn, ml, gpost, gpre, gffn, wg, wa, wm, wo, wfi, wfo):
    T = x2.shape[0]
    tm = TOK_TILE
    row = lambda i: (i, 0)
    const = lambda i: (0, 0)

    def wspec(w):
        return pl.BlockSpec(w.shape, const, pipeline_mode=pl.Buffered(1))

    gain = pl.BlockSpec((1, D_MODEL), const)
    return pl.pallas_call(
        _mix_ffn_kernel,
        grid=(T // tm,),
        in_specs=[
            pl.BlockSpec((tm, D_MODEL), row),
            pl.BlockSpec((tm, D_MODEL), row),
            pl.BlockSpec((tm, ATTN_Q_W), row),
            pl.BlockSpec((tm, M_V_W), row),
            gain, gain, gain,
            wspec(wg), wspec(wa), wspec(wm), wspec(wo), wspec(wfi), wspec(wfo),
        ],
        out_specs=pl.BlockSpec((tm, D_MODEL), row),
        out_shape=jax.ShapeDtypeStruct(x2.shape, F32),
        scratch_shapes=[pltpu.VMEM((tm, D_MODEL), F32)],
        compiler_params=pltpu.CompilerParams(dimension_semantics=("parallel",), vmem_limit_bytes=VMEM_LIMIT),
        name="mix_ffn",
    )(x2, h, attn, ml, gpost, gpre, gffn, wg, wa, wm, wo, wfi, wfo)


def _rope_tables(seq):
    f32 = np.float32
    inv_freq = (f32(ROPE_THETA) ** (-np.arange(0, HEAD_DIM, 2, dtype=f32) / f32(HEAD_DIM))).astype(f32)
    ang = np.arange(seq).astype(f32)[:, None] * inv_freq[None, :]
    emb = np.concatenate([ang, ang], axis=-1)
    cos = np.cos(emb).astype(f32)
    sin = np.sin(emb).astype(f32)
    first_half = np.arange(HEAD_DIM) < HEAD_DIM // 2
    sin_a = np.where(first_half, -sin, f32(0))
    sin_b = np.where(first_half, f32(0), sin)
    rep = LANES // HEAD_DIM
    return tuple(jnp.asarray(np.tile(t, (1, rep))) for t in (cos, sin_a, sin_b))


def _layer(x2, batch, seq, norm_pre_mix, norm_post_mix, norm_pre_ffn, norm_post_ffn, w_in, attn_sinks, conv_w,
           conv_b, b_igate, b_fgate, mlstm_head_norm, w_attn_branch, w_mlstm_branch, w_out, w_ffn_in, w_ffn_out):
    w_in_t = jnp.transpose(w_in)
    w_mix = _w_in_columns(w_in_t, 0, O_GATES)
    wif = jnp.pad(w_in[:, O_GATES:O_BRANCH], ((0, 0), (0, LANES - 2 * M_HEADS))).astype(BF16)
    wg = _w_in_columns(w_in_t, O_BRANCH, 2 * D_MODEL)
    cos, sa, sb = _rope_tables(seq)

    h, q, k, v, qk, mv, so, grow = _inproj(
        x2, norm_pre_mix[None, :], cos, sa, sb, conv_w, conv_b[None, :], w_mix, wif, seq)
    attn = _attention(attn_sinks, q, k, v, batch, seq)
    bias = jnp.concatenate([b_igate, b_fgate])
    ml = _mlstm(qk, mv, so, grow, bias[:, None], mlstm_head_norm[None, :], batch, seq)
    return _mix_ffn(x2, h, attn, ml, norm_post_mix[None, :], norm_pre_ffn[None, :],
                    norm_post_ffn[None, :], wg, w_attn_branch.astype(BF16), w_mlstm_branch.astype(BF16),
                    w_out.astype(BF16), w_ffn_in.astype(BF16), w_ffn_out.astype(BF16))


def kernel(x, norm_pre_mix, norm_post_mix, norm_pre_ffn, norm_post_ffn, w_in, attn_sinks, conv_w, conv_b, b_igate,
           b_fgate, mlstm_head_norm, w_attn_branch, w_mlstm_branch, w_out, w_ffn_in, w_ffn_out):
    B, S, D = x.shape
    assert D == D_MODEL and WINDOW == ATTN_BLOCK == M_CHUNK
    assert S % max(INPROJ_TILE, ATTN_TILE, MLSTM_TILE) == 0 and (B * S) % TOK_TILE == 0
    x2 = x.reshape(B * S, D)
    for l in range(w_in.shape[0]):
        x2 = _layer(x2, B, S, norm_pre_mix[l], norm_post_mix[l], norm_pre_ffn[l], norm_post_ffn[l], w_in[l],
                    attn_sinks[l], conv_w[l], conv_b[l], b_igate[l], b_fgate[l], mlstm_head_norm[l],
                    w_attn_branch[l], w_mlstm_branch[l], w_out[l], w_ffn_in[l], w_ffn_out[l])
    return x2.reshape(B, S, D)
```

```python
import functools

import numpy as np

import jax
import jax.numpy as jnp
from jax import lax
from jax.experimental import pallas as pl
from jax.experimental.pallas import tpu as pltpu

D_MODEL = 1024
HEAD_DIM = 64
N_Q_HEADS = 8
N_KV_HEADS = 2
WINDOW = 128
ATTN_BLOCK = 128
ROPE_THETA = 10000.0
M_HEADS = 4
M_QK_DIM = 64
M_V_DIM = 128
M_CHUNK = 128
CONV_WIDTH = 4
D_FF = 2816
EPS = 1e-6

ATTN_Q_W = N_Q_HEADS * HEAD_DIM
ATTN_KV_W = N_KV_HEADS * HEAD_DIM
M_QK_W = M_HEADS * M_QK_DIM
M_V_W = M_HEADS * M_V_DIM
KV_BD_W = 2 * ATTN_KV_W * 2
O_MQK = ATTN_Q_W + 2 * ATTN_KV_W
O_MV = O_MQK + 2 * M_QK_W
O_GATES = O_MV + 2 * M_V_W
O_BRANCH = O_GATES + 2 * M_HEADS

LANES = 128
SUBLANES = 8
VMEM_LIMIT = 56 * 1024 * 1024
CONV_HALO = SUBLANES

TOK_TILE = 512
INPROJ_TILE = 1024
ATTN_TILE = 2048
MLSTM_TILE = 2048
FF_CHUNK = 256
MERGE_CHUNK = 256
W_COLS_PER_STEP = 1152

F32 = jnp.float32
BF16 = jnp.bfloat16


def _dot(a, b):
    return jnp.dot(a, b, preferred_element_type=F32)


def _dot_nt(a, b):
    return lax.dot_general(a, b, (((1,), (1,)), ((), ())), preferred_element_type=F32)


def _rms(x):
    return x * lax.rsqrt(jnp.mean(x * x, axis=-1, keepdims=True) + EPS)


def _split3(x):
    hi = x.astype(BF16)
    r1 = x - hi.astype(F32)
    mid = r1.astype(BF16)
    lo = (r1 - mid.astype(F32)).astype(BF16)
    return hi, mid, lo


def _transpose_cast_kernel(wt_ref, o_ref):
    o_ref[...] = wt_ref[...].T.astype(BF16)


def _w_in_columns(w_in_t, col0, n_cols):
    k = w_in_t.shape[1]
    step = max(c for c in range(LANES, W_COLS_PER_STEP + 1, LANES) if n_cols % c == 0)
    return pl.pallas_call(
        _transpose_cast_kernel,
        grid=(n_cols // step,),
        in_specs=[pl.BlockSpec((pl.Element(step), pl.Element(k)),
                               lambda j: (pl.multiple_of(col0 + j * step, SUBLANES), 0))],
        out_specs=pl.BlockSpec((k, step), lambda j: (0, j)),
        out_shape=jax.ShapeDtypeStruct((k, n_cols), BF16),
        compiler_params=pltpu.CompilerParams(dimension_semantics=("parallel",), vmem_limit_bytes=VMEM_LIMIT),
        name="w_in_columns",
    )(w_in_t)


def _inproj_kernel(x_ref, g_ref, cos_ref, sa_ref, sb_ref, cw_ref, cb_ref, w_ref, wif_ref,
                   h_ref, q_ref, k_ref, v_ref, qk_ref, mv_ref, so_ref, grow_ref, cbuf, *, n_seq_tiles):
    tm = x_ref.shape[0]
    h = (_rms(x_ref[...]) * g_ref[...]).astype(BF16)
    h_ref[...] = h
    cos = cos_ref[...]
    sa = sa_ref[...]
    sb = sb_ref[...]
    lane = lax.broadcasted_iota(jnp.int32, cos.shape, 1)
    lo = lane < HEAD_DIM

    def rope(t):
        return t * cos + pltpu.roll(t, LANES - HEAD_DIM // 2, 1) * sa + pltpu.roll(t, HEAD_DIM // 2, 1) * sb

    def store_block_diag(ref, t):
        tb = t.astype(BF16)
        rb = pltpu.roll(t, HEAD_DIM, 1).astype(BF16)
        zero = jnp.zeros((), BF16)
        for i, (src, keep_lo) in enumerate(((tb, True), (rb, False), (rb, True), (tb, False))):
            ref[:, i * LANES:(i + 1) * LANES] = jnp.where(lo, src, zero) if keep_lo else jnp.where(lo, zero, src)

    def attn_proj():
        a = _dot(h, w_ref[:, :O_MQK])
        for j in range(ATTN_Q_W // LANES):
            q_ref[:, j * LANES:(j + 1) * LANES] = (
                rope(a[:, j * LANES:(j + 1) * LANES]) * (HEAD_DIM ** -0.5)).astype(BF16)
        store_block_diag(k_ref, rope(a[:, ATTN_Q_W:ATTN_Q_W + LANES]))
        store_block_diag(v_ref, a[:, ATTN_Q_W + LANES:ATTN_Q_W + 2 * LANES])
        grow_ref[...] = _dot(h, wif_ref[...]).T[:2 * M_HEADS, :]

    @pl.when(pl.program_id(0) % n_seq_tiles == 0)
    def _():
        cbuf[0:CONV_HALO, :] = jnp.zeros((CONV_HALO, cbuf.shape[1]), F32)

    def mqk_proj(c0, c1):
        cbuf[CONV_HALO:CONV_HALO + tm, c0:c1] = _dot(h, w_ref[:, O_MQK + c0:O_MQK + c1])

    def conv_chunk(j0):
        cols = slice(j0 * LANES, (j0 + 1) * LANES)
        y = cb_ref[:, cols]
        for j in range(CONV_WIDTH):
            off = CONV_HALO - (CONV_WIDTH - 1) + j
            y = y + cw_ref[j:j + 1, cols] * cbuf[off:off + tm, cols]
        cbuf[0:CONV_HALO, cols] = cbuf[tm:tm + CONV_HALO, cols]
        y = y * jax.nn.sigmoid(y)
        if j0 * LANES >= M_QK_W:
            y = y * (M_QK_DIM ** -0.5)
        qk_ref[:, cols] = y.astype(BF16)

    def mlstm_vo_proj():
        vo = _dot(h, w_ref[:, O_MV:O_GATES])
        mv_ref[...] = vo[:, :M_V_W].astype(BF16)
        so_ref[...] = jax.nn.sigmoid(vo[:, M_V_W:]).astype(BF16)

    mqk_proj(0, M_QK_W)
    conv_chunk(0)
    conv_chunk(1)
    attn_proj()
    mqk_proj(M_QK_W, 2 * M_QK_W)
    conv_chunk(2)
    conv_chunk(3)
    mlstm_vo_proj()


def _inproj(x2, g, cos, sa, sb, cw, cb, w, wif, seq):
    T = x2.shape[0]
    tm = INPROJ_TILE
    n_seq_tiles = seq // tm
    row = lambda i: (i, 0)
    pos = lambda i: (i % n_seq_tiles, 0)
    const = lambda i: (0, 0)

    def wspec(w):
        return pl.BlockSpec(w.shape, const, pipeline_mode=pl.Buffered(1))

    out_shape = (
        jax.ShapeDtypeStruct((T, D_MODEL), BF16),
        jax.ShapeDtypeStruct((T, ATTN_Q_W), BF16),
        jax.ShapeDtypeStruct((T, KV_BD_W), BF16),
        jax.ShapeDtypeStruct((T, KV_BD_W), BF16),
        jax.ShapeDtypeStruct((T, 2 * M_QK_W), BF16),
        jax.ShapeDtypeStruct((T, M_V_W), BF16),
        jax.ShapeDtypeStruct((T, M_V_W), BF16),
        jax.ShapeDtypeStruct((2 * M_HEADS, T), F32),
    )
    out_specs = (
        pl.BlockSpec((tm, D_MODEL), row),
        pl.BlockSpec((tm, ATTN_Q_W), row),
        pl.BlockSpec((tm, KV_BD_W), row),
        pl.BlockSpec((tm, KV_BD_W), row),
        pl.BlockSpec((tm, 2 * M_QK_W), row),
        pl.BlockSpec((tm, M_V_W), row),
        pl.BlockSpec((tm, M_V_W), row),
        pl.BlockSpec((2 * M_HEADS, tm), lambda i: (0, i)),
    )
    in_specs = [
        pl.BlockSpec((tm, D_MODEL), row),
        pl.BlockSpec((1, D_MODEL), const),
        pl.BlockSpec((tm, LANES), pos),
        pl.BlockSpec((tm, LANES), pos),
        pl.BlockSpec((tm, LANES), pos),
        pl.BlockSpec(cw.shape, const),
        pl.BlockSpec(cb.shape, const),
        wspec(w), wspec(wif),
    ]
    return pl.pallas_call(
        functools.partial(_inproj_kernel, n_seq_tiles=n_seq_tiles),
        grid=(T // tm,),
        in_specs=in_specs,
        out_specs=out_specs,
        out_shape=out_shape,
        scratch_shapes=[pltpu.VMEM((CONV_HALO + tm, 2 * M_QK_W), F32)],
        compiler_params=pltpu.CompilerParams(dimension_semantics=("arbitrary",), vmem_limit_bytes=VMEM_LIMIT),
        name="inproj",
    )(x2, g, cos, sa, sb, cw, cb, w, wif)


def _attn_kernel(sink_ref, q_ref, kc_ref, kp_ref, vc_ref, vp_ref, o_ref):
    t = pl.program_id(1)
    blk = ATTN_BLOCK
    n_blk = q_ref.shape[0] // blk
    lo_q = lax.broadcasted_iota(jnp.int32, (blk, LANES), 1) < HEAD_DIM
    ri = lax.broadcasted_iota(jnp.int32, (blk, blk), 0)
    ci = lax.broadcasted_iota(jnp.int32, (blk, blk), 1)
    own = ci <= ri
    prev_ok = ci > ri + jnp.where(t > 0, 0, blk)
    zero = jnp.zeros((), BF16)
    row_is_a = lax.broadcasted_iota(jnp.int32, (4 * blk, LANES), 0) < 2 * blk
    lane_is_a = lax.broadcasted_iota(jnp.int32, (4 * blk, LANES), 1) < HEAD_DIM
    ones_bd = (row_is_a == lane_is_a).astype(BF16)

    for n in range(n_blk):
        rows = slice(n * blk, (n + 1) * blk)
        k_prev = kp_ref if n == 0 else kc_ref.at[(n - 1) * blk:n * blk, :]
        v_prev = vp_ref if n == 0 else vc_ref.at[(n - 1) * blk:n * blk, :]
        for j in range(N_KV_HEADS):
            kv_cols = [slice((2 * j + i) * LANES, (2 * j + i + 1) * LANES) for i in range(2)]
            k_bd = jnp.concatenate([x for c in kv_cols for x in (k_prev[:, c], kc_ref[rows, c])], axis=0)
            v_bd = jnp.concatenate([x for c in kv_cols for x in (v_prev[:, c], vc_ref[rows, c])], axis=0)
            c0 = 2 * j * LANES
            q2 = jnp.concatenate([q_ref[rows, c0:c0 + LANES], q_ref[rows, c0 + LANES:c0 + 2 * LANES]], axis=0)
            s = _dot_nt(q2, k_bd)
            e_rows, sink_rows = [], []
            for p in range(2):
                e_cols, sink_cols = [], []
                for hh in range(2):
                    sink = sink_ref[4 * j + 2 * p + hh]
                    s_prev = s[p * blk:(p + 1) * blk, 2 * hh * blk:(2 * hh + 1) * blk]
                    s_own = s[p * blk:(p + 1) * blk, (2 * hh + 1) * blk:(2 * hh + 2) * blk]
                    if n == 0:
                        s_prev = jnp.where(prev_ok, s_prev, -jnp.inf)
                    sc = jnp.where(own, s_own, s_prev)
                    m = jnp.maximum(jnp.max(sc, axis=-1, keepdims=True), sink)
                    eb = jnp.exp(sc - m).astype(BF16)
                    e_cols += [jnp.where(own, zero, eb), jnp.where(own, eb, zero)]
                    sink_cols.append(jnp.exp(sink - m))
                e_rows.append(jnp.concatenate(e_cols, axis=1))
                sink_rows.append(jnp.where(lo_q, sink_cols[0], sink_cols[1]))
            o = _dot(jnp.concatenate(e_rows, axis=0), jnp.concatenate([v_bd, ones_bd], axis=1))
            for p in range(2):
                op = o[p * blk:(p + 1) * blk]
                o_ref[rows, c0 + p * LANES:c0 + (p + 1) * LANES] = (
                    op[:, :LANES] / (op[:, LANES:] + sink_rows[p])).astype(BF16)


def _attention(sinks, q, k, v, batch, seq):
    tq = ATTN_TILE
    nt = seq // tq
    per = tq // ATTN_BLOCK
    cur = lambda b, t: (b * nt + t, 0)
    prev = lambda b, t: (jnp.maximum((b * nt + t) * per - 1, 0), 0)
    return pl.pallas_call(
        _attn_kernel,
        grid=(batch, nt),
        in_specs=[
            pl.BlockSpec(memory_space=pltpu.SMEM),
            pl.BlockSpec((tq, ATTN_Q_W), cur),
            pl.BlockSpec((tq, KV_BD_W), cur),
            pl.BlockSpec((ATTN_BLOCK, KV_BD_W), prev),
            pl.BlockSpec((tq, KV_BD_W), cur),
            pl.BlockSpec((ATTN_BLOCK, KV_BD_W), prev),
        ],
        out_specs=pl.BlockSpec((tq, ATTN_Q_W), cur),
        out_shape=jax.ShapeDtypeStruct(q.shape, BF16),
        compiler_params=pltpu.CompilerParams(dimension_semantics=("parallel", "parallel"), vmem_limit_bytes=VMEM_LIMIT),
        name="swa_attention",
    )(sinks, q, k, k, v, v)


def _mlstm_kernel(qk_ref, mv_ref, so_ref, grow_ref, gnext_ref, brow_ref, hn_ref, out_ref, state, m_scr, gate_scr):
    t = pl.program_id(1)
    TL = qk_ref.shape[0]
    L = M_CHUNK
    R = 2 * M_HEADS

    ri = lax.broadcasted_iota(jnp.int32, (L, L), 0)
    ci = lax.broadcasted_iota(jnp.int32, (L, L), 1)
    causal = ci <= ri
    triu = (ri <= ci).astype(BF16)
    lane_in_chunk = lax.broadcasted_iota(jnp.int32, (R, TL), 1) & (L - 1)
    lo_b = lax.broadcasted_iota(jnp.int32, (L, LANES), 1) < M_QK_DIM
    top = lax.broadcasted_iota(jnp.int32, (L, LANES), 0) < M_QK_DIM
    top2 = lax.broadcasted_iota(jnp.int32, (L, 2 * M_V_DIM), 0) < M_QK_DIM
    ones_v = jnp.ones((L, M_V_DIM), BF16)
    zero_b = jnp.zeros((), BF16)
    chunk_rows = [slice(c * L, (c + 1) * L) for c in range(TL // L)]

    def row_bcast(x8, h, n=L):
        return jnp.broadcast_to(x8[h:h + 1, :], (n, x8.shape[1]))

    def gate_scan(g_ref):
        gr = g_ref[...] + brow_ref[...]
        lf = jax.nn.log_sigmoid(gr)
        b8 = jnp.concatenate([sum(_dot(part, triu) for part in _split3(lf[:, rows])) for rows in chunk_rows], axis=1)
        b_all = pltpu.roll(b8, M_HEADS, 0)
        r_all = gr - b_all
        cm_all = r_all
        sh = 1
        while sh < L:
            cm_all = jnp.where(lane_in_chunk >= sh, jnp.maximum(cm_all, pltpu.roll(cm_all, sh, 1)), cm_all)
            sh *= 2
        return b_all, r_all, cm_all

    @pl.when(t == 0)
    def _():
        state[...] = jnp.zeros(state.shape, F32)
        m_scr[...] = jnp.zeros(m_scr.shape, F32)
        for i, x in enumerate(gate_scan(grow_ref)):
            gate_scr[i] = x

    b_all, r_all, cm_all = gate_scr[0], gate_scr[1], gate_scr[2]
    next_gates = gate_scan(gnext_ref)

    m = m_scr[...]
    gates = []
    for rows in chunk_rows:
        b, r, cm = b_all[:, rows], r_all[:, rows], cm_all[:, rows]
        rmax = jnp.broadcast_to(cm[:, L - 1:L], (R, L))
        g = jnp.broadcast_to(b[:, L - 1:L], (R, L))
        big_m = jnp.maximum(m, cm)
        inter = jnp.exp(m - big_m)
        emt = jnp.exp(-(b + big_m))
        w = jnp.exp(r - rmax)
        m_loc = g + rmax
        m_new = jnp.maximum(g + m, m_loc)
        decay = jnp.exp(g + m - m_new)
        scale = jnp.exp(m_loc - m_new)
        m = m_new
        gates.append((r, w, decay, scale, big_m, inter, emt))
    m_scr[...] = m

    for rows, (r, w, decay, scale, big_m, inter, emt) in zip(chunk_rows, gates):
        for p in range(M_HEADS // 2):
            pc = slice(p * LANES, (p + 1) * LANES)
            q_pair_b = qk_ref[rows, pc]
            k_pair_b = qk_ref[rows, M_QK_W + p * LANES:M_QK_W + (p + 1) * LANES]
            c_prev = state[p]
            c_prev_b = c_prev.astype(BF16)
            inter_cols = jnp.where(top, row_bcast(inter, 2 * p), row_bcast(inter, 2 * p + 1)).T
            qs_pair = (q_pair_b.astype(F32) * inter_cols).astype(BF16)
            v_exts = []
            for hh in range(2):
                hd = 2 * p + hh
                cols = slice(hd * M_V_DIM, (hd + 1) * M_V_DIM)
                sel = lo_b if hh == 0 else jnp.logical_not(lo_b)
                v_ext = jnp.concatenate([mv_ref[rows, cols], ones_v], axis=1)
                v_exts.append(v_ext)
                m_col, emt_col = (row_bcast(x8, hd).T for x8 in (big_m, emt))

                d_mat = jnp.exp(jnp.where(causal, row_bcast(r, hd), -jnp.inf) - m_col)
                k_h = jnp.where(sel, k_pair_b, zero_b)
                s_mat = (_dot_nt(q_pair_b, k_h) * d_mat).astype(BF16)
                qs = jnp.where(sel, qs_pair, zero_b)
                nd = _dot(jnp.concatenate([s_mat, qs], axis=1), jnp.concatenate([v_ext, c_prev_b], axis=0))
                hcell = nd[:, :M_V_DIM] / jnp.maximum(jnp.abs(nd[:, M_V_DIM:]), emt_col)
                cell = _rms(hcell) * hn_ref[:, cols]
                out_ref[rows, cols] = (so_ref[rows, cols].astype(F32) * cell).astype(BF16)

            k_t = k_pair_b.astype(F32).T
            w_s = jnp.where(top, row_bcast(w, 2 * p), row_bcast(w, 2 * p + 1))
            kw_t = (k_t * w_s).astype(BF16)
            lhs = jnp.concatenate([jnp.where(top, kw_t, zero_b), jnp.where(top, zero_b, kw_t)], axis=1)
            a = _dot(lhs, jnp.concatenate(v_exts, axis=0))
            dec = jnp.where(top2, row_bcast(decay, 2 * p, L)[:, :1], row_bcast(decay, 2 * p + 1, L)[:, :1])
            sc = jnp.where(top2, row_bcast(scale, 2 * p, L)[:, :1], row_bcast(scale, 2 * p + 1, L)[:, :1])
            state[p] = dec * c_prev + sc * a

    for i, x in enumerate(next_gates):
        gate_scr[i] = x


def _mlstm(qk, mv, so, grow, brow, hn, batch, seq):
    tl = MLSTM_TILE
    nt = seq // tl
    cur = lambda b, t: (b * nt + t, 0)
    const = lambda b, t: (0, 0)
    return pl.pallas_call(
        _mlstm_kernel,
        grid=(batch, nt),
        in_specs=[
            pl.BlockSpec((tl, 2 * M_QK_W), cur),
            pl.BlockSpec((tl, M_V_W), cur),
            pl.BlockSpec((tl, M_V_W), cur),
            pl.BlockSpec((2 * M_HEADS, tl), lambda b, t: (0, b * nt + t)),
            pl.BlockSpec((2 * M_HEADS, tl), lambda b, t: (0, b * nt + jnp.minimum(t + 1, nt - 1))),
            pl.BlockSpec(brow.shape, const),
            pl.BlockSpec(hn.shape, const),
        ],
        out_specs=pl.BlockSpec((tl, M_V_W), cur),
        out_shape=jax.ShapeDtypeStruct(mv.shape, BF16),
        scratch_shapes=[
            pltpu.VMEM((M_HEADS // 2, 2 * M_QK_DIM, 2 * M_V_DIM), F32),
            pltpu.VMEM((2 * M_HEADS, M_CHUNK), F32),
            pltpu.VMEM((3, 2 * M_HEADS, tl), F32),
        ],
        compiler_params=pltpu.CompilerParams(dimension_semantics=("parallel", "arbitrary"), vmem_limit_bytes=VMEM_LIMIT),
        name="mlstm",
    )(qk, mv, so, grow, grow, brow, hn)


def _ff_chunks():
    return [(c0, min(c0 + FF_CHUNK, D_FF)) for c0 in range(0, D_FF, FF_CHUNK)]


def _mix_ffn_kernel(x_ref, h_ref, a_ref, m_ref, gpost_ref, gpre_ref, gffn_ref, wg_ref, wa_ref, wm_ref, wo_ref,
                    wfi_ref, wfo_ref, o_ref, acc_ref):
    x = x_ref[...]
    h = h_ref[...]
    a = a_ref[...]
    m = m_ref[...]
    parts = []
    for c0 in range(0, D_MODEL, MERGE_CHUNK):
        cols = slice(c0, c0 + MERGE_CHUNK)
        gcols = slice(D_MODEL + c0, D_MODEL + c0 + MERGE_CHUNK)
        g_attn = jax.nn.sigmoid(_dot(h, wg_ref[:, cols]))
        g_mlstm = jax.nn.sigmoid(_dot(h, wg_ref[:, gcols]))
        parts.append((g_attn * _dot(a, wa_ref[:, cols]) + g_mlstm * _dot(m, wm_ref[:, cols])).astype(BF16))
    y = _dot(jnp.concatenate(parts, axis=1), wo_ref[...])
    x1 = x + _rms(y) * gpost_ref[...]

    h2 = (_rms(x1) * gpre_ref[...]).astype(BF16)
    for c0, c1 in _ff_chunks():
        gate = _dot(h2, wfi_ref[:, c0:c1])
        up = _dot(h2, wfi_ref[:, D_FF + c0:D_FF + c1])
        act = (gate * jax.nn.sigmoid(gate) * up).astype(BF16)
        part = _dot(act, wfo_ref[c0:c1, :])
        if c0 == 0:
            acc_ref[...] = part
        else:
            acc_ref[...] += part
    o_ref[...] = x1 + _rms(acc_ref[...]) * gffn_ref[...]


def _mix_ffn(x2, h, attn, ml, gpost, gpre, gffn, wg, wa, wm, wo, wfi, wfo):
    T = x2.shape[0]
    tm = TOK_TILE
    row = lambda i: (i, 0)
    const = lambda i: (0, 0)

    def wspec(w):
        return pl.BlockSpec(w.shape, const, pipeline_mode=pl.Buffered(1))

    gain = pl.BlockSpec((1, D_MODEL), const)
    return pl.pallas_call(
        _mix_ffn_kernel,
        grid=(T // tm,),
        in_specs=[
            pl.BlockSpec((tm, D_MODEL), row),
            pl.BlockSpec((tm, D_MODEL), row),
            pl.BlockSpec((tm, ATTN_Q_W), row),
            pl.BlockSpec((tm, M_V_W), row),
            gain, gain, gain,
            wspec(wg), wspec(wa), wspec(wm), wspec(wo), wspec(wfi), wspec(wfo),
        ],
        out_specs=pl.BlockSpec((tm, D_MODEL), row),
        out_shape=jax.ShapeDtypeStruct(x2.shape, F32),
        scratch_shapes=[pltpu.VMEM((tm, D_MODEL), F32)],
        compiler_params=pltpu.CompilerParams(dimension_semantics=("parallel",), vmem_limit_bytes=VMEM_LIMIT),
        name="mix_ffn",
    )(x2, h, attn, ml, gpost, gpre, gffn, wg, wa, wm, wo, wfi, wfo)


def _rope_tables(seq):
    f32 = np.float32
    inv_freq = (f32(ROPE_THETA) ** (-np.arange(0, HEAD_DIM, 2, dtype=f32) / f32(HEAD_DIM))).astype(f32)
    ang = np.arange(seq).astype(f32)[:, None] * inv_freq[None, :]
    emb = np.concatenate([ang, ang], axis=-1)
    cos = np.cos(emb).astype(f32)
    sin = np.sin(emb).astype(f32)
    first_half = np.arange(HEAD_DIM) < HEAD_DIM // 2
    sin_a = np.where(first_half, -sin, f32(0))
    sin_b = np.where(first_half, f32(0), sin)
    rep = LANES // HEAD_DIM
    return tuple(jnp.asarray(np.tile(t, (1, rep))) for t in (cos, sin_a, sin_b))


def _layer(x2, batch, seq, norm_pre_mix, norm_post_mix, norm_pre_ffn, norm_post_ffn, w_in, attn_sinks, conv_w,
           conv_b, b_igate, b_fgate, mlstm_head_norm, w_attn_branch, w_mlstm_branch, w_out, w_ffn_in, w_ffn_out):
    w_in_t = jnp.transpose(w_in)
    w_mix = _w_in_columns(w_in_t, 0, O_GATES)
    wif = jnp.pad(w_in[:, O_GATES:O_BRANCH], ((0, 0), (0, LANES - 2 * M_HEADS))).astype(BF16)
    wg = _w_in_columns(w_in_t, O_BRANCH, 2 * D_MODEL)
    cos, sa, sb = _rope_tables(seq)

    h, q, k, v, qk, mv, so, grow = _inproj(
        x2, norm_pre_mix[None, :], cos, sa, sb, conv_w, conv_b[None, :], w_mix, wif, seq)
    attn = _attention(attn_sinks, q, k, v, batch, seq)
    bias = jnp.concatenate([b_igate, b_fgate])
    ml = _mlstm(qk, mv, so, grow, bias[:, None], mlstm_head_norm[None, :], batch, seq)
    return _mix_ffn(x2, h, attn, ml, norm_post_mix[None, :], norm_pre_ffn[None, :],
                    norm_post_ffn[None, :], wg, w_attn_branch.astype(BF16), w_mlstm_branch.astype(BF16),
                    w_out.astype(BF16), w_ffn_in.astype(BF16), w_ffn_out.astype(BF16))


def kernel(x, norm_pre_mix, norm_post_mix, norm_pre_ffn, norm_post_ffn, w_in, attn_sinks, conv_w, conv_b, b_igate,
           b_fgate, mlstm_head_norm, w_attn_branch, w_mlstm_branch, w_out, w_ffn_in, w_ffn_out):
    B, S, D = x.shape
    assert D == D_MODEL and WINDOW == ATTN_BLOCK == M_CHUNK
    assert S % max(INPROJ_TILE, ATTN_TILE, MLSTM_TILE) == 0 and (B * S) % TOK_TILE == 0
    x2 = x.reshape(B * S, D)
    for l in range(w_in.shape[0]):
        x2 = _layer(x2, B, S, norm_pre_mix[l], norm_post_mix[l], norm_pre_ffn[l], norm_post_ffn[l], w_in[l],
                    attn_sinks[l], conv_w[l], conv_b[l], b_igate[l], b_fgate[l], mlstm_head_norm[l],
                    w_attn_branch[l], w_mlstm_branch[l], w_out[l], w_ffn_in[l], w_ffn_out[l])
    return x2.reshape(B, S, D)
```

```python
import functools

import numpy as np

import jax
import jax.numpy as jnp
from jax import lax
from jax.experimental import pallas as pl
from jax.experimental.pallas import tpu as pltpu

D_MODEL = 1024
HEAD_DIM = 64
N_Q_HEADS = 8
N_KV_HEADS = 2
WINDOW = 128
ATTN_BLOCK = 128
ROPE_THETA = 10000.0
M_HEADS = 4
M_QK_DIM = 64
M_V_DIM = 128
M_CHUNK = 128
CONV_WIDTH = 4
D_FF = 2816
EPS = 1e-6

ATTN_Q_W = N_Q_HEADS * HEAD_DIM
ATTN_KV_W = N_KV_HEADS * HEAD_DIM
M_QK_W = M_HEADS * M_QK_DIM
M_V_W = M_HEADS * M_V_DIM
KV_BD_W = 2 * ATTN_KV_W * 2
O_MQK = ATTN_Q_W + 2 * ATTN_KV_W
O_MV = O_MQK + 2 * M_QK_W
O_GATES = O_MV + 2 * M_V_W
O_BRANCH = O_GATES + 2 * M_HEADS

LANES = 128
SUBLANES = 8
VMEM_LIMIT = 56 * 1024 * 1024
CONV_HALO = SUBLANES

TOK_TILE = 512
INPROJ_TILE = 1024
ATTN_TILE = 2048
MLSTM_TILE = 2048
FF_CHUNK = 256
MERGE_CHUNK = 256
W_COLS_PER_STEP = 1152

F32 = jnp.float32
BF16 = jnp.bfloat16


def _dot(a, b):
    return jnp.dot(a, b, preferred_element_type=F32)


def _dot_nt(a, b):
    return lax.dot_general(a, b, (((1,), (1,)), ((), ())), preferred_element_type=F32)


def _rms(x):
    return x * lax.rsqrt(jnp.mean(x * x, axis=-1, keepdims=True) + EPS)


def _split3(x):
    hi = x.astype(BF16)
    r1 = x - hi.astype(F32)
    mid = r1.astype(BF16)
    lo = (r1 - mid.astype(F32)).astype(BF16)
    return hi, mid, lo


def _transpose_cast_kernel(wt_ref, o_ref):
    o_ref[...] = wt_ref[...].T.astype(BF16)


def _w_in_columns(w_in_t, col0, n_cols):
    k = w_in_t.shape[1]
    step = max(c for c in range(LANES, n_cols + 1, LANES) if n_cols % c == 0 and (c <= W_COLS_PER_STEP or c == n_cols))
    return pl.pallas_call(
        _transpose_cast_kernel,
        grid=(n_cols // step,),
        in_specs=[pl.BlockSpec((pl.Element(step), pl.Element(k)),
                               lambda j: (pl.multiple_of(col0 + j * step, SUBLANES), 0))],
        out_specs=pl.BlockSpec((k, step), lambda j: (0, j)),
        out_shape=jax.ShapeDtypeStruct((k, n_cols), BF16),
        compiler_params=pltpu.CompilerParams(dimension_semantics=("parallel",), vmem_limit_bytes=VMEM_LIMIT),
        name="w_in_columns",
    )(w_in_t)


def _inproj_kernel(x_ref, g_ref, cos_ref, sa_ref, sb_ref, cw_ref, cb_ref, w_ref,
                   h_ref, q_ref, k_ref, v_ref, qk_ref, mv_ref, so_ref, grow_ref, cbuf, *, n_seq_tiles):
    tm = x_ref.shape[0]
    h = (_rms(x_ref[...]) * g_ref[...]).astype(BF16)
    h_ref[...] = h
    cos = cos_ref[...]
    sa = sa_ref[...]
    sb = sb_ref[...]
    lane = lax.broadcasted_iota(jnp.int32, cos.shape, 1)
    lo = lane < HEAD_DIM

    def rope(t):
        return t * cos + pltpu.roll(t, LANES - HEAD_DIM // 2, 1) * sa + pltpu.roll(t, HEAD_DIM // 2, 1) * sb

    def store_block_diag(ref, t):
        tb = t.astype(BF16)
        rb = pltpu.roll(t, HEAD_DIM, 1).astype(BF16)
        zero = jnp.zeros((), BF16)
        for i, (src, keep_lo) in enumerate(((tb, True), (rb, False), (rb, True), (tb, False))):
            ref[:, i * LANES:(i + 1) * LANES] = jnp.where(lo, src, zero) if keep_lo else jnp.where(lo, zero, src)

    def attn_proj():
        a = _dot(h, w_ref[:, :O_MQK])
        for j in range(ATTN_Q_W // LANES):
            q_ref[:, j * LANES:(j + 1) * LANES] = (
                rope(a[:, j * LANES:(j + 1) * LANES]) * (HEAD_DIM ** -0.5)).astype(BF16)
        store_block_diag(k_ref, rope(a[:, ATTN_Q_W:ATTN_Q_W + LANES]))
        store_block_diag(v_ref, a[:, ATTN_Q_W + LANES:ATTN_Q_W + 2 * LANES])

    @pl.when(pl.program_id(0) % n_seq_tiles == 0)
    def _():
        cbuf[0:CONV_HALO, :] = jnp.zeros((CONV_HALO, cbuf.shape[1]), F32)

    def mqk_proj(c0, c1):
        cbuf[CONV_HALO:CONV_HALO + tm, c0:c1] = _dot(h, w_ref[:, O_MQK + c0:O_MQK + c1])

    def conv_chunk(j0):
        cols = slice(j0 * LANES, (j0 + 1) * LANES)
        y = cb_ref[:, cols]
        for j in range(CONV_WIDTH):
            off = CONV_HALO - (CONV_WIDTH - 1) + j
            y = y + cw_ref[j:j + 1, cols] * cbuf[off:off + tm, cols]
        cbuf[0:CONV_HALO, cols] = cbuf[tm:tm + CONV_HALO, cols]
        y = y * jax.nn.sigmoid(y)
        if j0 * LANES >= M_QK_W:
            y = y * (M_QK_DIM ** -0.5)
        qk_ref[:, cols] = y.astype(BF16)

    def mlstm_vo_proj():
        vo = _dot(h, w_ref[:, O_MV:O_GATES + LANES])
        mv_ref[...] = vo[:, :M_V_W].astype(BF16)
        so_ref[...] = jax.nn.sigmoid(vo[:, M_V_W:2 * M_V_W]).astype(BF16)
        grow_ref[...] = vo[:, 2 * M_V_W:].T[:2 * M_HEADS, :]

    mqk_proj(0, M_QK_W)
    conv_chunk(0)
    conv_chunk(1)
    attn_proj()
    mqk_proj(M_QK_W, 2 * M_QK_W)
    conv_chunk(2)
    conv_chunk(3)
    mlstm_vo_proj()


def _inproj(x2, g, cos, sa, sb, cw, cb, w, seq):
    T = x2.shape[0]
    tm = INPROJ_TILE
    n_seq_tiles = seq // tm
    row = lambda i: (i, 0)
    pos = lambda i: (i % n_seq_tiles, 0)
    const = lambda i: (0, 0)

    def wspec(w):
        return pl.BlockSpec(w.shape, const, pipeline_mode=pl.Buffered(1))

    out_shape = (
        jax.ShapeDtypeStruct((T, D_MODEL), BF16),
        jax.ShapeDtypeStruct((T, ATTN_Q_W), BF16),
        jax.ShapeDtypeStruct((T, KV_BD_W), BF16),
        jax.ShapeDtypeStruct((T, KV_BD_W), BF16),
        jax.ShapeDtypeStruct((T, 2 * M_QK_W), BF16),
        jax.ShapeDtypeStruct((T, M_V_W), BF16),
        jax.ShapeDtypeStruct((T, M_V_W), BF16),
        jax.ShapeDtypeStruct((2 * M_HEADS, T), F32),
    )
    out_specs = (
        pl.BlockSpec((tm, D_MODEL), row),
        pl.BlockSpec((tm, ATTN_Q_W), row),
        pl.BlockSpec((tm, KV_BD_W), row),
        pl.BlockSpec((tm, KV_BD_W), row),
        pl.BlockSpec((tm, 2 * M_QK_W), row),
        pl.BlockSpec((tm, M_V_W), row),
        pl.BlockSpec((tm, M_V_W), row),
        pl.BlockSpec((2 * M_HEADS, tm), lambda i: (0, i)),
    )
    in_specs = [
        pl.BlockSpec((tm, D_MODEL), row),
        pl.BlockSpec((1, D_MODEL), const),
        pl.BlockSpec((tm, LANES), pos),
        pl.BlockSpec((tm, LANES), pos),
        pl.BlockSpec((tm, LANES), pos),
        pl.BlockSpec(cw.shape, const),
        pl.BlockSpec(cb.shape, const),
        wspec(w),
    ]
    return pl.pallas_call(
        functools.partial(_inproj_kernel, n_seq_tiles=n_seq_tiles),
        grid=(T // tm,),
        in_specs=in_specs,
        out_specs=out_specs,
        out_shape=out_shape,
        scratch_shapes=[pltpu.VMEM((CONV_HALO + tm, 2 * M_QK_W), F32)],
        compiler_params=pltpu.CompilerParams(dimension_semantics=("arbitrary",), vmem_limit_bytes=VMEM_LIMIT),
        name="inproj",
    )(x2, g, cos, sa, sb, cw, cb, w)


def _attn_kernel(sink_ref, q_ref, kc_ref, kp_ref, vc_ref, vp_ref, o_ref):
    t = pl.program_id(1)
    blk = ATTN_BLOCK
    n_blk = q_ref.shape[0] // blk
    lo_q = lax.broadcasted_iota(jnp.int32, (blk, LANES), 1) < HEAD_DIM
    ri = lax.broadcasted_iota(jnp.int32, (blk, blk), 0)
    ci = lax.broadcasted_iota(jnp.int32, (blk, blk), 1)
    own = ci <= ri
    prev_ok = ci > ri + jnp.where(t > 0, 0, blk)
    zero = jnp.zeros((), BF16)
    row_is_a = lax.broadcasted_iota(jnp.int32, (4 * blk, LANES), 0) < 2 * blk
    lane_is_a = lax.broadcasted_iota(jnp.int32, (4 * blk, LANES), 1) < HEAD_DIM
    ones_bd = (row_is_a == lane_is_a).astype(BF16)

    for n in range(n_blk):
        rows = slice(n * blk, (n + 1) * blk)
        k_prev = kp_ref if n == 0 else kc_ref.at[(n - 1) * blk:n * blk, :]
        v_prev = vp_ref if n == 0 else vc_ref.at[(n - 1) * blk:n * blk, :]
        for j in range(N_KV_HEADS):
            kv_cols = [slice((2 * j + i) * LANES, (2 * j + i + 1) * LANES) for i in range(2)]
            k_bd = jnp.concatenate([x for c in kv_cols for x in (k_prev[:, c], kc_ref[rows, c])], axis=0)
            v_bd = jnp.concatenate([x for c in kv_cols for x in (v_prev[:, c], vc_ref[rows, c])], axis=0)
            c0 = 2 * j * LANES
            q2 = jnp.concatenate([q_ref[rows, c0:c0 + LANES], q_ref[rows, c0 + LANES:c0 + 2 * LANES]], axis=0)
            s = _dot_nt(q2, k_bd)
            e_rows, sink_rows = [], []
            for p in range(2):
                e_cols, sink_cols = [], []
                for hh in range(2):
                    sink = sink_ref[4 * j + 2 * p + hh]
                    s_prev = s[p * blk:(p + 1) * blk, 2 * hh * blk:(2 * hh + 1) * blk]
                    s_own = s[p * blk:(p + 1) * blk, (2 * hh + 1) * blk:(2 * hh + 2) * blk]
                    if n == 0:
                        s_prev = jnp.where(prev_ok, s_prev, -jnp.inf)
                    sc = jnp.where(own, s_own, s_prev)
                    m = jnp.maximum(jnp.max(sc, axis=-1, keepdims=True), sink)
                    eb = jnp.exp(sc - m).astype(BF16)
                    e_cols += [jnp.where(own, zero, eb), jnp.where(own, eb, zero)]
                    sink_cols.append(jnp.exp(sink - m))
                e_rows.append(jnp.concatenate(e_cols, axis=1))
                sink_rows.append(jnp.where(lo_q, sink_cols[0], sink_cols[1]))
            o = _dot(jnp.concatenate(e_rows, axis=0), jnp.concatenate([v_bd, ones_bd], axis=1))
            for p in range(2):
                op = o[p * blk:(p + 1) * blk]
                o_ref[rows, c0 + p * LANES:c0 + (p + 1) * LANES] = (
                    op[:, :LANES] / (op[:, LANES:] + sink_rows[p])).astype(BF16)


def _attention(sinks, q, k, v, batch, seq):
    tq = ATTN_TILE
    nt = seq // tq
    per = tq // ATTN_BLOCK
    cur = lambda b, t: (b * nt + t, 0)
    prev = lambda b, t: (jnp.maximum((b * nt + t) * per - 1, 0), 0)
    return pl.pallas_call(
        _attn_kernel,
        grid=(batch, nt),
        in_specs=[
            pl.BlockSpec(memory_space=pltpu.SMEM),
            pl.BlockSpec((tq, ATTN_Q_W), cur),
            pl.BlockSpec((tq, KV_BD_W), cur),
            pl.BlockSpec((ATTN_BLOCK, KV_BD_W), prev),
            pl.BlockSpec((tq, KV_BD_W), cur),
            pl.BlockSpec((ATTN_BLOCK, KV_BD_W), prev),
        ],
        out_specs=pl.BlockSpec((tq, ATTN_Q_W), cur),
        out_shape=jax.ShapeDtypeStruct(q.shape, BF16),
        compiler_params=pltpu.CompilerParams(dimension_semantics=("parallel", "parallel"), vmem_limit_bytes=VMEM_LIMIT),
        name="swa_attention",
    )(sinks, q, k, k, v, v)


def _mlstm_kernel(qk_ref, mv_ref, so_ref, grow_ref, gnext_ref, brow_ref, hn_ref, out_ref, state, m_scr, gate_scr):
    t = pl.program_id(1)
    TL = qk_ref.shape[0]
    L = M_CHUNK
    R = 2 * M_HEADS

    ri = lax.broadcasted_iota(jnp.int32, (L, L), 0)
    ci = lax.broadcasted_iota(jnp.int32, (L, L), 1)
    causal = ci <= ri
    triu = (ri <= ci).astype(BF16)
    lane_in_chunk = lax.broadcasted_iota(jnp.int32, (R, TL), 1) & (L - 1)
    lo_b = lax.broadcasted_iota(jnp.int32, (L, LANES), 1) < M_QK_DIM
    top = lax.broadcasted_iota(jnp.int32, (L, LANES), 0) < M_QK_DIM
    top2 = lax.broadcasted_iota(jnp.int32, (L, 2 * M_V_DIM), 0) < M_QK_DIM
    ones_v = jnp.ones((L, M_V_DIM), BF16)
    zero_b = jnp.zeros((), BF16)
    chunk_rows = [slice(c * L, (c + 1) * L) for c in range(TL // L)]

    def row_bcast(x8, h, n=L):
        return jnp.broadcast_to(x8[h:h + 1, :], (n, x8.shape[1]))

    def gate_scan(g_ref):
        gr = g_ref[...] + brow_ref[...]
        lf = jax.nn.log_sigmoid(gr)
        b8 = jnp.concatenate([sum(_dot(part, triu) for part in _split3(lf[:, rows])) for rows in chunk_rows], axis=1)
        b_all = pltpu.roll(b8, M_HEADS, 0)
        r_all = gr - b_all
        cm_all = r_all
        sh = 1
        while sh < L:
            cm_all = jnp.where(lane_in_chunk >= sh, jnp.maximum(cm_all, pltpu.roll(cm_all, sh, 1)), cm_all)
            sh *= 2
        return b_all, r_all, cm_all

    @pl.when(t == 0)
    def _():
        state[...] = jnp.zeros(state.shape, F32)
        m_scr[...] = jnp.zeros(m_scr.shape, F32)
        for i, x in enumerate(gate_scan(grow_ref)):
            gate_scr[i] = x

    b_all, r_all, cm_all = gate_scr[0], gate_scr[1], gate_scr[2]
    next_gates = gate_scan(gnext_ref)

    m = m_scr[...]
    gates = []
    for rows in chunk_rows:
        b, r, cm = b_all[:, rows], r_all[:, rows], cm_all[:, rows]
        rmax = jnp.broadcast_to(cm[:, L - 1:L], (R, L))
        g = jnp.broadcast_to(b[:, L - 1:L], (R, L))
        big_m = jnp.maximum(m, cm)
        inter = jnp.exp(m - big_m)
        emt = jnp.exp(-(b + big_m))
        w = jnp.exp(r - rmax)
        m_loc = g + rmax
        m_new = jnp.maximum(g + m, m_loc)
        decay = jnp.exp(g + m - m_new)
        scale = jnp.exp(m_loc - m_new)
        m = m_new
        gates.append((r, w, decay, scale, big_m, inter, emt))
    m_scr[...] = m

    for rows, (r, w, decay, scale, big_m, inter, emt) in zip(chunk_rows, gates):
        for p in range(M_HEADS // 2):
            pc = slice(p * LANES, (p + 1) * LANES)
            q_pair_b = qk_ref[rows, pc]
            k_pair_b = qk_ref[rows, M_QK_W + p * LANES:M_QK_W + (p + 1) * LANES]
            c_prev = state[p]
            c_prev_b = c_prev.astype(BF16)
            inter_cols = jnp.where(top, row_bcast(inter, 2 * p), row_bcast(inter, 2 * p + 1)).T
            qs_pair = (q_pair_b.astype(F32) * inter_cols).astype(BF16)
            v_exts = []
            for hh in range(2):
                hd = 2 * p + hh
                cols = slice(hd * M_V_DIM, (hd + 1) * M_V_DIM)
                sel = lo_b if hh == 0 else jnp.logical_not(lo_b)
                v_ext = jnp.concatenate([mv_ref[rows, cols], ones_v], axis=1)
                v_exts.append(v_ext)
                m_col, emt_col = (row_bcast(x8, hd).T for x8 in (big_m, emt))

                d_mat = jnp.exp(jnp.where(causal, row_bcast(r, hd), -jnp.inf) - m_col)
                k_h = jnp.where(sel, k_pair_b, zero_b)
                s_mat = (_dot_nt(q_pair_b, k_h) * d_mat).astype(BF16)
                qs = jnp.where(sel, qs_pair, zero_b)
                nd = _dot(jnp.concatenate([s_mat, qs], axis=1), jnp.concatenate([v_ext, c_prev_b], axis=0))
                hcell = nd[:, :M_V_DIM] / jnp.maximum(jnp.abs(nd[:, M_V_DIM:]), emt_col)
                cell = _rms(hcell) * hn_ref[:, cols]
                out_ref[rows, cols] = (so_ref[rows, cols].astype(F32) * cell).astype(BF16)

            k_t = k_pair_b.astype(F32).T
            w_s = jnp.where(top, row_bcast(w, 2 * p), row_bcast(w, 2 * p + 1))
            kw_t = (k_t * w_s).astype(BF16)
            lhs = jnp.concatenate([jnp.where(top, kw_t, zero_b), jnp.where(top, zero_b, kw_t)], axis=1)
            a = _dot(lhs, jnp.concatenate(v_exts, axis=0))
            dec = jnp.where(top2, row_bcast(decay, 2 * p, L)[:, :1], row_bcast(decay, 2 * p + 1, L)[:, :1])
            sc = jnp.where(top2, row_bcast(scale, 2 * p, L)[:, :1], row_bcast(scale, 2 * p + 1, L)[:, :1])
            state[p] = dec * c_prev + sc * a

    for i, x in enumerate(next_gates):
        gate_scr[i] = x


def _mlstm(qk, mv, so, grow, brow, hn, batch, seq):
    tl = MLSTM_TILE
    nt = seq // tl
    cur = lambda b, t: (b * nt + t, 0)
    const = lambda b, t: (0, 0)
    return pl.pallas_call(
        _mlstm_kernel,
        grid=(batch, nt),
        in_specs=[
            pl.BlockSpec((tl, 2 * M_QK_W), cur),
            pl.BlockSpec((tl, M_V_W), cur),
            pl.BlockSpec((tl, M_V_W), cur),
            pl.BlockSpec((2 * M_HEADS, tl), lambda b, t: (0, b * nt + t)),
            pl.BlockSpec((2 * M_HEADS, tl), lambda b, t: (0, b * nt + jnp.minimum(t + 1, nt - 1))),
            pl.BlockSpec(brow.shape, const),
            pl.BlockSpec(hn.shape, const),
        ],
        out_specs=pl.BlockSpec((tl, M_V_W), cur),
        out_shape=jax.ShapeDtypeStruct(mv.shape, BF16),
        scratch_shapes=[
            pltpu.VMEM((M_HEADS // 2, 2 * M_QK_DIM, 2 * M_V_DIM), F32),
            pltpu.VMEM((2 * M_HEADS, M_CHUNK), F32),
            pltpu.VMEM((3, 2 * M_HEADS, tl), F32),
        ],
        compiler_params=pltpu.CompilerParams(dimension_semantics=("parallel", "arbitrary"), vmem_limit_bytes=VMEM_LIMIT),
        name="mlstm",
    )(qk, mv, so, grow, grow, brow, hn)


def _ff_chunks():
    return [(c0, min(c0 + FF_CHUNK, D_FF)) for c0 in range(0, D_FF, FF_CHUNK)]


def _mix_ffn_kernel(x_ref, h_ref, a_ref, m_ref, gpost_ref, gpre_ref, gffn_ref, wg_ref, wa_ref, wm_ref, wo_ref,
                    wfi_ref, wfo_ref, o_ref, acc_ref):
    x = x_ref[...]
    h = h_ref[...]
    a = a_ref[...]
    m = m_ref[...]
    parts = []
    for c0 in range(0, D_MODEL, MERGE_CHUNK):
        cols = slice(c0, c0 + MERGE_CHUNK)
        gcols = slice(D_MODEL + c0, D_MODEL + c0 + MERGE_CHUNK)
        g_attn = jax.nn.sigmoid(_dot(h, wg_ref[:, cols]))
        g_mlstm = jax.nn.sigmoid(_dot(h, wg_ref[:, gcols]))
        parts.append((g_attn * _dot(a, wa_ref[:, cols]) + g_mlstm * _dot(m, wm_ref[:, cols])).astype(BF16))
    y = _dot(jnp.concatenate(parts, axis=1), wo_ref[...])
    x1 = x + _rms(y) * gpost_ref[...]

    h2 = (_rms(x1) * gpre_ref[...]).astype(BF16)
    for c0, c1 in _ff_chunks():
        gate = _dot(h2, wfi_ref[:, c0:c1])
        up = _dot(h2, wfi_ref[:, D_FF + c0:D_FF + c1])
        act = (gate * jax.nn.sigmoid(gate) * up).astype(BF16)
        part = _dot(act, wfo_ref[c0:c1, :])
        if c0 == 0:
            acc_ref[...] = part
        else:
            acc_ref[...] += part
    o_ref[...] = x1 + _rms(acc_ref[...]) * gffn_ref[...]


def _mix_ffn(x2, h, attn, ml, gpost, gpre, gffn, wg, wa, wm, wo, wfi, wfo):
    T = x2.shape[0]
    tm = TOK_TILE
    row = lambda i: (i, 0)
    const = lambda i: (0, 0)

    def wspec(w):
        return pl.BlockSpec(w.shape, const, pipeline_mode=pl.Buffered(1))

    gain = pl.BlockSpec((1, D_MODEL), const)
    return pl.pallas_call(
        _mix_ffn_kernel,
        grid=(T // tm,),
        in_specs=[
            pl.BlockSpec((tm, D_MODEL), row),
            pl.BlockSpec((tm, D_MODEL), row),
            pl.BlockSpec((tm, ATTN_Q_W), row),
            pl.BlockSpec((tm, M_V_W), row),
            gain, gain, gain,
            wspec(wg), wspec(wa), wspec(wm), wspec(wo), wspec(wfi), wspec(wfo),
        ],
        out_specs=pl.BlockSpec((tm, D_MODEL), row),
        out_shape=jax.ShapeDtypeStruct(x2.shape, F32),
        scratch_shapes=[pltpu.VMEM((tm, D_MODEL), F32)],
        compiler_params=pltpu.CompilerParams(dimension_semantics=("parallel",), vmem_limit_bytes=VMEM_LIMIT),
        name="mix_ffn",
    )(x2, h, attn, ml, gpost, gpre, gffn, wg, wa, wm, wo, wfi, wfo)


def _rope_tables(seq):
    f32 = np.float32
    inv_freq = (f32(ROPE_THETA) ** (-np.arange(0, HEAD_DIM, 2, dtype=f32) / f32(HEAD_DIM))).astype(f32)
    ang = np.arange(seq).astype(f32)[:, None] * inv_freq[None, :]
    emb = np.concatenate([ang, ang], axis=-1)
    cos = np.cos(emb).astype(f32)
    sin = np.sin(emb).astype(f32)
    first_half = np.arange(HEAD_DIM) < HEAD_DIM // 2
    sin_a = np.where(first_half, -sin, f32(0))
    sin_b = np.where(first_half, f32(0), sin)
    rep = LANES // HEAD_DIM
    return tuple(jnp.asarray(np.tile(t, (1, rep))) for t in (cos, sin_a, sin_b))


def _layer(x2, batch, seq, norm_pre_mix, norm_post_mix, norm_pre_ffn, norm_post_ffn, w_in, attn_sinks, conv_w,
           conv_b, b_igate, b_fgate, mlstm_head_norm, w_attn_branch, w_mlstm_branch, w_out, w_ffn_in, w_ffn_out):
    w_in_t = jnp.transpose(w_in)
    w_mix = _w_in_columns(w_in_t, 0, O_GATES + LANES)
    wg = _w_in_columns(w_in_t, O_BRANCH, 2 * D_MODEL)
    cos, sa, sb = _rope_tables(seq)

    h, q, k, v, qk, mv, so, grow = _inproj(
        x2, norm_pre_mix[None, :], cos, sa, sb, conv_w, conv_b[None, :], w_mix, seq)
    attn = _attention(attn_sinks, q, k, v, batch, seq)
    bias = jnp.concatenate([b_igate, b_fgate])
    ml = _mlstm(qk, mv, so, grow, bias[:, None], mlstm_head_norm[None, :], batch, seq)
    return _mix_ffn(x2, h, attn, ml, norm_post_mix[None, :], norm_pre_ffn[None, :],
                    norm_post_ffn[None, :], wg, w_attn_branch.astype(BF16), w_mlstm_branch.astype(BF16),
                    w_out.astype(BF16), w_ffn_in.astype(BF16), w_ffn_out.astype(BF16))


def kernel(x, norm_pre_mix, norm_post_mix, norm_pre_ffn, norm_post_ffn, w_in, attn_sinks, conv_w, conv_b, b_igate,
           b_fgate, mlstm_head_norm, w_attn_branch, w_mlstm_branch, w_out, w_ffn_in, w_ffn_out):
    B, S, D = x.shape
    assert D == D_MODEL and WINDOW == ATTN_BLOCK == M_CHUNK
    assert S % max(INPROJ_TILE, ATTN_TILE, MLSTM_TILE) == 0 and (B * S) % TOK_TILE == 0
    x2 = x.reshape(B * S, D)
    for l in range(w_in.shape[0]):
        x2 = _layer(x2, B, S, norm_pre_mix[l], norm_post_mix[l], norm_pre_ffn[l], norm_post_ffn[l], w_in[l],
                    attn_sinks[l], conv_w[l], conv_b[l], b_igate[l], b_fgate[l], mlstm_head_norm[l],
                    w_attn_branch[l], w_mlstm_branch[l], w_out[l], w_ffn_in[l], w_ffn_out[l])
    return x2.reshape(B, S, D)
```

```python
import functools

import numpy as np

import jax
import jax.numpy as jnp
from jax import lax
from jax.experimental import pallas as pl
from jax.experimental.pallas import tpu as pltpu

D_MODEL = 1024
HEAD_DIM = 64
N_Q_HEADS = 8
N_KV_HEADS = 2
WINDOW = 128
ATTN_BLOCK = 128
ROPE_THETA = 10000.0
M_HEADS = 4
M_QK_DIM = 64
M_V_DIM = 128
M_CHUNK = 128
CONV_WIDTH = 4
D_FF = 2816
EPS = 1e-6

ATTN_Q_W = N_Q_HEADS * HEAD_DIM
ATTN_KV_W = N_KV_HEADS * HEAD_DIM
M_QK_W = M_HEADS * M_QK_DIM
M_V_W = M_HEADS * M_V_DIM
KV_BD_W = 2 * ATTN_KV_W * 2
O_MQK = ATTN_Q_W + 2 * ATTN_KV_W
O_MV = O_MQK + 2 * M_QK_W
O_GATES = O_MV + 2 * M_V_W
O_BRANCH = O_GATES + 2 * M_HEADS

LANES = 128
SUBLANES = 8
VMEM_LIMIT = 56 * 1024 * 1024
CONV_HALO = SUBLANES

TOK_TILE = 512
INPROJ_TILE = 1024
ATTN_TILE = 2048
MLSTM_TILE = 2048
FF_CHUNK = 256
MERGE_CHUNK = 256
W_COLS_PER_STEP = 1152

F32 = jnp.float32
BF16 = jnp.bfloat16


def _dot(a, b):
    return jnp.dot(a, b, preferred_element_type=F32)


def _dot_nt(a, b):
    return lax.dot_general(a, b, (((1,), (1,)), ((), ())), preferred_element_type=F32)


def _rms(x):
    return x * lax.rsqrt(jnp.mean(x * x, axis=-1, keepdims=True) + EPS)


def _split3(x):
    hi = x.astype(BF16)
    r1 = x - hi.astype(F32)
    mid = r1.astype(BF16)
    lo = (r1 - mid.astype(F32)).astype(BF16)
    return hi, mid, lo


def _transpose_cast_kernel(wt_ref, o_ref):
    o_ref[...] = wt_ref[...].T.astype(BF16)


def _w_in_columns(w_in_t, col0, n_cols):
    k = w_in_t.shape[1]
    step = max(c for c in range(LANES, n_cols + 1, LANES) if n_cols % c == 0 and (c <= W_COLS_PER_STEP or c == n_cols))
    return pl.pallas_call(
        _transpose_cast_kernel,
        grid=(n_cols // step,),
        in_specs=[pl.BlockSpec((pl.Element(step), pl.Element(k)),
                               lambda j: (pl.multiple_of(col0 + j * step, SUBLANES), 0))],
        out_specs=pl.BlockSpec((k, step), lambda j: (0, j)),
        out_shape=jax.ShapeDtypeStruct((k, n_cols), BF16),
        compiler_params=pltpu.CompilerParams(dimension_semantics=("parallel",), vmem_limit_bytes=VMEM_LIMIT),
        name="w_in_columns",
    )(w_in_t)


def _inproj_kernel(x_ref, g_ref, cos_ref, sa_ref, sb_ref, cw_ref, cb_ref, w_ref,
                   h_ref, q_ref, k_ref, v_ref, qk_ref, mv_ref, so_ref, grow_ref, cbuf, *, n_seq_tiles):
    tm = x_ref.shape[0]
    h = (_rms(x_ref[...]) * g_ref[...]).astype(BF16)
    h_ref[...] = h
    cos = cos_ref[...]
    sa = sa_ref[...]
    sb = sb_ref[...]
    lane = lax.broadcasted_iota(jnp.int32, cos.shape, 1)
    lo = lane < HEAD_DIM

    def rope(t):
        return t * cos + pltpu.roll(t, LANES - HEAD_DIM // 2, 1) * sa + pltpu.roll(t, HEAD_DIM // 2, 1) * sb

    def store_block_diag(ref, t):
        tb = t.astype(BF16)
        rb = pltpu.roll(t, HEAD_DIM, 1).astype(BF16)
        zero = jnp.zeros((), BF16)
        for i, (src, keep_lo) in enumerate(((tb, True), (rb, False), (rb, True), (tb, False))):
            ref[:, i * LANES:(i + 1) * LANES] = jnp.where(lo, src, zero) if keep_lo else jnp.where(lo, zero, src)

    def attn_proj():
        a = _dot(h, w_ref[:, :O_MQK])
        for j in range(ATTN_Q_W // LANES):
            q_ref[:, j * LANES:(j + 1) * LANES] = (
                rope(a[:, j * LANES:(j + 1) * LANES]) * (HEAD_DIM ** -0.5)).astype(BF16)
        store_block_diag(k_ref, rope(a[:, ATTN_Q_W:ATTN_Q_W + LANES]))
        store_block_diag(v_ref, a[:, ATTN_Q_W + LANES:ATTN_Q_W + 2 * LANES])

    @pl.when(pl.program_id(0) % n_seq_tiles == 0)
    def _():
        cbuf[0:CONV_HALO, :] = jnp.zeros((CONV_HALO, cbuf.shape[1]), F32)

    def mqk_proj(c0, c1):
        cbuf[CONV_HALO:CONV_HALO + tm, c0:c1] = _dot(h, w_ref[:, O_MQK + c0:O_MQK + c1])

    def conv_chunk(j0):
        cols = slice(j0 * LANES, (j0 + 1) * LANES)
        y = cb_ref[:, cols]
        for j in range(CONV_WIDTH):
            off = CONV_HALO - (CONV_WIDTH - 1) + j
            y = y + cw_ref[j:j + 1, cols] * cbuf[off:off + tm, cols]
        cbuf[0:CONV_HALO, cols] = cbuf[tm:tm + CONV_HALO, cols]
        y = y * jax.nn.sigmoid(y)
        if j0 * LANES >= M_QK_W:
            y = y * (M_QK_DIM ** -0.5)
        qk_ref[:, cols] = y.astype(BF16)

    def mlstm_vo_proj():
        vo = _dot(h, w_ref[:, O_MV:O_GATES + LANES])
        mv_ref[...] = vo[:, :M_V_W].astype(BF16)
        so_ref[...] = jax.nn.sigmoid(vo[:, M_V_W:2 * M_V_W]).astype(BF16)
        grow_ref[...] = vo[:, 2 * M_V_W:].T[:2 * M_HEADS, :]

    mqk_proj(0, M_QK_W)
    conv_chunk(0)
    conv_chunk(1)
    attn_proj()
    mqk_proj(M_QK_W, 2 * M_QK_W)
    conv_chunk(2)
    conv_chunk(3)
    mlstm_vo_proj()


def _inproj(x2, g, cos, sa, sb, cw, cb, w, seq):
    T = x2.shape[0]
    tm = INPROJ_TILE
    n_seq_tiles = seq // tm
    row = lambda i: (i, 0)
    pos = lambda i: (i % n_seq_tiles, 0)
    const = lambda i: (0, 0)

    def wspec(w):
        return pl.BlockSpec(w.shape, const, pipeline_mode=pl.Buffered(1))

    out_shape = (
        jax.ShapeDtypeStruct((T, D_MODEL), BF16),
        jax.ShapeDtypeStruct((T, ATTN_Q_W), BF16),
        jax.ShapeDtypeStruct((T, KV_BD_W), BF16),
        jax.ShapeDtypeStruct((T, KV_BD_W), BF16),
        jax.ShapeDtypeStruct((T, 2 * M_QK_W), BF16),
        jax.ShapeDtypeStruct((T, M_V_W), BF16),
        jax.ShapeDtypeStruct((T, M_V_W), BF16),
        jax.ShapeDtypeStruct((2 * M_HEADS, T), F32),
    )
    out_specs = (
        pl.BlockSpec((tm, D_MODEL), row),
        pl.BlockSpec((tm, ATTN_Q_W), row),
        pl.BlockSpec((tm, KV_BD_W), row),
        pl.BlockSpec((tm, KV_BD_W), row),
        pl.BlockSpec((tm, 2 * M_QK_W), row),
        pl.BlockSpec((tm, M_V_W), row),
        pl.BlockSpec((tm, M_V_W), row),
        pl.BlockSpec((2 * M_HEADS, tm), lambda i: (0, i)),
    )
    in_specs = [
        pl.BlockSpec((tm, D_MODEL), row),
        pl.BlockSpec((1, D_MODEL), const),
        pl.BlockSpec((tm, LANES), pos),
        pl.BlockSpec((tm, LANES), pos),
        pl.BlockSpec((tm, LANES), pos),
        pl.BlockSpec(cw.shape, const),
        pl.BlockSpec(cb.shape, const),
        wspec(w),
    ]
    return pl.pallas_call(
        functools.partial(_inproj_kernel, n_seq_tiles=n_seq_tiles),
        grid=(T // tm,),
        in_specs=in_specs,
        out_specs=out_specs,
        out_shape=out_shape,
        scratch_shapes=[pltpu.VMEM((CONV_HALO + tm, 2 * M_QK_W), F32)],
        compiler_params=pltpu.CompilerParams(dimension_semantics=("arbitrary",), vmem_limit_bytes=VMEM_LIMIT),
        name="inproj",
    )(x2, g, cos, sa, sb, cw, cb, w)


def _attn_kernel(sink_ref, q_ref, kc_ref, kp_ref, vc_ref, vp_ref, o_ref):
    t = pl.program_id(1)
    blk = ATTN_BLOCK
    n_blk = q_ref.shape[0] // blk
    lo_q = lax.broadcasted_iota(jnp.int32, (blk, LANES), 1) < HEAD_DIM
    ri = lax.broadcasted_iota(jnp.int32, (blk, blk), 0)
    ci = lax.broadcasted_iota(jnp.int32, (blk, blk), 1)
    own = ci <= ri
    prev_ok = ci > ri + jnp.where(t > 0, 0, blk)
    zero = jnp.zeros((), BF16)
    row_is_a = lax.broadcasted_iota(jnp.int32, (4 * blk, LANES), 0) < 2 * blk
    lane_is_a = lax.broadcasted_iota(jnp.int32, (4 * blk, LANES), 1) < HEAD_DIM
    ones_bd = (row_is_a == lane_is_a).astype(BF16)

    for n in range(n_blk):
        rows = slice(n * blk, (n + 1) * blk)
        k_prev = kp_ref if n == 0 else kc_ref.at[(n - 1) * blk:n * blk, :]
        v_prev = vp_ref if n == 0 else vc_ref.at[(n - 1) * blk:n * blk, :]
        for j in range(N_KV_HEADS):
            kv_cols = [slice((2 * j + i) * LANES, (2 * j + i + 1) * LANES) for i in range(2)]
            k_bd = jnp.concatenate([x for c in kv_cols for x in (k_prev[:, c], kc_ref[rows, c])], axis=0)
            v_bd = jnp.concatenate([x for c in kv_cols for x in (v_prev[:, c], vc_ref[rows, c])], axis=0)
            c0 = 2 * j * LANES
            q2 = jnp.concatenate([q_ref[rows, c0:c0 + LANES], q_ref[rows, c0 + LANES:c0 + 2 * LANES]], axis=0)
            s = _dot_nt(q2, k_bd)
            e_rows, sink_rows = [], []
            for p in range(2):
                e_cols, sink_cols = [], []
                for hh in range(2):
                    sink = sink_ref[4 * j + 2 * p + hh]
                    s_prev = s[p * blk:(p + 1) * blk, 2 * hh * blk:(2 * hh + 1) * blk]
                    s_own = s[p * blk:(p + 1) * blk, (2 * hh + 1) * blk:(2 * hh + 2) * blk]
                    if n == 0:
                        s_prev = jnp.where(prev_ok, s_prev, -jnp.inf)
                    sc = jnp.where(own, s_own, s_prev)
                    m = jnp.maximum(jnp.max(sc, axis=-1, keepdims=True), sink)
                    eb = jnp.exp(sc - m).astype(BF16)
                    e_cols += [jnp.where(own, zero, eb), jnp.where(own, eb, zero)]
                    sink_cols.append(jnp.exp(sink - m))
                e_rows.append(jnp.concatenate(e_cols, axis=1))
                sink_rows.append(jnp.where(lo_q, sink_cols[0], sink_cols[1]))
            o = _dot(jnp.concatenate(e_rows, axis=0), jnp.concatenate([v_bd, ones_bd], axis=1))
            for p in range(2):
                op = o[p * blk:(p + 1) * blk]
                o_ref[rows, c0 + p * LANES:c0 + (p + 1) * LANES] = (
                    op[:, :LANES] / (op[:, LANES:] + sink_rows[p])).astype(BF16)


def _attention(sinks, q, k, v, batch, seq):
    tq = ATTN_TILE
    nt = seq // tq
    per = tq // ATTN_BLOCK
    cur = lambda b, t: (b * nt + t, 0)
    prev = lambda b, t: (jnp.maximum((b * nt + t) * per - 1, 0), 0)
    return pl.pallas_call(
        _attn_kernel,
        grid=(batch, nt),
        in_specs=[
            pl.BlockSpec(memory_space=pltpu.SMEM),
            pl.BlockSpec((tq, ATTN_Q_W), cur),
            pl.BlockSpec((tq, KV_BD_W), cur),
            pl.BlockSpec((ATTN_BLOCK, KV_BD_W), prev),
            pl.BlockSpec((tq, KV_BD_W), cur),
            pl.BlockSpec((ATTN_BLOCK, KV_BD_W), prev),
        ],
        out_specs=pl.BlockSpec((tq, ATTN_Q_W), cur),
        out_shape=jax.ShapeDtypeStruct(q.shape, BF16),
        compiler_params=pltpu.CompilerParams(dimension_semantics=("parallel", "parallel"), vmem_limit_bytes=VMEM_LIMIT),
        name="swa_attention",
    )(sinks, q, k, k, v, v)


def _mlstm_kernel(qk_ref, mv_ref, so_ref, grow_ref, gnext_ref, brow_ref, hn_ref, out_ref, state, m_scr, gate_scr):
    t = pl.program_id(1)
    TL = qk_ref.shape[0]
    L = M_CHUNK
    R = 2 * M_HEADS

    ri = lax.broadcasted_iota(jnp.int32, (L, L), 0)
    ci = lax.broadcasted_iota(jnp.int32, (L, L), 1)
    causal = ci <= ri
    triu = (ri <= ci).astype(BF16)
    lane_in_chunk = lax.broadcasted_iota(jnp.int32, (R, TL), 1) & (L - 1)
    lo_b = lax.broadcasted_iota(jnp.int32, (L, LANES), 1) < M_QK_DIM
    top = lax.broadcasted_iota(jnp.int32, (L, LANES), 0) < M_QK_DIM
    top2 = lax.broadcasted_iota(jnp.int32, (L, 2 * M_V_DIM), 0) < M_QK_DIM
    ones_v = jnp.ones((L, M_V_DIM), BF16)
    zero_b = jnp.zeros((), BF16)
    chunk_rows = [slice(c * L, (c + 1) * L) for c in range(TL // L)]

    def row_bcast(x8, h, n=L):
        return jnp.broadcast_to(x8[h:h + 1, :], (n, x8.shape[1]))

    def gate_scan(g_ref):
        gr = g_ref[...] + brow_ref[...]
        lf = jax.nn.log_sigmoid(gr)
        b8 = jnp.concatenate([sum(_dot(part, triu) for part in _split3(lf[:, rows])) for rows in chunk_rows], axis=1)
        b_all = pltpu.roll(b8, M_HEADS, 0)
        r_all = gr - b_all
        cm_all = r_all
        sh = 1
        while sh < L:
            cm_all = jnp.where(lane_in_chunk >= sh, jnp.maximum(cm_all, pltpu.roll(cm_all, sh, 1)), cm_all)
            sh *= 2
        return b_all, r_all, cm_all

    @pl.when(t == 0)
    def _():
        state[...] = jnp.zeros(state.shape, F32)
        m_scr[...] = jnp.zeros(m_scr.shape, F32)
        for i, x in enumerate(gate_scan(grow_ref)):
            gate_scr[i] = x

    b_all, r_all, cm_all = gate_scr[0], gate_scr[1], gate_scr[2]
    next_gates = gate_scan(gnext_ref)

    m = m_scr[...]
    gates = []
    for rows in chunk_rows:
        b, r, cm = b_all[:, rows], r_all[:, rows], cm_all[:, rows]
        rmax = jnp.broadcast_to(cm[:, L - 1:L], (R, L))
        g = jnp.broadcast_to(b[:, L - 1:L], (R, L))
        big_m = jnp.maximum(m, cm)
        inter = jnp.exp(m - big_m)
        emt = jnp.exp(-(b + big_m))
        w = jnp.exp(r - rmax)
        m_loc = g + rmax
        m_new = jnp.maximum(g + m, m_loc)
        decay = jnp.exp(g + m - m_new)
        scale = jnp.exp(m_loc - m_new)
        m = m_new
        gates.append((r, w, decay, scale, big_m, inter, emt))
    m_scr[...] = m

    for rows, (r, w, decay, scale, big_m, inter, emt) in zip(chunk_rows, gates):
        for p in range(M_HEADS // 2):
            pc = slice(p * LANES, (p + 1) * LANES)
            q_pair_b = qk_ref[rows, pc]
            k_pair_b = qk_ref[rows, M_QK_W + p * LANES:M_QK_W + (p + 1) * LANES]
            c_prev = state[p]
            c_prev_b = c_prev.astype(BF16)
            inter_cols = jnp.where(top, row_bcast(inter, 2 * p), row_bcast(inter, 2 * p + 1)).T
            qs_pair = (q_pair_b.astype(F32) * inter_cols).astype(BF16)
            k_both = jnp.concatenate([jnp.where(lo_b, k_pair_b, zero_b), jnp.where(lo_b, zero_b, k_pair_b)], axis=0)
            s_pair = _dot_nt(q_pair_b, k_both)
            v_exts = []
            for hh in range(2):
                hd = 2 * p + hh
                cols = slice(hd * M_V_DIM, (hd + 1) * M_V_DIM)
                sel = lo_b if hh == 0 else jnp.logical_not(lo_b)
                v_ext = jnp.concatenate([mv_ref[rows, cols], ones_v], axis=1)
                v_exts.append(v_ext)
                m_col, emt_col = (row_bcast(x8, hd).T for x8 in (big_m, emt))

                d_mat = jnp.exp(jnp.where(causal, row_bcast(r, hd), -jnp.inf) - m_col)
                s_mat = (s_pair[:, hh * L:(hh + 1) * L] * d_mat).astype(BF16)
                qs = jnp.where(sel, qs_pair, zero_b)
                nd = _dot(jnp.concatenate([s_mat, qs], axis=1), jnp.concatenate([v_ext, c_prev_b], axis=0))
                hcell = nd[:, :M_V_DIM] / jnp.maximum(jnp.abs(nd[:, M_V_DIM:]), emt_col)
                cell = _rms(hcell) * hn_ref[:, cols]
                out_ref[rows, cols] = (so_ref[rows, cols].astype(F32) * cell).astype(BF16)

            k_t = k_pair_b.astype(F32).T
            w_s = jnp.where(top, row_bcast(w, 2 * p), row_bcast(w, 2 * p + 1))
            kw_t = (k_t * w_s).astype(BF16)
            lhs = jnp.concatenate([jnp.where(top, kw_t, zero_b), jnp.where(top, zero_b, kw_t)], axis=1)
            a = _dot(lhs, jnp.concatenate(v_exts, axis=0))
            dec = jnp.where(top2, row_bcast(decay, 2 * p, L)[:, :1], row_bcast(decay, 2 * p + 1, L)[:, :1])
            sc = jnp.where(top2, row_bcast(scale, 2 * p, L)[:, :1], row_bcast(scale, 2 * p + 1, L)[:, :1])
            state[p] = dec * c_prev + sc * a

    for i, x in enumerate(next_gates):
        gate_scr[i] = x


def _mlstm(qk, mv, so, grow, brow, hn, batch, seq):
    tl = MLSTM_TILE
    nt = seq // tl
    cur = lambda b, t: (b * nt + t, 0)
    const = lambda b, t: (0, 0)
    return pl.pallas_call(
        _mlstm_kernel,
        grid=(batch, nt),
        in_specs=[
            pl.BlockSpec((tl, 2 * M_QK_W), cur),
            pl.BlockSpec((tl, M_V_W), cur),
            pl.BlockSpec((tl, M_V_W), cur),
            pl.BlockSpec((2 * M_HEADS, tl), lambda b, t: (0, b * nt + t)),
            pl.BlockSpec((2 * M_HEADS, tl), lambda b, t: (0, b * nt + jnp.minimum(t + 1, nt - 1))),
            pl.BlockSpec(brow.shape, const),
            pl.BlockSpec(hn.shape, const),
        ],
        out_specs=pl.BlockSpec((tl, M_V_W), cur),
        out_shape=jax.ShapeDtypeStruct(mv.shape, BF16),
        scratch_shapes=[
            pltpu.VMEM((M_HEADS // 2, 2 * M_QK_DIM, 2 * M_V_DIM), F32),
            pltpu.VMEM((2 * M_HEADS, M_CHUNK), F32),
            pltpu.VMEM((3, 2 * M_HEADS, tl), F32),
        ],
        compiler_params=pltpu.CompilerParams(dimension_semantics=("parallel", "arbitrary"), vmem_limit_bytes=VMEM_LIMIT),
        name="mlstm",
    )(qk, mv, so, grow, grow, brow, hn)


def _ff_chunks():
    return [(c0, min(c0 + FF_CHUNK, D_FF)) for c0 in range(0, D_FF, FF_CHUNK)]


def _mix_ffn_kernel(x_ref, h_ref, a_ref, m_ref, gpost_ref, gpre_ref, gffn_ref, wg_ref, wa_ref, wm_ref, wo_ref,
                    wfi_ref, wfo_ref, o_ref, acc_ref):
    x = x_ref[...]
    h = h_ref[...]
    a = a_ref[...]
    m = m_ref[...]
    parts = []
    for c0 in range(0, D_MODEL, MERGE_CHUNK):
        cols = slice(c0, c0 + MERGE_CHUNK)
        gcols = slice(D_MODEL + c0, D_MODEL + c0 + MERGE_CHUNK)
        g_attn = jax.nn.sigmoid(_dot(h, wg_ref[:, cols]))
        g_mlstm = jax.nn.sigmoid(_dot(h, wg_ref[:, gcols]))
        parts.append((g_attn * _dot(a, wa_ref[:, cols]) + g_mlstm * _dot(m, wm_ref[:, cols])).astype(BF16))
    y = _dot(jnp.concatenate(parts, axis=1), wo_ref[...])
    x1 = x + _rms(y) * gpost_ref[...]

    h2 = (_rms(x1) * gpre_ref[...]).astype(BF16)
    for c0, c1 in _ff_chunks():
        gate = _dot(h2, wfi_ref[:, c0:c1])
        up = _dot(h2, wfi_ref[:, D_FF + c0:D_FF + c1])
        act = (gate * jax.nn.sigmoid(gate) * up).astype(BF16)
        part = _dot(act, wfo_ref[c0:c1, :])
        if c0 == 0:
            acc_ref[...] = part
        else:
            acc_ref[...] += part
    o_ref[...] = x1 + _rms(acc_ref[...]) * gffn_ref[...]


def _mix_ffn(x2, h, attn, ml, gpost, gpre, gffn, wg, wa, wm, wo, wfi, wfo):
    T = x2.shape[0]
    tm = TOK_TILE
    row = lambda i: (i, 0)
    const = lambda i: (0, 0)

    def wspec(w):
        return pl.BlockSpec(w.shape, const, pipeline_mode=pl.Buffered(1))

    gain = pl.BlockSpec((1, D_MODEL), const)
    return pl.pallas_call(
        _mix_ffn_kernel,
        grid=(T // tm,),
        in_specs=[
            pl.BlockSpec((tm, D_MODEL), row),
            pl.BlockSpec((tm, D_MODEL), row),
            pl.BlockSpec((tm, ATTN_Q_W), row),
            pl.BlockSpec((tm, M_V_W), row),
            gain, gain, gain,
            wspec(wg), wspec(wa), wspec(wm), wspec(wo), wspec(wfi), wspec(wfo),
        ],
        out_specs=pl.BlockSpec((tm, D_MODEL), row),
        out_shape=jax.ShapeDtypeStruct(x2.shape, F32),
        scratch_shapes=[pltpu.VMEM((tm, D_MODEL), F32)],
        compiler_params=pltpu.CompilerParams(dimension_semantics=("parallel",), vmem_limit_bytes=VMEM_LIMIT),
        name="mix_ffn",
    )(x2, h, attn, ml, gpost, gpre, gffn, wg, wa, wm, wo, wfi, wfo)


def _rope_tables(seq):
    f32 = np.float32
    inv_freq = (f32(ROPE_THETA) ** (-np.arange(0, HEAD_DIM, 2, dtype=f32) / f32(HEAD_DIM))).astype(f32)
    ang = np.arange(seq).astype(f32)[:, None] * inv_freq[None, :]
    emb = np.concatenate([ang, ang], axis=-1)
    cos = np.cos(emb).astype(f32)
    sin = np.sin(emb).astype(f32)
    first_half = np.arange(HEAD_DIM) < HEAD_DIM // 2
    sin_a = np.where(first_half, -sin, f32(0))
    sin_b = np.where(first_half, f32(0), sin)
    rep = LANES // HEAD_DIM
    return tuple(jnp.asarray(np.tile(t, (1, rep))) for t in (cos, sin_a, sin_b))


def _layer(x2, batch, seq, norm_pre_mix, norm_post_mix, norm_pre_ffn, norm_post_ffn, w_in, attn_sinks, conv_w,
           conv_b, b_igate, b_fgate, mlstm_head_norm, w_attn_branch, w_mlstm_branch, w_out, w_ffn_in, w_ffn_out):
    w_in_t = jnp.transpose(w_in)
    w_mix = _w_in_columns(w_in_t, 0, O_GATES + LANES)
    wg = _w_in_columns(w_in_t, O_BRANCH, 2 * D_MODEL)
    cos, sa, sb = _rope_tables(seq)

    h, q, k, v, qk, mv, so, grow = _inproj(
        x2, norm_pre_mix[None, :], cos, sa, sb, conv_w, conv_b[None, :], w_mix, seq)
    attn = _attention(attn_sinks, q, k, v, batch, seq)
    bias = jnp.concatenate([b_igate, b_fgate])
    ml = _mlstm(qk, mv, so, grow, bias[:, None], mlstm_head_norm[None, :], batch, seq)
    return _mix_ffn(x2, h, attn, ml, norm_post_mix[None, :], norm_pre_ffn[None, :],
                    norm_post_ffn[None, :], wg, w_attn_branch.astype(BF16), w_mlstm_branch.astype(BF16),
                    w_out.astype(BF16), w_ffn_in.astype(BF16), w_ffn_out.astype(BF16))


def kernel(x, norm_pre_mix, norm_post_mix, norm_pre_ffn, norm_post_ffn, w_in, attn_sinks, conv_w, conv_b, b_igate,
           b_fgate, mlstm_head_norm, w_attn_branch, w_mlstm_branch, w_out, w_ffn_in, w_ffn_out):
    B, S, D = x.shape
    assert D == D_MODEL and WINDOW == ATTN_BLOCK == M_CHUNK
    assert S % max(INPROJ_TILE, ATTN_TILE, MLSTM_TILE) == 0 and (B * S) % TOK_TILE == 0
    x2 = x.reshape(B * S, D)
    for l in range(w_in.shape[0]):
        x2 = _layer(x2, B, S, norm_pre_mix[l], norm_post_mix[l], norm_pre_ffn[l], norm_post_ffn[l], w_in[l],
                    attn_sinks[l], conv_w[l], conv_b[l], b_igate[l], b_fgate[l], mlstm_head_norm[l],
                    w_attn_branch[l], w_mlstm_branch[l], w_out[l], w_ffn_in[l], w_ffn_out[l])
    return x2.reshape(B, S, D)
```

```python
import functools

import numpy as np

import jax
import jax.numpy as jnp
from jax import lax
from jax.experimental import pallas as pl
from jax.experimental.pallas import tpu as pltpu

D_MODEL = 1024
HEAD_DIM = 64
N_Q_HEADS = 8
N_KV_HEADS = 2
WINDOW = 128
ATTN_BLOCK = 128
ROPE_THETA = 10000.0
M_HEADS = 4
M_QK_DIM = 64
M_V_DIM = 128
M_CHUNK = 128
CONV_WIDTH = 4
D_FF = 2816
EPS = 1e-6

ATTN_Q_W = N_Q_HEADS * HEAD_DIM
ATTN_KV_W = N_KV_HEADS * HEAD_DIM
M_QK_W = M_HEADS * M_QK_DIM
M_V_W = M_HEADS * M_V_DIM
KV_BD_W = 2 * ATTN_KV_W * 2
O_MQK = ATTN_Q_W + 2 * ATTN_KV_W
O_MV = O_MQK + 2 * M_QK_W
O_GATES = O_MV + 2 * M_V_W
O_BRANCH = O_GATES + 2 * M_HEADS

LANES = 128
SUBLANES = 8
VMEM_LIMIT = 56 * 1024 * 1024
CONV_HALO = SUBLANES

TOK_TILE = 512
INPROJ_TILE = 1024
ATTN_TILE = 2048
MLSTM_TILE = 2048
FF_CHUNK = 256
MERGE_CHUNK = 256
W_COLS_PER_STEP = 1152

F32 = jnp.float32
BF16 = jnp.bfloat16


def _dot(a, b):
    return jnp.dot(a, b, preferred_element_type=F32)


def _dot_nt(a, b):
    return lax.dot_general(a, b, (((1,), (1,)), ((), ())), preferred_element_type=F32)


def _rms(x):
    return x * lax.rsqrt(jnp.mean(x * x, axis=-1, keepdims=True) + EPS)


def _split3(x):
    hi = x.astype(BF16)
    r1 = x - hi.astype(F32)
    mid = r1.astype(BF16)
    lo = (r1 - mid.astype(F32)).astype(BF16)
    return hi, mid, lo


def _transpose_cast_kernel(wt_ref, o_ref):
    o_ref[...] = wt_ref[...].T.astype(BF16)


def _w_in_columns(w_in_t, col0, n_cols):
    k = w_in_t.shape[1]
    step = max(c for c in range(LANES, n_cols + 1, LANES) if n_cols % c == 0 and (c <= W_COLS_PER_STEP or c == n_cols))
    return pl.pallas_call(
        _transpose_cast_kernel,
        grid=(n_cols // step,),
        in_specs=[pl.BlockSpec((pl.Element(step), pl.Element(k)),
                               lambda j: (pl.multiple_of(col0 + j * step, SUBLANES), 0))],
        out_specs=pl.BlockSpec((k, step), lambda j: (0, j)),
        out_shape=jax.ShapeDtypeStruct((k, n_cols), BF16),
        compiler_params=pltpu.CompilerParams(dimension_semantics=("parallel",), vmem_limit_bytes=VMEM_LIMIT),
        name="w_in_columns",
    )(w_in_t)


def _inproj_kernel(x_ref, g_ref, cos_ref, sa_ref, sb_ref, cw_ref, cb_ref, w_ref,
                   h_ref, q_ref, k_ref, v_ref, qk_ref, mv_ref, so_ref, grow_ref, cbuf, *, n_seq_tiles):
    tm = x_ref.shape[0]
    h = (_rms(x_ref[...]) * g_ref[...]).astype(BF16)
    h_ref[...] = h
    cos = cos_ref[...]
    sa = sa_ref[...]
    sb = sb_ref[...]
    lane = lax.broadcasted_iota(jnp.int32, cos.shape, 1)
    lo = lane < HEAD_DIM

    def rope(t):
        return t * cos + pltpu.roll(t, LANES - HEAD_DIM // 2, 1) * sa + pltpu.roll(t, HEAD_DIM // 2, 1) * sb

    def store_block_diag(ref, t):
        tb = t.astype(BF16)
        rb = pltpu.roll(t, HEAD_DIM, 1).astype(BF16)
        zero = jnp.zeros((), BF16)
        for i, (src, keep_lo) in enumerate(((tb, True), (rb, False), (rb, True), (tb, False))):
            ref[:, i * LANES:(i + 1) * LANES] = jnp.where(lo, src, zero) if keep_lo else jnp.where(lo, zero, src)

    def attn_proj():
        a = _dot(h, w_ref[:, :O_MQK])
        for j in range(ATTN_Q_W // LANES):
            q_ref[:, j * LANES:(j + 1) * LANES] = (
                rope(a[:, j * LANES:(j + 1) * LANES]) * (HEAD_DIM ** -0.5)).astype(BF16)
        store_block_diag(k_ref, rope(a[:, ATTN_Q_W:ATTN_Q_W + LANES]))
        store_block_diag(v_ref, a[:, ATTN_Q_W + LANES:ATTN_Q_W + 2 * LANES])

    @pl.when(pl.program_id(0) % n_seq_tiles == 0)
    def _():
        cbuf[0:CONV_HALO, :] = jnp.zeros((CONV_HALO, cbuf.shape[1]), F32)

    def mqk_proj(c0, c1):
        cbuf[CONV_HALO:CONV_HALO + tm, c0:c1] = _dot(h, w_ref[:, O_MQK + c0:O_MQK + c1])

    def conv_chunk(j0):
        cols = slice(j0 * LANES, (j0 + 1) * LANES)
        y = cb_ref[:, cols]
        for j in range(CONV_WIDTH):
            off = CONV_HALO - (CONV_WIDTH - 1) + j
            y = y + cw_ref[j:j + 1, cols] * cbuf[off:off + tm, cols]
        cbuf[0:CONV_HALO, cols] = cbuf[tm:tm + CONV_HALO, cols]
        y = y * jax.nn.sigmoid(y)
        if j0 * LANES >= M_QK_W:
            y = y * (M_QK_DIM ** -0.5)
        qk_ref[:, cols] = y.astype(BF16)

    def mlstm_vo_proj():
        vo = _dot(h, w_ref[:, O_MV:O_GATES + LANES])
        mv_ref[...] = vo[:, :M_V_W].astype(BF16)
        so_ref[...] = jax.nn.sigmoid(vo[:, M_V_W:2 * M_V_W]).astype(BF16)
        grow_ref[...] = vo[:, 2 * M_V_W:].T[:2 * M_HEADS, :]

    mqk_proj(0, M_QK_W)
    conv_chunk(0)
    conv_chunk(1)
    attn_proj()
    mqk_proj(M_QK_W, 2 * M_QK_W)
    conv_chunk(2)
    conv_chunk(3)
    mlstm_vo_proj()


def _inproj(x2, g, cos, sa, sb, cw, cb, w, seq):
    T = x2.shape[0]
    tm = INPROJ_TILE
    n_seq_tiles = seq // tm
    row = lambda i: (i, 0)
    pos = lambda i: (i % n_seq_tiles, 0)
    const = lambda i: (0, 0)

    def wspec(w):
        return pl.BlockSpec(w.shape, const, pipeline_mode=pl.Buffered(1))

    out_shape = (
        jax.ShapeDtypeStruct((T, D_MODEL), BF16),
        jax.ShapeDtypeStruct((T, ATTN_Q_W), BF16),
        jax.ShapeDtypeStruct((T, KV_BD_W), BF16),
        jax.ShapeDtypeStruct((T, KV_BD_W), BF16),
        jax.ShapeDtypeStruct((T, 2 * M_QK_W), BF16),
        jax.ShapeDtypeStruct((T, M_V_W), BF16),
        jax.ShapeDtypeStruct((T, M_V_W), BF16),
        jax.ShapeDtypeStruct((2 * M_HEADS, T), F32),
    )
    out_specs = (
        pl.BlockSpec((tm, D_MODEL), row),
        pl.BlockSpec((tm, ATTN_Q_W), row),
        pl.BlockSpec((tm, KV_BD_W), row),
        pl.BlockSpec((tm, KV_BD_W), row),
        pl.BlockSpec((tm, 2 * M_QK_W), row),
        pl.BlockSpec((tm, M_V_W), row),
        pl.BlockSpec((tm, M_V_W), row),
        pl.BlockSpec((2 * M_HEADS, tm), lambda i: (0, i)),
    )
    in_specs = [
        pl.BlockSpec((tm, D_MODEL), row),
        pl.BlockSpec((1, D_MODEL), const),
        pl.BlockSpec((tm, LANES), pos),
        pl.BlockSpec((tm, LANES), pos),
        pl.BlockSpec((tm, LANES), pos),
        pl.BlockSpec(cw.shape, const),
        pl.BlockSpec(cb.shape, const),
        wspec(w),
    ]
    return pl.pallas_call(
        functools.partial(_inproj_kernel, n_seq_tiles=n_seq_tiles),
        grid=(T // tm,),
        in_specs=in_specs,
        out_specs=out_specs,
        out_shape=out_shape,
        scratch_shapes=[pltpu.VMEM((CONV_HALO + tm, 2 * M_QK_W), F32)],
        compiler_params=pltpu.CompilerParams(dimension_semantics=("arbitrary",), vmem_limit_bytes=VMEM_LIMIT),
        name="inproj",
    )(x2, g, cos, sa, sb, cw, cb, w)


def _attn_kernel(sink_ref, q_ref, kc_ref, kp_ref, vc_ref, vp_ref, o_ref):
    t = pl.program_id(1)
    blk = ATTN_BLOCK
    n_blk = q_ref.shape[0] // blk
    lo_q = lax.broadcasted_iota(jnp.int32, (blk, LANES), 1) < HEAD_DIM
    ri = lax.broadcasted_iota(jnp.int32, (blk, blk), 0)
    ci = lax.broadcasted_iota(jnp.int32, (blk, blk), 1)
    own = ci <= ri
    prev_ok = ci > ri + jnp.where(t > 0, 0, blk)
    zero = jnp.zeros((), BF16)
    row_is_a = lax.broadcasted_iota(jnp.int32, (4 * blk, LANES), 0) < 2 * blk
    lane_is_a = lax.broadcasted_iota(jnp.int32, (4 * blk, LANES), 1) < HEAD_DIM
    ones_bd = (row_is_a == lane_is_a).astype(BF16)

    for n in range(n_blk):
        rows = slice(n * blk, (n + 1) * blk)
        k_prev = kp_ref if n == 0 else kc_ref.at[(n - 1) * blk:n * blk, :]
        v_prev = vp_ref if n == 0 else vc_ref.at[(n - 1) * blk:n * blk, :]
        for j in range(N_KV_HEADS):
            kv_cols = [slice((2 * j + i) * LANES, (2 * j + i + 1) * LANES) for i in range(2)]
            k_bd = jnp.concatenate([x for c in kv_cols for x in (k_prev[:, c], kc_ref[rows, c])], axis=0)
            v_bd = jnp.concatenate([x for c in kv_cols for x in (v_prev[:, c], vc_ref[rows, c])], axis=0)
            c0 = 2 * j * LANES
            q2 = jnp.concatenate([q_ref[rows, c0:c0 + LANES], q_ref[rows, c0 + LANES:c0 + 2 * LANES]], axis=0)
            s = _dot_nt(q2, k_bd)
            e_rows, sink_rows = [], []
            for p in range(2):
                e_cols, sink_cols = [], []
                for hh in range(2):
                    sink = sink_ref[4 * j + 2 * p + hh]
                    s_prev = s[p * blk:(p + 1) * blk, 2 * hh * blk:(2 * hh + 1) * blk]
                    s_own = s[p * blk:(p + 1) * blk, (2 * hh + 1) * blk:(2 * hh + 2) * blk]
                    if n == 0:
                        s_prev = jnp.where(prev_ok, s_prev, -jnp.inf)
                    sc = jnp.where(own, s_own, s_prev)
                    m = jnp.maximum(jnp.max(sc, axis=-1, keepdims=True), sink)
                    eb = jnp.exp(sc - m).astype(BF16)
                    e_cols += [jnp.where(own, zero, eb), jnp.where(own, eb, zero)]
                    sink_cols.append(jnp.exp(sink - m))
                e_rows.append(jnp.concatenate(e_cols, axis=1))
                sink_rows.append(jnp.where(lo_q, sink_cols[0], sink_cols[1]))
            o = _dot(jnp.concatenate(e_rows, axis=0), jnp.concatenate([v_bd, ones_bd], axis=1))
            for p in range(2):
                op = o[p * blk:(p + 1) * blk]
                o_ref[rows, c0 + p * LANES:c0 + (p + 1) * LANES] = (
                    op[:, :LANES] / (op[:, LANES:] + sink_rows[p])).astype(BF16)


def _attention(sinks, q, k, v, batch, seq):
    tq = ATTN_TILE
    nt = seq // tq
    per = tq // ATTN_BLOCK
    cur = lambda b, t: (b * nt + t, 0)
    prev = lambda b, t: (jnp.maximum((b * nt + t) * per - 1, 0), 0)
    return pl.pallas_call(
        _attn_kernel,
        grid=(batch, nt),
        in_specs=[
            pl.BlockSpec(memory_space=pltpu.SMEM),
            pl.BlockSpec((tq, ATTN_Q_W), cur),
            pl.BlockSpec((tq, KV_BD_W), cur),
            pl.BlockSpec((ATTN_BLOCK, KV_BD_W), prev),
            pl.BlockSpec((tq, KV_BD_W), cur),
            pl.BlockSpec((ATTN_BLOCK, KV_BD_W), prev),
        ],
        out_specs=pl.BlockSpec((tq, ATTN_Q_W), cur),
        out_shape=jax.ShapeDtypeStruct(q.shape, BF16),
        compiler_params=pltpu.CompilerParams(dimension_semantics=("parallel", "parallel"), vmem_limit_bytes=VMEM_LIMIT),
        name="swa_attention",
    )(sinks, q, k, k, v, v)


def _mlstm_kernel(qk_ref, mv_ref, so_ref, grow_ref, gnext_ref, brow_ref, hn_ref, out_ref, state, m_scr, gate_scr):
    t = pl.program_id(1)
    TL = qk_ref.shape[0]
    L = M_CHUNK
    R = 2 * M_HEADS

    ri = lax.broadcasted_iota(jnp.int32, (L, L), 0)
    ci = lax.broadcasted_iota(jnp.int32, (L, L), 1)
    causal = ci <= ri
    triu = (ri <= ci).astype(BF16)
    lane_in_chunk = lax.broadcasted_iota(jnp.int32, (R, TL), 1) & (L - 1)
    lo_b = lax.broadcasted_iota(jnp.int32, (L, LANES), 1) < M_QK_DIM
    top = lax.broadcasted_iota(jnp.int32, (L, LANES), 0) < M_QK_DIM
    top2 = lax.broadcasted_iota(jnp.int32, (L, 2 * M_V_DIM), 0) < M_QK_DIM
    ones_v = jnp.ones((L, M_V_DIM), BF16)
    zero_b = jnp.zeros((), BF16)
    chunk_rows = [slice(c * L, (c + 1) * L) for c in range(TL // L)]

    def row_bcast(x8, h, n=L):
        return jnp.broadcast_to(x8[h:h + 1, :], (n, x8.shape[1]))

    def gate_scan(g_ref):
        gr = g_ref[...] + brow_ref[...]
        lf = jax.nn.log_sigmoid(gr)
        stacked = jnp.concatenate([part[:, rows] for part in _split3(lf) for rows in chunk_rows], axis=0)
        sums = _dot(stacked, triu)
        n_c = len(chunk_rows)
        b8 = jnp.concatenate([sum(sums[(i * n_c + c) * R:(i * n_c + c + 1) * R] for i in range(3)) for c in range(n_c)],
                             axis=1)
        b_all = pltpu.roll(b8, M_HEADS, 0)
        r_all = gr - b_all
        cm_all = r_all
        sh = 1
        while sh < L:
            cm_all = jnp.where(lane_in_chunk >= sh, jnp.maximum(cm_all, pltpu.roll(cm_all, sh, 1)), cm_all)
            sh *= 2
        return b_all, r_all, cm_all

    @pl.when(t == 0)
    def _():
        state[...] = jnp.zeros(state.shape, F32)
        m_scr[...] = jnp.zeros(m_scr.shape, F32)
        for i, x in enumerate(gate_scan(grow_ref)):
            gate_scr[i] = x

    b_all, r_all, cm_all = gate_scr[0], gate_scr[1], gate_scr[2]
    next_gates = gate_scan(gnext_ref)

    m = m_scr[...]
    gates = []
    for rows in chunk_rows:
        b, r, cm = b_all[:, rows], r_all[:, rows], cm_all[:, rows]
        rmax = jnp.broadcast_to(cm[:, L - 1:L], (R, L))
        g = jnp.broadcast_to(b[:, L - 1:L], (R, L))
        big_m = jnp.maximum(m, cm)
        inter = jnp.exp(m - big_m)
        emt = jnp.exp(-(b + big_m))
        w = jnp.exp(r - rmax)
        m_loc = g + rmax
        m_new = jnp.maximum(g + m, m_loc)
        decay = jnp.exp(g + m - m_new)
        scale = jnp.exp(m_loc - m_new)
        m = m_new
        gates.append((r, w, decay, scale, big_m, inter, emt))
    m_scr[...] = m

    for rows, (r, w, decay, scale, big_m, inter, emt) in zip(chunk_rows, gates):
        for p in range(M_HEADS // 2):
            pc = slice(p * LANES, (p + 1) * LANES)
            q_pair_b = qk_ref[rows, pc]
            k_pair_b = qk_ref[rows, M_QK_W + p * LANES:M_QK_W + (p + 1) * LANES]
            c_prev = state[p]
            c_prev_b = c_prev.astype(BF16)
            inter_cols = jnp.where(top, row_bcast(inter, 2 * p), row_bcast(inter, 2 * p + 1)).T
            qs_pair = (q_pair_b.astype(F32) * inter_cols).astype(BF16)
            k_both = jnp.concatenate([jnp.where(lo_b, k_pair_b, zero_b), jnp.where(lo_b, zero_b, k_pair_b)], axis=0)
            s_pair = _dot_nt(q_pair_b, k_both)
            v_exts = []
            for hh in range(2):
                hd = 2 * p + hh
                cols = slice(hd * M_V_DIM, (hd + 1) * M_V_DIM)
                sel = lo_b if hh == 0 else jnp.logical_not(lo_b)
                v_ext = jnp.concatenate([mv_ref[rows, cols], ones_v], axis=1)
                v_exts.append(v_ext)
                m_col, emt_col = (row_bcast(x8, hd).T for x8 in (big_m, emt))

                d_mat = jnp.exp(jnp.where(causal, row_bcast(r, hd), -jnp.inf) - m_col)
                s_mat = (s_pair[:, hh * L:(hh + 1) * L] * d_mat).astype(BF16)
                qs = jnp.where(sel, qs_pair, zero_b)
                nd = _dot(jnp.concatenate([s_mat, qs], axis=1), jnp.concatenate([v_ext, c_prev_b], axis=0))
                hcell = nd[:, :M_V_DIM] / jnp.maximum(jnp.abs(nd[:, M_V_DIM:]), emt_col)
                cell = _rms(hcell) * hn_ref[:, cols]
                out_ref[rows, cols] = (so_ref[rows, cols].astype(F32) * cell).astype(BF16)

            k_t = k_pair_b.astype(F32).T
            w_s = jnp.where(top, row_bcast(w, 2 * p), row_bcast(w, 2 * p + 1))
            kw_t = (k_t * w_s).astype(BF16)
            lhs = jnp.concatenate([jnp.where(top, kw_t, zero_b), jnp.where(top, zero_b, kw_t)], axis=1)
            a = _dot(lhs, jnp.concatenate(v_exts, axis=0))
            dec = jnp.where(top2, row_bcast(decay, 2 * p, L)[:, :1], row_bcast(decay, 2 * p + 1, L)[:, :1])
            sc = jnp.where(top2, row_bcast(scale, 2 * p, L)[:, :1], row_bcast(scale, 2 * p + 1, L)[:, :1])
            state[p] = dec * c_prev + sc * a

    for i, x in enumerate(next_gates):
        gate_scr[i] = x


def _mlstm(qk, mv, so, grow, brow, hn, batch, seq):
    tl = MLSTM_TILE
    nt = seq // tl
    cur = lambda b, t: (b * nt + t, 0)
    const = lambda b, t: (0, 0)
    return pl.pallas_call(
        _mlstm_kernel,
        grid=(batch, nt),
        in_specs=[
            pl.BlockSpec((tl, 2 * M_QK_W), cur),
            pl.BlockSpec((tl, M_V_W), cur),
            pl.BlockSpec((tl, M_V_W), cur),
            pl.BlockSpec((2 * M_HEADS, tl), lambda b, t: (0, b * nt + t)),
            pl.BlockSpec((2 * M_HEADS, tl), lambda b, t: (0, b * nt + jnp.minimum(t + 1, nt - 1))),
            pl.BlockSpec(brow.shape, const),
            pl.BlockSpec(hn.shape, const),
        ],
        out_specs=pl.BlockSpec((tl, M_V_W), cur),
        out_shape=jax.ShapeDtypeStruct(mv.shape, BF16),
        scratch_shapes=[
            pltpu.VMEM((M_HEADS // 2, 2 * M_QK_DIM, 2 * M_V_DIM), F32),
            pltpu.VMEM((2 * M_HEADS, M_CHUNK), F32),
            pltpu.VMEM((3, 2 * M_HEADS, tl), F32),
        ],
        compiler_params=pltpu.CompilerParams(dimension_semantics=("parallel", "arbitrary"), vmem_limit_bytes=VMEM_LIMIT),
        name="mlstm",
    )(qk, mv, so, grow, grow, brow, hn)


def _ff_chunks():
    return [(c0, min(c0 + FF_CHUNK, D_FF)) for c0 in range(0, D_FF, FF_CHUNK)]


def _mix_ffn_kernel(x_ref, h_ref, a_ref, m_ref, gpost_ref, gpre_ref, gffn_ref, wg_ref, wa_ref, wm_ref, wo_ref,
                    wfi_ref, wfo_ref, o_ref, acc_ref):
    x = x_ref[...]
    h = h_ref[...]
    a = a_ref[...]
    m = m_ref[...]
    parts = []
    for c0 in range(0, D_MODEL, MERGE_CHUNK):
        cols = slice(c0, c0 + MERGE_CHUNK)
        gcols = slice(D_MODEL + c0, D_MODEL + c0 + MERGE_CHUNK)
        g_attn = jax.nn.sigmoid(_dot(h, wg_ref[:, cols]))
        g_mlstm = jax.nn.sigmoid(_dot(h, wg_ref[:, gcols]))
        parts.append((g_attn * _dot(a, wa_ref[:, cols]) + g_mlstm * _dot(m, wm_ref[:, cols])).astype(BF16))
    y = _dot(jnp.concatenate(parts, axis=1), wo_ref[...])
    x1 = x + _rms(y) * gpost_ref[...]

    h2 = (_rms(x1) * gpre_ref[...]).astype(BF16)
    for c0, c1 in _ff_chunks():
        gate = _dot(h2, wfi_ref[:, c0:c1])
        up = _dot(h2, wfi_ref[:, D_FF + c0:D_FF + c1])
        act = (gate * jax.nn.sigmoid(gate) * up).astype(BF16)
        part = _dot(act, wfo_ref[c0:c1, :])
        if c0 == 0:
            acc_ref[...] = part
        else:
            acc_ref[...] += part
    o_ref[...] = x1 + _rms(acc_ref[...]) * gffn_ref[...]


def _mix_ffn(x2, h, attn, ml, gpost, gpre, gffn, wg, wa, wm, wo, wfi, wfo):
    T = x2.shape[0]
    tm = TOK_TILE
    row = lambda i: (i, 0)
    const = lambda i: (0, 0)

    def wspec(w):
        return pl.BlockSpec(w.shape, const, pipeline_mode=pl.Buffered(1))

    gain = pl.BlockSpec((1, D_MODEL), const)
    return pl.pallas_call(
        _mix_ffn_kernel,
        grid=(T // tm,),
        in_specs=[
            pl.BlockSpec((tm, D_MODEL), row),
            pl.BlockSpec((tm, D_MODEL), row),
            pl.BlockSpec((tm, ATTN_Q_W), row),
            pl.BlockSpec((tm, M_V_W), row),
            gain, gain, gain,
            wspec(wg), wspec(wa), wspec(wm), wspec(wo), wspec(wfi), wspec(wfo),
        ],
        out_specs=pl.BlockSpec((tm, D_MODEL), row),
        out_shape=jax.ShapeDtypeStruct(x2.shape, F32),
        scratch_shapes=[pltpu.VMEM((tm, D_MODEL), F32)],
        compiler_params=pltpu.CompilerParams(dimension_semantics=("parallel",), vmem_limit_bytes=VMEM_LIMIT),
        name="mix_ffn",
    )(x2, h, attn, ml, gpost, gpre, gffn, wg, wa, wm, wo, wfi, wfo)


def _rope_tables(seq):
    f32 = np.float32
    inv_freq = (f32(ROPE_THETA) ** (-np.arange(0, HEAD_DIM, 2, dtype=f32) / f32(HEAD_DIM))).astype(f32)
    ang = np.arange(seq).astype(f32)[:, None] * inv_freq[None, :]
    emb = np.concatenate([ang, ang], axis=-1)
    cos = np.cos(emb).astype(f32)
    sin = np.sin(emb).astype(f32)
    first_half = np.arange(HEAD_DIM) < HEAD_DIM // 2
    sin_a = np.where(first_half, -sin, f32(0))
    sin_b = np.where(first_half, f32(0), sin)
    rep = LANES // HEAD_DIM
    return tuple(jnp.asarray(np.tile(t, (1, rep))) for t in (cos, sin_a, sin_b))


def _layer(x2, batch, seq, norm_pre_mix, norm_post_mix, norm_pre_ffn, norm_post_ffn, w_in, attn_sinks, conv_w,
           conv_b, b_igate, b_fgate, mlstm_head_norm, w_attn_branch, w_mlstm_branch, w_out, w_ffn_in, w_ffn_out):
    w_in_t = jnp.transpose(w_in)
    w_mix = _w_in_columns(w_in_t, 0, O_GATES + LANES)
    wg = _w_in_columns(w_in_t, O_BRANCH, 2 * D_MODEL)
    cos, sa, sb = _rope_tables(seq)

    h, q, k, v, qk, mv, so, grow = _inproj(
        x2, norm_pre_mix[None, :], cos, sa, sb, conv_w, conv_b[None, :], w_mix, seq)
    attn = _attention(attn_sinks, q, k, v, batch, seq)
    bias = jnp.concatenate([b_igate, b_fgate])
    ml = _mlstm(qk, mv, so, grow, bias[:, None], mlstm_head_norm[None, :], batch, seq)
    return _mix_ffn(x2, h, attn, ml, norm_post_mix[None, :], norm_pre_ffn[None, :],
                    norm_post_ffn[None, :], wg, w_attn_branch.astype(BF16), w_mlstm_branch.astype(BF16),
                    w_out.astype(BF16), w_ffn_in.astype(BF16), w_ffn_out.astype(BF16))


def kernel(x, norm_pre_mix, norm_post_mix, norm_pre_ffn, norm_post_ffn, w_in, attn_sinks, conv_w, conv_b, b_igate,
           b_fgate, mlstm_head_norm, w_attn_branch, w_mlstm_branch, w_out, w_ffn_in, w_ffn_out):
    B, S, D = x.shape
    assert D == D_MODEL and WINDOW == ATTN_BLOCK == M_CHUNK
    assert S % max(INPROJ_TILE, ATTN_TILE, MLSTM_TILE) == 0 and (B * S) % TOK_TILE == 0
    x2 = x.reshape(B * S, D)
    for l in range(w_in.shape[0]):
        x2 = _layer(x2, B, S, norm_pre_mix[l], norm_post_mix[l], norm_pre_ffn[l], norm_post_ffn[l], w_in[l],
                    attn_sinks[l], conv_w[l], conv_b[l], b_igate[l], b_fgate[l], mlstm_head_norm[l],
                    w_attn_branch[l], w_mlstm_branch[l], w_out[l], w_ffn_in[l], w_ffn_out[l])
    return x2.reshape(B, S, D)
```

```python
import functools

import numpy as np

import jax
import jax.numpy as jnp
from jax import lax
from jax.experimental import pallas as pl
from jax.experimental.pallas import tpu as pltpu

D_MODEL = 1024
HEAD_DIM = 64
N_Q_HEADS = 8
N_KV_HEADS = 2
WINDOW = 128
ATTN_BLOCK = 128
ROPE_THETA = 10000.0
M_HEADS = 4
M_QK_DIM = 64
M_V_DIM = 128
M_CHUNK = 128
CONV_WIDTH = 4
D_FF = 2816
EPS = 1e-6

ATTN_Q_W = N_Q_HEADS * HEAD_DIM
ATTN_KV_W = N_KV_HEADS * HEAD_DIM
M_QK_W = M_HEADS * M_QK_DIM
M_V_W = M_HEADS * M_V_DIM
KV_BD_W = 2 * ATTN_KV_W * 2
O_MQK = ATTN_Q_W + 2 * ATTN_KV_W
O_MV = O_MQK + 2 * M_QK_W
O_GATES = O_MV + 2 * M_V_W
O_BRANCH = O_GATES + 2 * M_HEADS

LANES = 128
SUBLANES = 8
VMEM_LIMIT = 56 * 1024 * 1024
CONV_HALO = SUBLANES

TOK_TILE = 512
INPROJ_TILE = 1024
ATTN_TILE = 2048
MLSTM_TILE = 2048
FF_CHUNK = 256
MERGE_CHUNK = 256
W_COLS_PER_STEP = 1152

F32 = jnp.float32
BF16 = jnp.bfloat16


def _dot(a, b):
    return jnp.dot(a, b, preferred_element_type=F32)


def _dot_nt(a, b):
    return lax.dot_general(a, b, (((1,), (1,)), ((), ())), preferred_element_type=F32)


def _rms(x):
    return x * lax.rsqrt(jnp.mean(x * x, axis=-1, keepdims=True) + EPS)


def _split3(x):
    hi = x.astype(BF16)
    r1 = x - hi.astype(F32)
    mid = r1.astype(BF16)
    lo = (r1 - mid.astype(F32)).astype(BF16)
    return hi, mid, lo


def _transpose_cast_kernel(wt_ref, o_ref):
    o_ref[...] = wt_ref[...].T.astype(BF16)


def _w_in_columns(w_in_t, col0, n_cols):
    k = w_in_t.shape[1]
    step = max(c for c in range(LANES, n_cols + 1, LANES) if n_cols % c == 0 and (c <= W_COLS_PER_STEP or c == n_cols))
    return pl.pallas_call(
        _transpose_cast_kernel,
        grid=(n_cols // step,),
        in_specs=[pl.BlockSpec((pl.Element(step), pl.Element(k)),
                               lambda j: (pl.multiple_of(col0 + j * step, SUBLANES), 0))],
        out_specs=pl.BlockSpec((k, step), lambda j: (0, j)),
        out_shape=jax.ShapeDtypeStruct((k, n_cols), BF16),
        compiler_params=pltpu.CompilerParams(dimension_semantics=("parallel",), vmem_limit_bytes=VMEM_LIMIT),
        name="w_in_columns",
    )(w_in_t)


def _inproj_kernel(x_ref, g_ref, cos_ref, sa_ref, sb_ref, cw_ref, cb_ref, w_ref,
                   h_ref, q_ref, k_ref, v_ref, qk_ref, mv_ref, so_ref, grow_ref, cbuf, *, n_seq_tiles):
    tm = x_ref.shape[0]
    h = (_rms(x_ref[...]) * g_ref[...]).astype(BF16)
    h_ref[...] = h
    cos = cos_ref[...]
    sa = sa_ref[...]
    sb = sb_ref[...]
    lane = lax.broadcasted_iota(jnp.int32, cos.shape, 1)
    lo = lane < HEAD_DIM

    def rope(t):
        return t * cos + pltpu.roll(t, LANES - HEAD_DIM // 2, 1) * sa + pltpu.roll(t, HEAD_DIM // 2, 1) * sb

    def store_block_diag(ref, t):
        tb = t.astype(BF16)
        rb = pltpu.roll(t, HEAD_DIM, 1).astype(BF16)
        zero = jnp.zeros((), BF16)
        for i, (src, keep_lo) in enumerate(((tb, True), (rb, False), (rb, True), (tb, False))):
            ref[:, i * LANES:(i + 1) * LANES] = jnp.where(lo, src, zero) if keep_lo else jnp.where(lo, zero, src)

    def attn_proj():
        a = _dot(h, w_ref[:, :O_MQK])
        for j in range(ATTN_Q_W // LANES):
            q_ref[:, j * LANES:(j + 1) * LANES] = (
                rope(a[:, j * LANES:(j + 1) * LANES]) * (HEAD_DIM ** -0.5)).astype(BF16)
        store_block_diag(k_ref, rope(a[:, ATTN_Q_W:ATTN_Q_W + LANES]))
        store_block_diag(v_ref, a[:, ATTN_Q_W + LANES:ATTN_Q_W + 2 * LANES])

    @pl.when(pl.program_id(0) % n_seq_tiles == 0)
    def _():
        cbuf[0:CONV_HALO, :] = jnp.zeros((CONV_HALO, cbuf.shape[1]), F32)

    def mqk_proj(c0, c1):
        cbuf[CONV_HALO:CONV_HALO + tm, c0:c1] = _dot(h, w_ref[:, O_MQK + c0:O_MQK + c1])

    def conv_chunk(j0):
        cols = slice(j0 * LANES, (j0 + 1) * LANES)
        y = cb_ref[:, cols]
        for j in range(CONV_WIDTH):
            off = CONV_HALO - (CONV_WIDTH - 1) + j
            y = y + cw_ref[j:j + 1, cols] * cbuf[off:off + tm, cols]
        cbuf[0:CONV_HALO, cols] = cbuf[tm:tm + CONV_HALO, cols]
        y = y * jax.nn.sigmoid(y)
        if j0 * LANES >= M_QK_W:
            y = y * (M_QK_DIM ** -0.5)
        qk_ref[:, cols] = y.astype(BF16)

    def mlstm_vo_proj():
        vo = _dot(h, w_ref[:, O_MV:O_GATES + LANES])
        mv_ref[...] = vo[:, :M_V_W].astype(BF16)
        so_ref[...] = jax.nn.sigmoid(vo[:, M_V_W:2 * M_V_W]).astype(BF16)
        grow_ref[...] = vo[:, 2 * M_V_W:].T[:2 * M_HEADS, :]

    mqk_proj(0, M_QK_W)
    conv_chunk(0)
    conv_chunk(1)
    attn_proj()
    mqk_proj(M_QK_W, 2 * M_QK_W)
    conv_chunk(2)
    conv_chunk(3)
    mlstm_vo_proj()


def _inproj(x2, g, cos, sa, sb, cw, cb, w, seq):
    T = x2.shape[0]
    tm = INPROJ_TILE
    n_seq_tiles = seq // tm
    row = lambda i: (i, 0)
    pos = lambda i: (i % n_seq_tiles, 0)
    const = lambda i: (0, 0)

    def wspec(w):
        return pl.BlockSpec(w.shape, const, pipeline_mode=pl.Buffered(1))

    out_shape = (
        jax.ShapeDtypeStruct((T, D_MODEL), BF16),
        jax.ShapeDtypeStruct((T, ATTN_Q_W), BF16),
        jax.ShapeDtypeStruct((T, KV_BD_W), BF16),
        jax.ShapeDtypeStruct((T, KV_BD_W), BF16),
        jax.ShapeDtypeStruct((T, 2 * M_QK_W), BF16),
        jax.ShapeDtypeStruct((T, M_V_W), BF16),
        jax.ShapeDtypeStruct((T, M_V_W), BF16),
        jax.ShapeDtypeStruct((2 * M_HEADS, T), F32),
    )
    out_specs = (
        pl.BlockSpec((tm, D_MODEL), row),
        pl.BlockSpec((tm, ATTN_Q_W), row),
        pl.BlockSpec((tm, KV_BD_W), row),
        pl.BlockSpec((tm, KV_BD_W), row),
        pl.BlockSpec((tm, 2 * M_QK_W), row),
        pl.BlockSpec((tm, M_V_W), row),
        pl.BlockSpec((tm, M_V_W), row),
        pl.BlockSpec((2 * M_HEADS, tm), lambda i: (0, i)),
    )
    in_specs = [
        pl.BlockSpec((tm, D_MODEL), row),
        pl.BlockSpec((1, D_MODEL), const),
        pl.BlockSpec((tm, LANES), pos),
        pl.BlockSpec((tm, LANES), pos),
        pl.BlockSpec((tm, LANES), pos),
        pl.BlockSpec(cw.shape, const),
        pl.BlockSpec(cb.shape, const),
        wspec(w),
    ]
    return pl.pallas_call(
        functools.partial(_inproj_kernel, n_seq_tiles=n_seq_tiles),
        grid=(T // tm,),
        in_specs=in_specs,
        out_specs=out_specs,
        out_shape=out_shape,
        scratch_shapes=[pltpu.VMEM((CONV_HALO + tm, 2 * M_QK_W), F32)],
        compiler_params=pltpu.CompilerParams(dimension_semantics=("arbitrary",), vmem_limit_bytes=VMEM_LIMIT),
        name="inproj",
    )(x2, g, cos, sa, sb, cw, cb, w)


def _attn_kernel(sink_ref, q_ref, kc_ref, kp_ref, vc_ref, vp_ref, o_ref):
    t = pl.program_id(1)
    blk = ATTN_BLOCK
    n_blk = q_ref.shape[0] // blk
    lo_q = lax.broadcasted_iota(jnp.int32, (blk, LANES), 1) < HEAD_DIM
    ri = lax.broadcasted_iota(jnp.int32, (blk, blk), 0)
    ci = lax.broadcasted_iota(jnp.int32, (blk, blk), 1)
    own = ci <= ri
    prev_ok = ci > ri + jnp.where(t > 0, 0, blk)
    zero = jnp.zeros((), BF16)
    row_is_a = lax.broadcasted_iota(jnp.int32, (4 * blk, LANES), 0) < 2 * blk
    lane_is_a = lax.broadcasted_iota(jnp.int32, (4 * blk, LANES), 1) < HEAD_DIM
    ones_bd = (row_is_a == lane_is_a).astype(BF16)

    for n in range(n_blk):
        rows = slice(n * blk, (n + 1) * blk)
        k_prev = kp_ref if n == 0 else kc_ref.at[(n - 1) * blk:n * blk, :]
        v_prev = vp_ref if n == 0 else vc_ref.at[(n - 1) * blk:n * blk, :]
        for j in range(N_KV_HEADS):
            kv_cols = [slice((2 * j + i) * LANES, (2 * j + i + 1) * LANES) for i in range(2)]
            k_bd = jnp.concatenate([x for c in kv_cols for x in (k_prev[:, c], kc_ref[rows, c])], axis=0)
            v_bd = jnp.concatenate([x for c in kv_cols for x in (v_prev[:, c], vc_ref[rows, c])], axis=0)
            c0 = 2 * j * LANES
            q2 = jnp.concatenate([q_ref[rows, c0:c0 + LANES], q_ref[rows, c0 + LANES:c0 + 2 * LANES]], axis=0)
            s = _dot_nt(q2, k_bd)
            e_rows, sink_rows = [], []
            for p in range(2):
                e_cols, sink_cols = [], []
                for hh in range(2):
                    sink = sink_ref[4 * j + 2 * p + hh]
                    s_prev = s[p * blk:(p + 1) * blk, 2 * hh * blk:(2 * hh + 1) * blk]
                    s_own = s[p * blk:(p + 1) * blk, (2 * hh + 1) * blk:(2 * hh + 2) * blk]
                    if n == 0:
                        s_prev = jnp.where(prev_ok, s_prev, -jnp.inf)
                    sc = jnp.where(own, s_own, s_prev)
                    m = jnp.maximum(jnp.max(sc, axis=-1, keepdims=True), sink)
                    eb = jnp.exp(sc - m).astype(BF16)
                    e_cols += [jnp.where(own, zero, eb), jnp.where(own, eb, zero)]
                    sink_cols.append(jnp.exp(sink - m))
                e_rows.append(jnp.concatenate(e_cols, axis=1))
                sink_rows.append(jnp.where(lo_q, sink_cols[0], sink_cols[1]))
            o = _dot(jnp.concatenate(e_rows, axis=0), jnp.concatenate([v_bd, ones_bd], axis=1))
            for p in range(2):
                op = o[p * blk:(p + 1) * blk]
                o_ref[rows, c0 + p * LANES:c0 + (p + 1) * LANES] = (
                    op[:, :LANES] / (op[:, LANES:] + sink_rows[p])).astype(BF16)


def _attention(sinks, q, k, v, batch, seq):
    tq = ATTN_TILE
    nt = seq // tq
    per = tq // ATTN_BLOCK
    cur = lambda b, t: (b * nt + t, 0)
    prev = lambda b, t: (jnp.maximum((b * nt + t) * per - 1, 0), 0)
    return pl.pallas_call(
        _attn_kernel,
        grid=(batch, nt),
        in_specs=[
            pl.BlockSpec(memory_space=pltpu.SMEM),
            pl.BlockSpec((tq, ATTN_Q_W), cur),
            pl.BlockSpec((tq, KV_BD_W), cur),
            pl.BlockSpec((ATTN_BLOCK, KV_BD_W), prev),
            pl.BlockSpec((tq, KV_BD_W), cur),
            pl.BlockSpec((ATTN_BLOCK, KV_BD_W), prev),
        ],
        out_specs=pl.BlockSpec((tq, ATTN_Q_W), cur),
        out_shape=jax.ShapeDtypeStruct(q.shape, BF16),
        compiler_params=pltpu.CompilerParams(dimension_semantics=("parallel", "parallel"), vmem_limit_bytes=VMEM_LIMIT),
        name="swa_attention",
    )(sinks, q, k, k, v, v)


def _mlstm_kernel(qk_ref, mv_ref, so_ref, grow_ref, gnext_ref, brow_ref, hn_ref, *refs):
    n_w = (len(refs) - 4) // 2
    w_refs, out_ref, wb_refs = refs[:n_w], refs[n_w], refs[n_w + 1:2 * n_w + 1]
    state, m_scr, gate_scr = refs[2 * n_w + 1:]
    t = pl.program_id(1)
    TL = qk_ref.shape[0]
    L = M_CHUNK
    R = 2 * M_HEADS

    ri = lax.broadcasted_iota(jnp.int32, (L, L), 0)
    ci = lax.broadcasted_iota(jnp.int32, (L, L), 1)
    causal = ci <= ri
    triu = (ri <= ci).astype(BF16)
    lane_in_chunk = lax.broadcasted_iota(jnp.int32, (R, TL), 1) & (L - 1)
    lo_b = lax.broadcasted_iota(jnp.int32, (L, LANES), 1) < M_QK_DIM
    top = lax.broadcasted_iota(jnp.int32, (L, LANES), 0) < M_QK_DIM
    top2 = lax.broadcasted_iota(jnp.int32, (L, 2 * M_V_DIM), 0) < M_QK_DIM
    ones_v = jnp.ones((L, M_V_DIM), BF16)
    zero_b = jnp.zeros((), BF16)
    chunk_rows = [slice(c * L, (c + 1) * L) for c in range(TL // L)]

    def row_bcast(x8, h, n=L):
        return jnp.broadcast_to(x8[h:h + 1, :], (n, x8.shape[1]))

    def gate_scan(g_ref):
        gr = g_ref[...] + brow_ref[...]
        lf = jax.nn.log_sigmoid(gr)
        b8 = jnp.concatenate([sum(_dot(part, triu) for part in _split3(lf[:, rows])) for rows in chunk_rows], axis=1)
        b_all = pltpu.roll(b8, M_HEADS, 0)
        r_all = gr - b_all
        cm_all = r_all
        sh = 1
        while sh < L:
            cm_all = jnp.where(lane_in_chunk >= sh, jnp.maximum(cm_all, pltpu.roll(cm_all, sh, 1)), cm_all)
            sh *= 2
        return b_all, r_all, cm_all

    @pl.when(t == 0)
    def _():
        state[...] = jnp.zeros(state.shape, F32)
        m_scr[...] = jnp.zeros(m_scr.shape, F32)
        for i, x in enumerate(gate_scan(grow_ref)):
            gate_scr[i] = x

    b_all, r_all, cm_all = gate_scr[0], gate_scr[1], gate_scr[2]
    next_gates = gate_scan(gnext_ref)

    m = m_scr[...]
    gates = []
    for rows in chunk_rows:
        b, r, cm = b_all[:, rows], r_all[:, rows], cm_all[:, rows]
        rmax = jnp.broadcast_to(cm[:, L - 1:L], (R, L))
        g = jnp.broadcast_to(b[:, L - 1:L], (R, L))
        big_m = jnp.maximum(m, cm)
        inter = jnp.exp(m - big_m)
        emt = jnp.exp(-(b + big_m))
        w = jnp.exp(r - rmax)
        m_loc = g + rmax
        m_new = jnp.maximum(g + m, m_loc)
        decay = jnp.exp(g + m - m_new)
        scale = jnp.exp(m_loc - m_new)
        m = m_new
        gates.append((r, w, decay, scale, big_m, inter, emt))
    m_scr[...] = m

    for rows, (r, w, decay, scale, big_m, inter, emt) in zip(chunk_rows, gates):
        for p in range(M_HEADS // 2):
            pc = slice(p * LANES, (p + 1) * LANES)
            q_pair_b = qk_ref[rows, pc]
            k_pair_b = qk_ref[rows, M_QK_W + p * LANES:M_QK_W + (p + 1) * LANES]
            c_prev = state[p]
            c_prev_b = c_prev.astype(BF16)
            inter_cols = jnp.where(top, row_bcast(inter, 2 * p), row_bcast(inter, 2 * p + 1)).T
            qs_pair = (q_pair_b.astype(F32) * inter_cols).astype(BF16)
            k_both = jnp.concatenate([jnp.where(lo_b, k_pair_b, zero_b), jnp.where(lo_b, zero_b, k_pair_b)], axis=0)
            s_pair = _dot_nt(q_pair_b, k_both)
            v_exts = []
            for hh in range(2):
                hd = 2 * p + hh
                cols = slice(hd * M_V_DIM, (hd + 1) * M_V_DIM)
                sel = lo_b if hh == 0 else jnp.logical_not(lo_b)
                v_ext = jnp.concatenate([mv_ref[rows, cols], ones_v], axis=1)
                v_exts.append(v_ext)
                m_col, emt_col = (row_bcast(x8, hd).T for x8 in (big_m, emt))

                d_mat = jnp.exp(jnp.where(causal, row_bcast(r, hd), -jnp.inf) - m_col)
                s_mat = (s_pair[:, hh * L:(hh + 1) * L] * d_mat).astype(BF16)
                qs = jnp.where(sel, qs_pair, zero_b)
                nd = _dot(jnp.concatenate([s_mat, qs], axis=1), jnp.concatenate([v_ext, c_prev_b], axis=0))
                hcell = nd[:, :M_V_DIM] / jnp.maximum(jnp.abs(nd[:, M_V_DIM:]), emt_col)
                cell = _rms(hcell) * hn_ref[:, cols]
                out_ref[rows, cols] = (so_ref[rows, cols].astype(F32) * cell).astype(BF16)

            k_t = k_pair_b.astype(F32).T
            w_s = jnp.where(top, row_bcast(w, 2 * p), row_bcast(w, 2 * p + 1))
            kw_t = (k_t * w_s).astype(BF16)
            lhs = jnp.concatenate([jnp.where(top, kw_t, zero_b), jnp.where(top, zero_b, kw_t)], axis=1)
            a = _dot(lhs, jnp.concatenate(v_exts, axis=0))
            dec = jnp.where(top2, row_bcast(decay, 2 * p, L)[:, :1], row_bcast(decay, 2 * p + 1, L)[:, :1])
            sc = jnp.where(top2, row_bcast(scale, 2 * p, L)[:, :1], row_bcast(scale, 2 * p + 1, L)[:, :1])
            state[p] = dec * c_prev + sc * a

    for i, x in enumerate(next_gates):
        gate_scr[i] = x

    for w_ref, wb_ref in zip(w_refs, wb_refs):
        wb_ref[...] = w_ref[...].astype(BF16)


def _mlstm(qk, mv, so, grow, brow, hn, weights, batch, seq):
    tl = MLSTM_TILE
    nt = seq // tl
    steps = batch * nt
    assert all(w.shape[0] % (steps * 2 * SUBLANES) == 0 for w in weights)
    cur = lambda b, t: (b * nt + t, 0)
    const = lambda b, t: (0, 0)
    w_specs = [pl.BlockSpec((w.shape[0] // steps, w.shape[1]), cur) for w in weights]
    outs = pl.pallas_call(
        _mlstm_kernel,
        grid=(batch, nt),
        in_specs=[
            pl.BlockSpec((tl, 2 * M_QK_W), cur),
            pl.BlockSpec((tl, M_V_W), cur),
            pl.BlockSpec((tl, M_V_W), cur),
            pl.BlockSpec((2 * M_HEADS, tl), lambda b, t: (0, b * nt + t)),
            pl.BlockSpec((2 * M_HEADS, tl), lambda b, t: (0, b * nt + jnp.minimum(t + 1, nt - 1))),
            pl.BlockSpec(brow.shape, const),
            pl.BlockSpec(hn.shape, const),
        ] + w_specs,
        out_specs=[pl.BlockSpec((tl, M_V_W), cur)] + w_specs,
        out_shape=[jax.ShapeDtypeStruct(mv.shape, BF16)] + [jax.ShapeDtypeStruct(w.shape, BF16) for w in weights],
        scratch_shapes=[
            pltpu.VMEM((M_HEADS // 2, 2 * M_QK_DIM, 2 * M_V_DIM), F32),
            pltpu.VMEM((2 * M_HEADS, M_CHUNK), F32),
            pltpu.VMEM((3, 2 * M_HEADS, tl), F32),
        ],
        compiler_params=pltpu.CompilerParams(dimension_semantics=("parallel", "arbitrary"), vmem_limit_bytes=VMEM_LIMIT),
        name="mlstm",
    )(qk, mv, so, grow, grow, brow, hn, *weights)
    return outs[0], outs[1:]


def _ff_chunks():
    return [(c0, min(c0 + FF_CHUNK, D_FF)) for c0 in range(0, D_FF, FF_CHUNK)]


def _mix_ffn_kernel(x_ref, h_ref, a_ref, m_ref, gpost_ref, gpre_ref, gffn_ref, wg_ref, wa_ref, wm_ref, wo_ref,
                    wfi_ref, wfo_ref, o_ref, acc_ref):
    x = x_ref[...]
    h = h_ref[...]
    a = a_ref[...]
    m = m_ref[...]
    parts = []
    for c0 in range(0, D_MODEL, MERGE_CHUNK):
        cols = slice(c0, c0 + MERGE_CHUNK)
        gcols = slice(D_MODEL + c0, D_MODEL + c0 + MERGE_CHUNK)
        g_attn = jax.nn.sigmoid(_dot(h, wg_ref[:, cols]))
        g_mlstm = jax.nn.sigmoid(_dot(h, wg_ref[:, gcols]))
        parts.append((g_attn * _dot(a, wa_ref[:, cols]) + g_mlstm * _dot(m, wm_ref[:, cols])).astype(BF16))
    y = _dot(jnp.concatenate(parts, axis=1), wo_ref[...])
    x1 = x + _rms(y) * gpost_ref[...]

    h2 = (_rms(x1) * gpre_ref[...]).astype(BF16)
    for c0, c1 in _ff_chunks():
        gate = _dot(h2, wfi_ref[:, c0:c1])
        up = _dot(h2, wfi_ref[:, D_FF + c0:D_FF + c1])
        act = (gate * jax.nn.sigmoid(gate) * up).astype(BF16)
        part = _dot(act, wfo_ref[c0:c1, :])
        if c0 == 0:
            acc_ref[...] = part
        else:
            acc_ref[...] += part
    o_ref[...] = x1 + _rms(acc_ref[...]) * gffn_ref[...]


def _mix_ffn(x2, h, attn, ml, gpost, gpre, gffn, wg, wa, wm, wo, wfi, wfo):
    T = x2.shape[0]
    tm = TOK_TILE
    row = lambda i: (i, 0)
    const = lambda i: (0, 0)

    def wspec(w):
        return pl.BlockSpec(w.shape, const, pipeline_mode=pl.Buffered(1))

    gain = pl.BlockSpec((1, D_MODEL), const)
    return pl.pallas_call(
        _mix_ffn_kernel,
        grid=(T // tm,),
        in_specs=[
            pl.BlockSpec((tm, D_MODEL), row),
            pl.BlockSpec((tm, D_MODEL), row),
            pl.BlockSpec((tm, ATTN_Q_W), row),
            pl.BlockSpec((tm, M_V_W), row),
            gain, gain, gain,
            wspec(wg), wspec(wa), wspec(wm), wspec(wo), wspec(wfi), wspec(wfo),
        ],
        out_specs=pl.BlockSpec((tm, D_MODEL), row),
        out_shape=jax.ShapeDtypeStruct(x2.shape, F32),
        scratch_shapes=[pltpu.VMEM((tm, D_MODEL), F32)],
        compiler_params=pltpu.CompilerParams(dimension_semantics=("parallel",), vmem_limit_bytes=VMEM_LIMIT),
        name="mix_ffn",
    )(x2, h, attn, ml, gpost, gpre, gffn, wg, wa, wm, wo, wfi, wfo)


def _rope_tables(seq):
    f32 = np.float32
    inv_freq = (f32(ROPE_THETA) ** (-np.arange(0, HEAD_DIM, 2, dtype=f32) / f32(HEAD_DIM))).astype(f32)
    ang = np.arange(seq).astype(f32)[:, None] * inv_freq[None, :]
    emb = np.concatenate([ang, ang], axis=-1)
    cos = np.cos(emb).astype(f32)
    sin = np.sin(emb).astype(f32)
    first_half = np.arange(HEAD_DIM) < HEAD_DIM // 2
    sin_a = np.where(first_half, -sin, f32(0))
    sin_b = np.where(first_half, f32(0), sin)
    rep = LANES // HEAD_DIM
    return tuple(jnp.asarray(np.tile(t, (1, rep))) for t in (cos, sin_a, sin_b))


def _layer(x2, batch, seq, norm_pre_mix, norm_post_mix, norm_pre_ffn, norm_post_ffn, w_in, attn_sinks, conv_w,
           conv_b, b_igate, b_fgate, mlstm_head_norm, w_attn_branch, w_mlstm_branch, w_out, w_ffn_in, w_ffn_out):
    w_in_t = jnp.transpose(w_in)
    w_mix = _w_in_columns(w_in_t, 0, O_GATES + LANES)
    wg = _w_in_columns(w_in_t, O_BRANCH, 2 * D_MODEL)
    cos, sa, sb = _rope_tables(seq)

    h, q, k, v, qk, mv, so, grow = _inproj(
        x2, norm_pre_mix[None, :], cos, sa, sb, conv_w, conv_b[None, :], w_mix, seq)
    attn = _attention(attn_sinks, q, k, v, batch, seq)
    bias = jnp.concatenate([b_igate, b_fgate])
    ml, later_weights = _mlstm(qk, mv, so, grow, bias[:, None], mlstm_head_norm[None, :],
                               (w_attn_branch, w_mlstm_branch, w_out, w_ffn_in, w_ffn_out), batch, seq)
    return _mix_ffn(x2, h, attn, ml, norm_post_mix[None, :], norm_pre_ffn[None, :],
                    norm_post_ffn[None, :], wg, *later_weights)


def kernel(x, norm_pre_mix, norm_post_mix, norm_pre_ffn, norm_post_ffn, w_in, attn_sinks, conv_w, conv_b, b_igate,
           b_fgate, mlstm_head_norm, w_attn_branch, w_mlstm_branch, w_out, w_ffn_in, w_ffn_out):
    B, S, D = x.shape
    assert D == D_MODEL and WINDOW == ATTN_BLOCK == M_CHUNK
    assert S % max(INPROJ_TILE, ATTN_TILE, MLSTM_TILE) == 0 and (B * S) % TOK_TILE == 0
    x2 = x.reshape(B * S, D)
    for l in range(w_in.shape[0]):
        x2 = _layer(x2, B, S, norm_pre_mix[l], norm_post_mix[l], norm_pre_ffn[l], norm_post_ffn[l], w_in[l],
                    attn_sinks[l], conv_w[l], conv_b[l], b_igate[l], b_fgate[l], mlstm_head_norm[l],
                    w_attn_branch[l], w_mlstm_branch[l], w_out[l], w_ffn_in[l], w_ffn_out[l])
    return x2.reshape(B, S, D)
```

```python
import functools

import numpy as np

import jax
import jax.numpy as jnp
from jax import lax
from jax.experimental import pallas as pl
from jax.experimental.pallas import tpu as pltpu

D_MODEL = 1024
HEAD_DIM = 64
N_Q_HEADS = 8
N_KV_HEADS = 2
WINDOW = 128
ATTN_BLOCK = 128
ROPE_THETA = 10000.0
M_HEADS = 4
M_QK_DIM = 64
M_V_DIM = 128
M_CHUNK = 128
CONV_WIDTH = 4
D_FF = 2816
EPS = 1e-6

ATTN_Q_W = N_Q_HEADS * HEAD_DIM
ATTN_KV_W = N_KV_HEADS * HEAD_DIM
M_QK_W = M_HEADS * M_QK_DIM
M_V_W = M_HEADS * M_V_DIM
KV_BD_W = 2 * ATTN_KV_W * 2
O_MQK = ATTN_Q_W + 2 * ATTN_KV_W
O_MV = O_MQK + 2 * M_QK_W
O_GATES = O_MV + 2 * M_V_W
O_BRANCH = O_GATES + 2 * M_HEADS

LANES = 128
SUBLANES = 8
VMEM_LIMIT = 56 * 1024 * 1024
CONV_HALO = SUBLANES

TOK_TILE = 512
INPROJ_TILE = 1024
ATTN_TILE = 2048
MLSTM_TILE = 2048
FF_CHUNK = 256
MERGE_CHUNK = 256
W_COLS_PER_STEP = 1152

F32 = jnp.float32
BF16 = jnp.bfloat16


def _dot(a, b):
    return jnp.dot(a, b, preferred_element_type=F32)


def _dot_nt(a, b):
    return lax.dot_general(a, b, (((1,), (1,)), ((), ())), preferred_element_type=F32)


def _rms(x):
    return x * lax.rsqrt(jnp.mean(x * x, axis=-1, keepdims=True) + EPS)


def _split3(x):
    hi = x.astype(BF16)
    r1 = x - hi.astype(F32)
    mid = r1.astype(BF16)
    lo = (r1 - mid.astype(F32)).astype(BF16)
    return hi, mid, lo


def _transpose_cast_kernel(wt_ref, o_ref):
    o_ref[...] = wt_ref[...].T.astype(BF16)


def _w_in_columns(w_in_t, col0, n_cols):
    k = w_in_t.shape[1]
    step = max(c for c in range(LANES, n_cols + 1, LANES) if n_cols % c == 0 and (c <= W_COLS_PER_STEP or c == n_cols))
    return pl.pallas_call(
        _transpose_cast_kernel,
        grid=(n_cols // step,),
        in_specs=[pl.BlockSpec((pl.Element(step), pl.Element(k)),
                               lambda j: (pl.multiple_of(col0 + j * step, SUBLANES), 0))],
        out_specs=pl.BlockSpec((k, step), lambda j: (0, j)),
        out_shape=jax.ShapeDtypeStruct((k, n_cols), BF16),
        compiler_params=pltpu.CompilerParams(dimension_semantics=("parallel",), vmem_limit_bytes=VMEM_LIMIT),
        name="w_in_columns",
    )(w_in_t)


def _inproj_kernel(x_ref, g_ref, cos_ref, sa_ref, sb_ref, cw_ref, cb_ref, w_ref,
                   h_ref, q_ref, k_ref, v_ref, qk_ref, mv_ref, so_ref, grow_ref, cbuf, *, n_seq_tiles):
    tm = x_ref.shape[0]
    h = (_rms(x_ref[...]) * g_ref[...]).astype(BF16)
    h_ref[...] = h
    cos = cos_ref[...]
    sa = sa_ref[...]
    sb = sb_ref[...]
    lane = lax.broadcasted_iota(jnp.int32, cos.shape, 1)
    lo = lane < HEAD_DIM

    def rope(t):
        return t * cos + pltpu.roll(t, LANES - HEAD_DIM // 2, 1) * sa + pltpu.roll(t, HEAD_DIM // 2, 1) * sb

    def store_block_diag(ref, t):
        tb = t.astype(BF16)
        rb = pltpu.roll(t, HEAD_DIM, 1).astype(BF16)
        zero = jnp.zeros((), BF16)
        for i, (src, keep_lo) in enumerate(((tb, True), (rb, False), (rb, True), (tb, False))):
            ref[:, i * LANES:(i + 1) * LANES] = jnp.where(lo, src, zero) if keep_lo else jnp.where(lo, zero, src)

    def attn_proj():
        a = _dot(h, w_ref[:, :O_MQK])
        for j in range(ATTN_Q_W // LANES):
            q_ref[:, j * LANES:(j + 1) * LANES] = (
                rope(a[:, j * LANES:(j + 1) * LANES]) * (HEAD_DIM ** -0.5)).astype(BF16)
        store_block_diag(k_ref, rope(a[:, ATTN_Q_W:ATTN_Q_W + LANES]))
        store_block_diag(v_ref, a[:, ATTN_Q_W + LANES:ATTN_Q_W + 2 * LANES])

    @pl.when(pl.program_id(0) % n_seq_tiles == 0)
    def _():
        cbuf[0:CONV_HALO, :] = jnp.zeros((CONV_HALO, cbuf.shape[1]), F32)

    def mqk_proj(c0, c1):
        cbuf[CONV_HALO:CONV_HALO + tm, c0:c1] = _dot(h, w_ref[:, O_MQK + c0:O_MQK + c1])

    def conv_chunk(j0):
        cols = slice(j0 * LANES, (j0 + 1) * LANES)
        y = cb_ref[:, cols]
        for j in range(CONV_WIDTH):
            off = CONV_HALO - (CONV_WIDTH - 1) + j
            y = y + cw_ref[j:j + 1, cols] * cbuf[off:off + tm, cols]
        cbuf[0:CONV_HALO, cols] = cbuf[tm:tm + CONV_HALO, cols]
        y = y * jax.nn.sigmoid(y)
        if j0 * LANES >= M_QK_W:
            y = y * (M_QK_DIM ** -0.5)
        qk_ref[:, cols] = y.astype(BF16)

    def mlstm_vo_proj():
        vo = _dot(h, w_ref[:, O_MV:O_GATES + LANES])
        mv_ref[...] = vo[:, :M_V_W].astype(BF16)
        so_ref[...] = jax.nn.sigmoid(vo[:, M_V_W:2 * M_V_W]).astype(BF16)
        grow_ref[...] = vo[:, 2 * M_V_W:].T[:2 * M_HEADS, :]

    mqk_proj(0, M_QK_W)
    conv_chunk(0)
    conv_chunk(1)
    attn_proj()
    mqk_proj(M_QK_W, 2 * M_QK_W)
    conv_chunk(2)
    conv_chunk(3)
    mlstm_vo_proj()


def _inproj(x2, g, cos, sa, sb, cw, cb, w, seq):
    T = x2.shape[0]
    tm = INPROJ_TILE
    n_seq_tiles = seq // tm
    row = lambda i: (i, 0)
    pos = lambda i: (i % n_seq_tiles, 0)
    const = lambda i: (0, 0)

    def wspec(w):
        return pl.BlockSpec(w.shape, const, pipeline_mode=pl.Buffered(1))

    out_shape = (
        jax.ShapeDtypeStruct((T, D_MODEL), BF16),
        jax.ShapeDtypeStruct((T, ATTN_Q_W), BF16),
        jax.ShapeDtypeStruct((T, KV_BD_W), BF16),
        jax.ShapeDtypeStruct((T, KV_BD_W), BF16),
        jax.ShapeDtypeStruct((T, 2 * M_QK_W), BF16),
        jax.ShapeDtypeStruct((T, M_V_W), BF16),
        jax.ShapeDtypeStruct((T, M_V_W), BF16),
        jax.ShapeDtypeStruct((2 * M_HEADS, T), F32),
    )
    out_specs = (
        pl.BlockSpec((tm, D_MODEL), row),
        pl.BlockSpec((tm, ATTN_Q_W), row),
        pl.BlockSpec((tm, KV_BD_W), row),
        pl.BlockSpec((tm, KV_BD_W), row),
        pl.BlockSpec((tm, 2 * M_QK_W), row),
        pl.BlockSpec((tm, M_V_W), row),
        pl.BlockSpec((tm, M_V_W), row),
        pl.BlockSpec((2 * M_HEADS, tm), lambda i: (0, i)),
    )
    in_specs = [
        pl.BlockSpec((tm, D_MODEL), row),
        pl.BlockSpec((1, D_MODEL), const),
        pl.BlockSpec((tm, LANES), pos),
        pl.BlockSpec((tm, LANES), pos),
        pl.BlockSpec((tm, LANES), pos),
        pl.BlockSpec(cw.shape, const),
        pl.BlockSpec(cb.shape, const),
        wspec(w),
    ]
    return pl.pallas_call(
        functools.partial(_inproj_kernel, n_seq_tiles=n_seq_tiles),
        grid=(T // tm,),
        in_specs=in_specs,
        out_specs=out_specs,
        out_shape=out_shape,
        scratch_shapes=[pltpu.VMEM((CONV_HALO + tm, 2 * M_QK_W), F32)],
        compiler_params=pltpu.CompilerParams(dimension_semantics=("arbitrary",), vmem_limit_bytes=VMEM_LIMIT),
        name="inproj",
    )(x2, g, cos, sa, sb, cw, cb, w)


def _attn_kernel(sink_ref, q_ref, kc_ref, kp_ref, vc_ref, vp_ref, wt_ref, o_ref, wb_ref):
    wb_ref[...] = wt_ref[...].T.astype(BF16)
    t = pl.program_id(1)
    blk = ATTN_BLOCK
    n_blk = q_ref.shape[0] // blk
    lo_q = lax.broadcasted_iota(jnp.int32, (blk, LANES), 1) < HEAD_DIM
    ri = lax.broadcasted_iota(jnp.int32, (blk, blk), 0)
    ci = lax.broadcasted_iota(jnp.int32, (blk, blk), 1)
    own = ci <= ri
    prev_ok = ci > ri + jnp.where(t > 0, 0, blk)
    zero = jnp.zeros((), BF16)
    row_is_a = lax.broadcasted_iota(jnp.int32, (4 * blk, LANES), 0) < 2 * blk
    lane_is_a = lax.broadcasted_iota(jnp.int32, (4 * blk, LANES), 1) < HEAD_DIM
    ones_bd = (row_is_a == lane_is_a).astype(BF16)

    for n in range(n_blk):
        rows = slice(n * blk, (n + 1) * blk)
        k_prev = kp_ref if n == 0 else kc_ref.at[(n - 1) * blk:n * blk, :]
        v_prev = vp_ref if n == 0 else vc_ref.at[(n - 1) * blk:n * blk, :]
        for j in range(N_KV_HEADS):
            kv_cols = [slice((2 * j + i) * LANES, (2 * j + i + 1) * LANES) for i in range(2)]
            k_bd = jnp.concatenate([x for c in kv_cols for x in (k_prev[:, c], kc_ref[rows, c])], axis=0)
            v_bd = jnp.concatenate([x for c in kv_cols for x in (v_prev[:, c], vc_ref[rows, c])], axis=0)
            c0 = 2 * j * LANES
            q2 = jnp.concatenate([q_ref[rows, c0:c0 + LANES], q_ref[rows, c0 + LANES:c0 + 2 * LANES]], axis=0)
            s = _dot_nt(q2, k_bd)
            e_rows, sink_rows = [], []
            for p in range(2):
                e_cols, sink_cols = [], []
                for hh in range(2):
                    sink = sink_ref[4 * j + 2 * p + hh]
                    s_prev = s[p * blk:(p + 1) * blk, 2 * hh * blk:(2 * hh + 1) * blk]
                    s_own = s[p * blk:(p + 1) * blk, (2 * hh + 1) * blk:(2 * hh + 2) * blk]
                    if n == 0:
                        s_prev = jnp.where(prev_ok, s_prev, -jnp.inf)
                    sc = jnp.where(own, s_own, s_prev)
                    m = jnp.maximum(jnp.max(sc, axis=-1, keepdims=True), sink)
                    eb = jnp.exp(sc - m).astype(BF16)
                    e_cols += [jnp.where(own, zero, eb), jnp.where(own, eb, zero)]
                    sink_cols.append(jnp.exp(sink - m))
                e_rows.append(jnp.concatenate(e_cols, axis=1))
                sink_rows.append(jnp.where(lo_q, sink_cols[0], sink_cols[1]))
            o = _dot(jnp.concatenate(e_rows, axis=0), jnp.concatenate([v_bd, ones_bd], axis=1))
            for p in range(2):
                op = o[p * blk:(p + 1) * blk]
                o_ref[rows, c0 + p * LANES:c0 + (p + 1) * LANES] = (
                    op[:, :LANES] / (op[:, LANES:] + sink_rows[p])).astype(BF16)


def _attention(sinks, q, k, v, w_in_t, col0, n_cols, batch, seq):
    tq = ATTN_TILE
    nt = seq // tq
    per = tq // ATTN_BLOCK
    kdim = w_in_t.shape[1]
    w_step = n_cols // (batch * nt)
    assert n_cols % (batch * nt) == 0 and w_step % LANES == 0
    cur = lambda b, t: (b * nt + t, 0)
    prev = lambda b, t: (jnp.maximum((b * nt + t) * per - 1, 0), 0)
    return pl.pallas_call(
        _attn_kernel,
        grid=(batch, nt),
        in_specs=[
            pl.BlockSpec(memory_space=pltpu.SMEM),
            pl.BlockSpec((tq, ATTN_Q_W), cur),
            pl.BlockSpec((tq, KV_BD_W), cur),
            pl.BlockSpec((ATTN_BLOCK, KV_BD_W), prev),
            pl.BlockSpec((tq, KV_BD_W), cur),
            pl.BlockSpec((ATTN_BLOCK, KV_BD_W), prev),
            pl.BlockSpec((pl.Element(w_step), pl.Element(kdim)),
                         lambda b, t: (pl.multiple_of(col0 + (b * nt + t) * w_step, SUBLANES), 0)),
        ],
        out_specs=[pl.BlockSpec((tq, ATTN_Q_W), cur), pl.BlockSpec((kdim, w_step), lambda b, t: (0, b * nt + t))],
        out_shape=[jax.ShapeDtypeStruct(q.shape, BF16), jax.ShapeDtypeStruct((kdim, n_cols), BF16)],
        compiler_params=pltpu.CompilerParams(dimension_semantics=("parallel", "parallel"), vmem_limit_bytes=VMEM_LIMIT),
        name="swa_attention",
    )(sinks, q, k, k, v, v, w_in_t)


def _mlstm_kernel(qk_ref, mv_ref, so_ref, grow_ref, gnext_ref, brow_ref, hn_ref, *refs):
    n_w = (len(refs) - 4) // 2
    w_refs, out_ref, wb_refs = refs[:n_w], refs[n_w], refs[n_w + 1:2 * n_w + 1]
    state, m_scr, gate_scr = refs[2 * n_w + 1:]
    t = pl.program_id(1)
    TL = qk_ref.shape[0]
    L = M_CHUNK
    R = 2 * M_HEADS

    ri = lax.broadcasted_iota(jnp.int32, (L, L), 0)
    ci = lax.broadcasted_iota(jnp.int32, (L, L), 1)
    causal = ci <= ri
    triu = (ri <= ci).astype(BF16)
    lane_in_chunk = lax.broadcasted_iota(jnp.int32, (R, TL), 1) & (L - 1)
    lo_b = lax.broadcasted_iota(jnp.int32, (L, LANES), 1) < M_QK_DIM
    top = lax.broadcasted_iota(jnp.int32, (L, LANES), 0) < M_QK_DIM
    top2 = lax.broadcasted_iota(jnp.int32, (L, 2 * M_V_DIM), 0) < M_QK_DIM
    ones_v = jnp.ones((L, M_V_DIM), BF16)
    zero_b = jnp.zeros((), BF16)
    chunk_rows = [slice(c * L, (c + 1) * L) for c in range(TL // L)]

    def row_bcast(x8, h, n=L):
        return jnp.broadcast_to(x8[h:h + 1, :], (n, x8.shape[1]))

    def gate_scan(g_ref):
        gr = g_ref[...] + brow_ref[...]
        lf = jax.nn.log_sigmoid(gr)
        b8 = jnp.concatenate([sum(_dot(part, triu) for part in _split3(lf[:, rows])) for rows in chunk_rows], axis=1)
        b_all = pltpu.roll(b8, M_HEADS, 0)
        r_all = gr - b_all
        cm_all = r_all
        sh = 1
        while sh < L:
            cm_all = jnp.where(lane_in_chunk >= sh, jnp.maximum(cm_all, pltpu.roll(cm_all, sh, 1)), cm_all)
            sh *= 2
        return b_all, r_all, cm_all

    @pl.when(t == 0)
    def _():
        state[...] = jnp.zeros(state.shape, F32)
        m_scr[...] = jnp.zeros(m_scr.shape, F32)
        for i, x in enumerate(gate_scan(grow_ref)):
            gate_scr[i] = x

    b_all, r_all, cm_all = gate_scr[0], gate_scr[1], gate_scr[2]
    next_gates = gate_scan(gnext_ref)

    m = m_scr[...]
    gates = []
    for rows in chunk_rows:
        b, r, cm = b_all[:, rows], r_all[:, rows], cm_all[:, rows]
        rmax = jnp.broadcast_to(cm[:, L - 1:L], (R, L))
        g = jnp.broadcast_to(b[:, L - 1:L], (R, L))
        big_m = jnp.maximum(m, cm)
        inter = jnp.exp(m - big_m)
        emt = jnp.exp(-(b + big_m))
        w = jnp.exp(r - rmax)
        m_loc = g + rmax
        m_new = jnp.maximum(g + m, m_loc)
        decay = jnp.exp(g + m - m_new)
        scale = jnp.exp(m_loc - m_new)
        m = m_new
        gates.append((r, w, decay, scale, big_m, inter, emt))
    m_scr[...] = m

    for rows, (r, w, decay, scale, big_m, inter, emt) in zip(chunk_rows, gates):
        for p in range(M_HEADS // 2):
            pc = slice(p * LANES, (p + 1) * LANES)
            q_pair_b = qk_ref[rows, pc]
            k_pair_b = qk_ref[rows, M_QK_W + p * LANES:M_QK_W + (p + 1) * LANES]
            c_prev = state[p]
            c_prev_b = c_prev.astype(BF16)
            inter_cols = jnp.where(top, row_bcast(inter, 2 * p), row_bcast(inter, 2 * p + 1)).T
            qs_pair = (q_pair_b.astype(F32) * inter_cols).astype(BF16)
            k_both = jnp.concatenate([jnp.where(lo_b, k_pair_b, zero_b), jnp.where(lo_b, zero_b, k_pair_b)], axis=0)
            s_pair = _dot_nt(q_pair_b, k_both)
            v_exts = []
            for hh in range(2):
                hd = 2 * p + hh
                cols = slice(hd * M_V_DIM, (hd + 1) * M_V_DIM)
                sel = lo_b if hh == 0 else jnp.logical_not(lo_b)
                v_ext = jnp.concatenate([mv_ref[rows, cols], ones_v], axis=1)
                v_exts.append(v_ext)
                m_col, emt_col = (row_bcast(x8, hd).T for x8 in (big_m, emt))

                d_mat = jnp.exp(jnp.where(causal, row_bcast(r, hd), -jnp.inf) - m_col)
                s_mat = (s_pair[:, hh * L:(hh + 1) * L] * d_mat).astype(BF16)
                qs = jnp.where(sel, qs_pair, zero_b)
                nd = _dot(jnp.concatenate([s_mat, qs], axis=1), jnp.concatenate([v_ext, c_prev_b], axis=0))
                hcell = nd[:, :M_V_DIM] / jnp.maximum(jnp.abs(nd[:, M_V_DIM:]), emt_col)
                cell = _rms(hcell) * hn_ref[:, cols]
                out_ref[rows, cols] = (so_ref[rows, cols].astype(F32) * cell).astype(BF16)

            k_t = k_pair_b.astype(F32).T
            w_s = jnp.where(top, row_bcast(w, 2 * p), row_bcast(w, 2 * p + 1))
            kw_t = (k_t * w_s).astype(BF16)
            lhs = jnp.concatenate([jnp.where(top, kw_t, zero_b), jnp.where(top, zero_b, kw_t)], axis=1)
            a = _dot(lhs, jnp.concatenate(v_exts, axis=0))
            dec = jnp.where(top2, row_bcast(decay, 2 * p, L)[:, :1], row_bcast(decay, 2 * p + 1, L)[:, :1])
            sc = jnp.where(top2, row_bcast(scale, 2 * p, L)[:, :1], row_bcast(scale, 2 * p + 1, L)[:, :1])
            state[p] = dec * c_prev + sc * a

    for i, x in enumerate(next_gates):
        gate_scr[i] = x

    for w_ref, wb_ref in zip(w_refs, wb_refs):
        wb_ref[...] = w_ref[...].astype(BF16)


def _mlstm(qk, mv, so, grow, brow, hn, weights, batch, seq):
    tl = MLSTM_TILE
    nt = seq // tl
    steps = batch * nt
    assert all(w.shape[0] % (steps * 2 * SUBLANES) == 0 for w in weights)
    cur = lambda b, t: (b * nt + t, 0)
    const = lambda b, t: (0, 0)
    w_specs = [pl.BlockSpec((w.shape[0] // steps, w.shape[1]), cur) for w in weights]
    outs = pl.pallas_call(
        _mlstm_kernel,
        grid=(batch, nt),
        in_specs=[
            pl.BlockSpec((tl, 2 * M_QK_W), cur),
            pl.BlockSpec((tl, M_V_W), cur),
            pl.BlockSpec((tl, M_V_W), cur),
            pl.BlockSpec((2 * M_HEADS, tl), lambda b, t: (0, b * nt + t)),
            pl.BlockSpec((2 * M_HEADS, tl), lambda b, t: (0, b * nt + jnp.minimum(t + 1, nt - 1))),
            pl.BlockSpec(brow.shape, const),
            pl.BlockSpec(hn.shape, const),
        ] + w_specs,
        out_specs=[pl.BlockSpec((tl, M_V_W), cur)] + w_specs,
        out_shape=[jax.ShapeDtypeStruct(mv.shape, BF16)] + [jax.ShapeDtypeStruct(w.shape, BF16) for w in weights],
        scratch_shapes=[
            pltpu.VMEM((M_HEADS // 2, 2 * M_QK_DIM, 2 * M_V_DIM), F32),
            pltpu.VMEM((2 * M_HEADS, M_CHUNK), F32),
            pltpu.VMEM((3, 2 * M_HEADS, tl), F32),
        ],
        compiler_params=pltpu.CompilerParams(dimension_semantics=("parallel", "arbitrary"), vmem_limit_bytes=VMEM_LIMIT),
        name="mlstm",
    )(qk, mv, so, grow, grow, brow, hn, *weights)
    return outs[0], outs[1:]


def _ff_chunks():
    return [(c0, min(c0 + FF_CHUNK, D_FF)) for c0 in range(0, D_FF, FF_CHUNK)]


def _mix_ffn_kernel(x_ref, h_ref, a_ref, m_ref, gpost_ref, gpre_ref, gffn_ref, wg_ref, wa_ref, wm_ref, wo_ref,
                    wfi_ref, wfo_ref, o_ref, acc_ref):
    x = x_ref[...]
    h = h_ref[...]
    a = a_ref[...]
    m = m_ref[...]
    parts = []
    for c0 in range(0, D_MODEL, MERGE_CHUNK):
        cols = slice(c0, c0 + MERGE_CHUNK)
        gcols = slice(D_MODEL + c0, D_MODEL + c0 + MERGE_CHUNK)
        g_attn = jax.nn.sigmoid(_dot(h, wg_ref[:, cols]))
        g_mlstm = jax.nn.sigmoid(_dot(h, wg_ref[:, gcols]))
        parts.append((g_attn * _dot(a, wa_ref[:, cols]) + g_mlstm * _dot(m, wm_ref[:, cols])).astype(BF16))
    y = _dot(jnp.concatenate(parts, axis=1), wo_ref[...])
    x1 = x + _rms(y) * gpost_ref[...]

    h2 = (_rms(x1) * gpre_ref[...]).astype(BF16)
    for c0, c1 in _ff_chunks():
        gate = _dot(h2, wfi_ref[:, c0:c1])
        up = _dot(h2, wfi_ref[:, D_FF + c0:D_FF + c1])
        act = (gate * jax.nn.sigmoid(gate) * up).astype(BF16)
        part = _dot(act, wfo_ref[c0:c1, :])
        if c0 == 0:
            acc_ref[...] = part
        else:
            acc_ref[...] += part
    o_ref[...] = x1 + _rms(acc_ref[...]) * gffn_ref[...]


def _mix_ffn(x2, h, attn, ml, gpost, gpre, gffn, wg, wa, wm, wo, wfi, wfo):
    T = x2.shape[0]
    tm = TOK_TILE
    row = lambda i: (i, 0)
    const = lambda i: (0, 0)

    def wspec(w):
        return pl.BlockSpec(w.shape, const, pipeline_mode=pl.Buffered(1))

    gain = pl.BlockSpec((1, D_MODEL), const)
    return pl.pallas_call(
        _mix_ffn_kernel,
        grid=(T // tm,),
        in_specs=[
            pl.BlockSpec((tm, D_MODEL), row),
            pl.BlockSpec((tm, D_MODEL), row),
            pl.BlockSpec((tm, ATTN_Q_W), row),
            pl.BlockSpec((tm, M_V_W), row),
            gain, gain, gain,
            wspec(wg), wspec(wa), wspec(wm), wspec(wo), wspec(wfi), wspec(wfo),
        ],
        out_specs=pl.BlockSpec((tm, D_MODEL), row),
        out_shape=jax.ShapeDtypeStruct(x2.shape, F32),
        scratch_shapes=[pltpu.VMEM((tm, D_MODEL), F32)],
        compiler_params=pltpu.CompilerParams(dimension_semantics=("parallel",), vmem_limit_bytes=VMEM_LIMIT),
        name="mix_ffn",
    )(x2, h, attn, ml, gpost, gpre, gffn, wg, wa, wm, wo, wfi, wfo)


def _rope_tables(seq):
    f32 = np.float32
    inv_freq = (f32(ROPE_THETA) ** (-np.arange(0, HEAD_DIM, 2, dtype=f32) / f32(HEAD_DIM))).astype(f32)
    ang = np.arange(seq).astype(f32)[:, None] * inv_freq[None, :]
    emb = np.concatenate([ang, ang], axis=-1)
    cos = np.cos(emb).astype(f32)
    sin = np.sin(emb).astype(f32)
    first_half = np.arange(HEAD_DIM) < HEAD_DIM // 2
    sin_a = np.where(first_half, -sin, f32(0))
    sin_b = np.where(first_half, f32(0), sin)
    rep = LANES // HEAD_DIM
    return tuple(jnp.asarray(np.tile(t, (1, rep))) for t in (cos, sin_a, sin_b))


def _layer(x2, batch, seq, norm_pre_mix, norm_post_mix, norm_pre_ffn, norm_post_ffn, w_in, attn_sinks, conv_w,
           conv_b, b_igate, b_fgate, mlstm_head_norm, w_attn_branch, w_mlstm_branch, w_out, w_ffn_in, w_ffn_out):
    w_in_t = jnp.transpose(w_in)
    w_mix = _w_in_columns(w_in_t, 0, O_GATES + LANES)
    cos, sa, sb = _rope_tables(seq)

    h, q, k, v, qk, mv, so, grow = _inproj(
        x2, norm_pre_mix[None, :], cos, sa, sb, conv_w, conv_b[None, :], w_mix, seq)
    attn, wg = _attention(attn_sinks, q, k, v, w_in_t, O_BRANCH, 2 * D_MODEL, batch, seq)
    bias = jnp.concatenate([b_igate, b_fgate])
    ml, later_weights = _mlstm(qk, mv, so, grow, bias[:, None], mlstm_head_norm[None, :],
                               (w_attn_branch, w_mlstm_branch, w_out, w_ffn_in, w_ffn_out), batch, seq)
    return _mix_ffn(x2, h, attn, ml, norm_post_mix[None, :], norm_pre_ffn[None, :],
                    norm_post_ffn[None, :], wg, *later_weights)


def kernel(x, norm_pre_mix, norm_post_mix, norm_pre_ffn, norm_post_ffn, w_in, attn_sinks, conv_w, conv_b, b_igate,
           b_fgate, mlstm_head_norm, w_attn_branch, w_mlstm_branch, w_out, w_ffn_in, w_ffn_out):
    B, S, D = x.shape
    assert D == D_MODEL and WINDOW == ATTN_BLOCK == M_CHUNK
    assert S % max(INPROJ_TILE, ATTN_TILE, MLSTM_TILE) == 0 and (B * S) % TOK_TILE == 0
    x2 = x.reshape(B * S, D)
    for l in range(w_in.shape[0]):
        x2 = _layer(x2, B, S, norm_pre_mix[l], norm_post_mix[l], norm_pre_ffn[l], norm_post_ffn[l], w_in[l],
                    attn_sinks[l], conv_w[l], conv_b[l], b_igate[l], b_fgate[l], mlstm_head_norm[l],
                    w_attn_branch[l], w_mlstm_branch[l], w_out[l], w_ffn_in[l], w_ffn_out[l])
    return x2.reshape(B, S, D)
```

```python
import functools

import numpy as np

import jax
import jax.numpy as jnp
from jax import lax
from jax.experimental import pallas as pl
from jax.experimental.pallas import tpu as pltpu

D_MODEL = 1024
HEAD_DIM = 64
N_Q_HEADS = 8
N_KV_HEADS = 2
WINDOW = 128
ATTN_BLOCK = 128
ROPE_THETA = 10000.0
M_HEADS = 4
M_QK_DIM = 64
M_V_DIM = 128
M_CHUNK = 128
CONV_WIDTH = 4
D_FF = 2816
EPS = 1e-6

ATTN_Q_W = N_Q_HEADS * HEAD_DIM
ATTN_KV_W = N_KV_HEADS * HEAD_DIM
M_QK_W = M_HEADS * M_QK_DIM
M_V_W = M_HEADS * M_V_DIM
KV_BD_W = 2 * ATTN_KV_W * 2
O_MQK = ATTN_Q_W + 2 * ATTN_KV_W
O_MV = O_MQK + 2 * M_QK_W
O_GATES = O_MV + 2 * M_V_W
O_BRANCH = O_GATES + 2 * M_HEADS

LANES = 128
SUBLANES = 8
VMEM_LIMIT = 56 * 1024 * 1024
CONV_HALO = SUBLANES

TOK_TILE = 512
INPROJ_TILE = 1024
ATTN_TILE = 2048
MLSTM_TILE = 2048
FF_CHUNK = 256
MERGE_CHUNK = 256

F32 = jnp.float32
BF16 = jnp.bfloat16


def _dot(a, b):
    return jnp.dot(a, b, preferred_element_type=F32)


def _dot_nt(a, b):
    return lax.dot_general(a, b, (((1,), (1,)), ((), ())), preferred_element_type=F32)


def _rms(x):
    return x * lax.rsqrt(jnp.mean(x * x, axis=-1, keepdims=True) + EPS)


def _split3(x):
    hi = x.astype(BF16)
    r1 = x - hi.astype(F32)
    mid = r1.astype(BF16)
    lo = (r1 - mid.astype(F32)).astype(BF16)
    return hi, mid, lo


def _inproj_kernel(x_ref, g_ref, cos_ref, sa_ref, sb_ref, cw_ref, cb_ref, wt_ref,
                   h_ref, q_ref, k_ref, v_ref, qk_ref, mv_ref, so_ref, grow_ref, cbuf, w_ref, *, n_seq_tiles):
    tm = x_ref.shape[0]

    @pl.when(pl.program_id(0) == 0)
    def _():
        w_ref[...] = wt_ref[...].T.astype(BF16)

    h = (_rms(x_ref[...]) * g_ref[...]).astype(BF16)
    h_ref[...] = h
    cos = cos_ref[...]
    sa = sa_ref[...]
    sb = sb_ref[...]
    lane = lax.broadcasted_iota(jnp.int32, cos.shape, 1)
    lo = lane < HEAD_DIM

    def rope(t):
        return t * cos + pltpu.roll(t, LANES - HEAD_DIM // 2, 1) * sa + pltpu.roll(t, HEAD_DIM // 2, 1) * sb

    def store_block_diag(ref, t):
        tb = t.astype(BF16)
        rb = pltpu.roll(t, HEAD_DIM, 1).astype(BF16)
        zero = jnp.zeros((), BF16)
        for i, (src, keep_lo) in enumerate(((tb, True), (rb, False), (rb, True), (tb, False))):
            ref[:, i * LANES:(i + 1) * LANES] = jnp.where(lo, src, zero) if keep_lo else jnp.where(lo, zero, src)

    def attn_proj():
        a = _dot(h, w_ref[:, :O_MQK])
        for j in range(ATTN_Q_W // LANES):
            q_ref[:, j * LANES:(j + 1) * LANES] = (
                rope(a[:, j * LANES:(j + 1) * LANES]) * (HEAD_DIM ** -0.5)).astype(BF16)
        store_block_diag(k_ref, rope(a[:, ATTN_Q_W:ATTN_Q_W + LANES]))
        store_block_diag(v_ref, a[:, ATTN_Q_W + LANES:ATTN_Q_W + 2 * LANES])

    @pl.when(pl.program_id(0) % n_seq_tiles == 0)
    def _():
        cbuf[0:CONV_HALO, :] = jnp.zeros((CONV_HALO, cbuf.shape[1]), F32)

    def mqk_proj(c0, c1):
        cbuf[CONV_HALO:CONV_HALO + tm, c0:c1] = _dot(h, w_ref[:, O_MQK + c0:O_MQK + c1])

    def conv_chunk(j0):
        cols = slice(j0 * LANES, (j0 + 1) * LANES)
        y = cb_ref[:, cols]
        for j in range(CONV_WIDTH):
            off = CONV_HALO - (CONV_WIDTH - 1) + j
            y = y + cw_ref[j:j + 1, cols] * cbuf[off:off + tm, cols]
        cbuf[0:CONV_HALO, cols] = cbuf[tm:tm + CONV_HALO, cols]
        y = y * jax.nn.sigmoid(y)
        if j0 * LANES >= M_QK_W:
            y = y * (M_QK_DIM ** -0.5)
        qk_ref[:, cols] = y.astype(BF16)

    def mlstm_vo_proj():
        vo = _dot(h, w_ref[:, O_MV:O_GATES + LANES])
        mv_ref[...] = vo[:, :M_V_W].astype(BF16)
        so_ref[...] = jax.nn.sigmoid(vo[:, M_V_W:2 * M_V_W]).astype(BF16)
        grow_ref[...] = vo[:, 2 * M_V_W:].T[:2 * M_HEADS, :]

    mqk_proj(0, M_QK_W)
    conv_chunk(0)
    conv_chunk(1)
    attn_proj()
    mqk_proj(M_QK_W, 2 * M_QK_W)
    conv_chunk(2)
    conv_chunk(3)
    mlstm_vo_proj()


def _inproj(x2, g, cos, sa, sb, cw, cb, w_in_t, n_cols, seq):
    T = x2.shape[0]
    tm = INPROJ_TILE
    n_seq_tiles = seq // tm
    row = lambda i: (i, 0)
    pos = lambda i: (i % n_seq_tiles, 0)
    const = lambda i: (0, 0)

    out_shape = (
        jax.ShapeDtypeStruct((T, D_MODEL), BF16),
        jax.ShapeDtypeStruct((T, ATTN_Q_W), BF16),
        jax.ShapeDtypeStruct((T, KV_BD_W), BF16),
        jax.ShapeDtypeStruct((T, KV_BD_W), BF16),
        jax.ShapeDtypeStruct((T, 2 * M_QK_W), BF16),
        jax.ShapeDtypeStruct((T, M_V_W), BF16),
        jax.ShapeDtypeStruct((T, M_V_W), BF16),
        jax.ShapeDtypeStruct((2 * M_HEADS, T), F32),
    )
    out_specs = (
        pl.BlockSpec((tm, D_MODEL), row),
        pl.BlockSpec((tm, ATTN_Q_W), row),
        pl.BlockSpec((tm, KV_BD_W), row),
        pl.BlockSpec((tm, KV_BD_W), row),
        pl.BlockSpec((tm, 2 * M_QK_W), row),
        pl.BlockSpec((tm, M_V_W), row),
        pl.BlockSpec((tm, M_V_W), row),
        pl.BlockSpec((2 * M_HEADS, tm), lambda i: (0, i)),
    )
    in_specs = [
        pl.BlockSpec((tm, D_MODEL), row),
        pl.BlockSpec((1, D_MODEL), const),
        pl.BlockSpec((tm, LANES), pos),
        pl.BlockSpec((tm, LANES), pos),
        pl.BlockSpec((tm, LANES), pos),
        pl.BlockSpec(cw.shape, const),
        pl.BlockSpec(cb.shape, const),
        pl.BlockSpec((n_cols, w_in_t.shape[1]), const, pipeline_mode=pl.Buffered(1)),
    ]
    return pl.pallas_call(
        functools.partial(_inproj_kernel, n_seq_tiles=n_seq_tiles),
        grid=(T // tm,),
        in_specs=in_specs,
        out_specs=out_specs,
        out_shape=out_shape,
        scratch_shapes=[pltpu.VMEM((CONV_HALO + tm, 2 * M_QK_W), F32),
                        pltpu.VMEM((w_in_t.shape[1], n_cols), BF16)],
        compiler_params=pltpu.CompilerParams(dimension_semantics=("arbitrary",), vmem_limit_bytes=VMEM_LIMIT),
        name="inproj",
    )(x2, g, cos, sa, sb, cw, cb, w_in_t)


def _attn_kernel(sink_ref, q_ref, kc_ref, kp_ref, vc_ref, vp_ref, wt_ref, o_ref, wb_ref):
    wb_ref[...] = wt_ref[...].T.astype(BF16)
    t = pl.program_id(1)
    blk = ATTN_BLOCK
    n_blk = q_ref.shape[0] // blk
    lo_q = lax.broadcasted_iota(jnp.int32, (blk, LANES), 1) < HEAD_DIM
    ri = lax.broadcasted_iota(jnp.int32, (blk, blk), 0)
    ci = lax.broadcasted_iota(jnp.int32, (blk, blk), 1)
    own = ci <= ri
    prev_ok = ci > ri + jnp.where(t > 0, 0, blk)
    zero = jnp.zeros((), BF16)
    row_is_a = lax.broadcasted_iota(jnp.int32, (4 * blk, LANES), 0) < 2 * blk
    lane_is_a = lax.broadcasted_iota(jnp.int32, (4 * blk, LANES), 1) < HEAD_DIM
    ones_bd = (row_is_a == lane_is_a).astype(BF16)

    for n in range(n_blk):
        rows = slice(n * blk, (n + 1) * blk)
        k_prev = kp_ref if n == 0 else kc_ref.at[(n - 1) * blk:n * blk, :]
        v_prev = vp_ref if n == 0 else vc_ref.at[(n - 1) * blk:n * blk, :]
        for j in range(N_KV_HEADS):
            kv_cols = [slice((2 * j + i) * LANES, (2 * j + i + 1) * LANES) for i in range(2)]
            k_bd = jnp.concatenate([x for c in kv_cols for x in (k_prev[:, c], kc_ref[rows, c])], axis=0)
            v_bd = jnp.concatenate([x for c in kv_cols for x in (v_prev[:, c], vc_ref[rows, c])], axis=0)
            c0 = 2 * j * LANES
            q2 = jnp.concatenate([q_ref[rows, c0:c0 + LANES], q_ref[rows, c0 + LANES:c0 + 2 * LANES]], axis=0)
            s = _dot_nt(q2, k_bd)
            e_rows, sink_rows = [], []
            for p in range(2):
                e_cols, sink_cols = [], []
                for hh in range(2):
                    sink = sink_ref[4 * j + 2 * p + hh]
                    s_prev = s[p * blk:(p + 1) * blk, 2 * hh * blk:(2 * hh + 1) * blk]
                    s_own = s[p * blk:(p + 1) * blk, (2 * hh + 1) * blk:(2 * hh + 2) * blk]
                    if n == 0:
                        s_prev = jnp.where(prev_ok, s_prev, -jnp.inf)
                    sc = jnp.where(own, s_own, s_prev)
                    m = jnp.maximum(jnp.max(sc, axis=-1, keepdims=True), sink)
                    eb = jnp.exp(sc - m).astype(BF16)
                    e_cols += [jnp.where(own, zero, eb), jnp.where(own, eb, zero)]
                    sink_cols.append(jnp.exp(sink - m))
                e_rows.append(jnp.concatenate(e_cols, axis=1))
                sink_rows.append(jnp.where(lo_q, sink_cols[0], sink_cols[1]))
            o = _dot(jnp.concatenate(e_rows, axis=0), jnp.concatenate([v_bd, ones_bd], axis=1))
            for p in range(2):
                op = o[p * blk:(p + 1) * blk]
                o_ref[rows, c0 + p * LANES:c0 + (p + 1) * LANES] = (
                    op[:, :LANES] / (op[:, LANES:] + sink_rows[p])).astype(BF16)


def _attention(sinks, q, k, v, w_in_t, col0, n_cols, batch, seq):
    tq = ATTN_TILE
    nt = seq // tq
    per = tq // ATTN_BLOCK
    kdim = w_in_t.shape[1]
    w_step = n_cols // (batch * nt)
    assert n_cols % (batch * nt) == 0 and w_step % LANES == 0
    cur = lambda b, t: (b * nt + t, 0)
    prev = lambda b, t: (jnp.maximum((b * nt + t) * per - 1, 0), 0)
    return pl.pallas_call(
        _attn_kernel,
        grid=(batch, nt),
        in_specs=[
            pl.BlockSpec(memory_space=pltpu.SMEM),
            pl.BlockSpec((tq, ATTN_Q_W), cur),
            pl.BlockSpec((tq, KV_BD_W), cur),
            pl.BlockSpec((ATTN_BLOCK, KV_BD_W), prev),
            pl.BlockSpec((tq, KV_BD_W), cur),
            pl.BlockSpec((ATTN_BLOCK, KV_BD_W), prev),
            pl.BlockSpec((pl.Element(w_step), pl.Element(kdim)),
                         lambda b, t: (pl.multiple_of(col0 + (b * nt + t) * w_step, SUBLANES), 0)),
        ],
        out_specs=[pl.BlockSpec((tq, ATTN_Q_W), cur), pl.BlockSpec((kdim, w_step), lambda b, t: (0, b * nt + t))],
        out_shape=[jax.ShapeDtypeStruct(q.shape, BF16), jax.ShapeDtypeStruct((kdim, n_cols), BF16)],
        compiler_params=pltpu.CompilerParams(dimension_semantics=("parallel", "parallel"), vmem_limit_bytes=VMEM_LIMIT),
        name="swa_attention",
    )(sinks, q, k, k, v, v, w_in_t)


def _mlstm_kernel(qk_ref, mv_ref, so_ref, grow_ref, gnext_ref, brow_ref, hn_ref, *refs):
    n_w = (len(refs) - 4) // 2
    w_refs, out_ref, wb_refs = refs[:n_w], refs[n_w], refs[n_w + 1:2 * n_w + 1]
    state, m_scr, gate_scr = refs[2 * n_w + 1:]
    t = pl.program_id(1)
    TL = qk_ref.shape[0]
    L = M_CHUNK
    R = 2 * M_HEADS

    ri = lax.broadcasted_iota(jnp.int32, (L, L), 0)
    ci = lax.broadcasted_iota(jnp.int32, (L, L), 1)
    causal = ci <= ri
    triu = (ri <= ci).astype(BF16)
    lane_in_chunk = lax.broadcasted_iota(jnp.int32, (R, TL), 1) & (L - 1)
    lo_b = lax.broadcasted_iota(jnp.int32, (L, LANES), 1) < M_QK_DIM
    top = lax.broadcasted_iota(jnp.int32, (L, LANES), 0) < M_QK_DIM
    top2 = lax.broadcasted_iota(jnp.int32, (L, 2 * M_V_DIM), 0) < M_QK_DIM
    ones_v = jnp.ones((L, M_V_DIM), BF16)
    zero_b = jnp.zeros((), BF16)
    chunk_rows = [slice(c * L, (c + 1) * L) for c in range(TL // L)]

    def row_bcast(x8, h, n=L):
        return jnp.broadcast_to(x8[h:h + 1, :], (n, x8.shape[1]))

    def gate_scan(g_ref):
        gr = g_ref[...] + brow_ref[...]
        lf = jax.nn.log_sigmoid(gr)
        b8 = jnp.concatenate([sum(_dot(part, triu) for part in _split3(lf[:, rows])) for rows in chunk_rows], axis=1)
        b_all = pltpu.roll(b8, M_HEADS, 0)
        r_all = gr - b_all
        cm_all = r_all
        sh = 1
        while sh < L:
            cm_all = jnp.where(lane_in_chunk >= sh, jnp.maximum(cm_all, pltpu.roll(cm_all, sh, 1)), cm_all)
            sh *= 2
        return b_all, r_all, cm_all

    @pl.when(t == 0)
    def _():
        state[...] = jnp.zeros(state.shape, F32)
        m_scr[...] = jnp.zeros(m_scr.shape, F32)
        for i, x in enumerate(gate_scan(grow_ref)):
            gate_scr[i] = x

    b_all, r_all, cm_all = gate_scr[0], gate_scr[1], gate_scr[2]
    next_gates = gate_scan(gnext_ref)

    m = m_scr[...]
    gates = []
    for rows in chunk_rows:
        b, r, cm = b_all[:, rows], r_all[:, rows], cm_all[:, rows]
        rmax = jnp.broadcast_to(cm[:, L - 1:L], (R, L))
        g = jnp.broadcast_to(b[:, L - 1:L], (R, L))
        big_m = jnp.maximum(m, cm)
        inter = jnp.exp(m - big_m)
        emt = jnp.exp(-(b + big_m))
        w = jnp.exp(r - rmax)
        m_loc = g + rmax
        m_new = jnp.maximum(g + m, m_loc)
        decay = jnp.exp(g + m - m_new)
        scale = jnp.exp(m_loc - m_new)
        m = m_new
        gates.append((r, w, decay, scale, big_m, inter, emt))
    m_scr[...] = m

    for rows, (r, w, decay, scale, big_m, inter, emt) in zip(chunk_rows, gates):
        for p in range(M_HEADS // 2):
            pc = slice(p * LANES, (p + 1) * LANES)
            q_pair_b = qk_ref[rows, pc]
            k_pair_b = qk_ref[rows, M_QK_W + p * LANES:M_QK_W + (p + 1) * LANES]
            c_prev = state[p]
            c_prev_b = c_prev.astype(BF16)
            inter_cols = jnp.where(top, row_bcast(inter, 2 * p), row_bcast(inter, 2 * p + 1)).T
            qs_pair = (q_pair_b.astype(F32) * inter_cols).astype(BF16)
            k_both = jnp.concatenate([jnp.where(lo_b, k_pair_b, zero_b), jnp.where(lo_b, zero_b, k_pair_b)], axis=0)
            s_pair = _dot_nt(q_pair_b, k_both)
            v_exts = []
            for hh in range(2):
                hd = 2 * p + hh
                cols = slice(hd * M_V_DIM, (hd + 1) * M_V_DIM)
                sel = lo_b if hh == 0 else jnp.logical_not(lo_b)
                v_ext = jnp.concatenate([mv_ref[rows, cols], ones_v], axis=1)
                v_exts.append(v_ext)
                m_col, emt_col = (row_bcast(x8, hd).T for x8 in (big_m, emt))

                d_mat = jnp.exp(jnp.where(causal, row_bcast(r, hd), -jnp.inf) - m_col)
                s_mat = (s_pair[:, hh * L:(hh + 1) * L] * d_mat).astype(BF16)
                qs = jnp.where(sel, qs_pair, zero_b)
                nd = _dot(jnp.concatenate([s_mat, qs], axis=1), jnp.concatenate([v_ext, c_prev_b], axis=0))
                hcell = nd[:, :M_V_DIM] / jnp.maximum(jnp.abs(nd[:, M_V_DIM:]), emt_col)
                cell = _rms(hcell) * hn_ref[:, cols]
                out_ref[rows, cols] = (so_ref[rows, cols].astype(F32) * cell).astype(BF16)

            k_t = k_pair_b.astype(F32).T
            w_s = jnp.where(top, row_bcast(w, 2 * p), row_bcast(w, 2 * p + 1))
            kw_t = (k_t * w_s).astype(BF16)
            lhs = jnp.concatenate([jnp.where(top, kw_t, zero_b), jnp.where(top, zero_b, kw_t)], axis=1)
            a = _dot(lhs, jnp.concatenate(v_exts, axis=0))
            dec = jnp.where(top2, row_bcast(decay, 2 * p, L)[:, :1], row_bcast(decay, 2 * p + 1, L)[:, :1])
            sc = jnp.where(top2, row_bcast(scale, 2 * p, L)[:, :1], row_bcast(scale, 2 * p + 1, L)[:, :1])
            state[p] = dec * c_prev + sc * a

    for i, x in enumerate(next_gates):
        gate_scr[i] = x

    for w_ref, wb_ref in zip(w_refs, wb_refs):
        wb_ref[...] = w_ref[...].astype(BF16)


def _mlstm(qk, mv, so, grow, brow, hn, weights, batch, seq):
    tl = MLSTM_TILE
    nt = seq // tl
    steps = batch * nt
    assert all(w.shape[0] % (steps * 2 * SUBLANES) == 0 for w in weights)
    cur = lambda b, t: (b * nt + t, 0)
    const = lambda b, t: (0, 0)
    w_specs = [pl.BlockSpec((w.shape[0] // steps, w.shape[1]), cur) for w in weights]
    outs = pl.pallas_call(
        _mlstm_kernel,
        grid=(batch, nt),
        in_specs=[
            pl.BlockSpec((tl, 2 * M_QK_W), cur),
            pl.BlockSpec((tl, M_V_W), cur),
            pl.BlockSpec((tl, M_V_W), cur),
            pl.BlockSpec((2 * M_HEADS, tl), lambda b, t: (0, b * nt + t)),
            pl.BlockSpec((2 * M_HEADS, tl), lambda b, t: (0, b * nt + jnp.minimum(t + 1, nt - 1))),
            pl.BlockSpec(brow.shape, const),
            pl.BlockSpec(hn.shape, const),
        ] + w_specs,
        out_specs=[pl.BlockSpec((tl, M_V_W), cur)] + w_specs,
        out_shape=[jax.ShapeDtypeStruct(mv.shape, BF16)] + [jax.ShapeDtypeStruct(w.shape, BF16) for w in weights],
        scratch_shapes=[
            pltpu.VMEM((M_HEADS // 2, 2 * M_QK_DIM, 2 * M_V_DIM), F32),
            pltpu.VMEM((2 * M_HEADS, M_CHUNK), F32),
            pltpu.VMEM((3, 2 * M_HEADS, tl), F32),
        ],
        compiler_params=pltpu.CompilerParams(dimension_semantics=("parallel", "arbitrary"), vmem_limit_bytes=VMEM_LIMIT),
        name="mlstm",
    )(qk, mv, so, grow, grow, brow, hn, *weights)
    return outs[0], outs[1:]


def _ff_chunks():
    return [(c0, min(c0 + FF_CHUNK, D_FF)) for c0 in range(0, D_FF, FF_CHUNK)]


def _mix_ffn_kernel(x_ref, h_ref, a_ref, m_ref, gpost_ref, gpre_ref, gffn_ref, wg_ref, wa_ref, wm_ref, wo_ref,
                    wfi_ref, wfo_ref, o_ref, acc_ref):
    x = x_ref[...]
    h = h_ref[...]
    a = a_ref[...]
    m = m_ref[...]
    parts = []
    for c0 in range(0, D_MODEL, MERGE_CHUNK):
        cols = slice(c0, c0 + MERGE_CHUNK)
        gcols = slice(D_MODEL + c0, D_MODEL + c0 + MERGE_CHUNK)
        g_attn = jax.nn.sigmoid(_dot(h, wg_ref[:, cols]))
        g_mlstm = jax.nn.sigmoid(_dot(h, wg_ref[:, gcols]))
        parts.append((g_attn * _dot(a, wa_ref[:, cols]) + g_mlstm * _dot(m, wm_ref[:, cols])).astype(BF16))
    y = _dot(jnp.concatenate(parts, axis=1), wo_ref[...])
    x1 = x + _rms(y) * gpost_ref[...]

    h2 = (_rms(x1) * gpre_ref[...]).astype(BF16)
    for c0, c1 in _ff_chunks():
        gate = _dot(h2, wfi_ref[:, c0:c1])
        up = _dot(h2, wfi_ref[:, D_FF + c0:D_FF + c1])
        act = (gate * jax.nn.sigmoid(gate) * up).astype(BF16)
        part = _dot(act, wfo_ref[c0:c1, :])
        if c0 == 0:
            acc_ref[...] = part
        else:
            acc_ref[...] += part
    o_ref[...] = x1 + _rms(acc_ref[...]) * gffn_ref[...]


def _mix_ffn(x2, h, attn, ml, gpost, gpre, gffn, wg, wa, wm, wo, wfi, wfo):
    T = x2.shape[0]
    tm = TOK_TILE
    row = lambda i: (i, 0)
    const = lambda i: (0, 0)

    def wspec(w):
        return pl.BlockSpec(w.shape, const, pipeline_mode=pl.Buffered(1))

    gain = pl.BlockSpec((1, D_MODEL), const)
    return pl.pallas_call(
        _mix_ffn_kernel,
        grid=(T // tm,),
        in_specs=[
            pl.BlockSpec((tm, D_MODEL), row),
            pl.BlockSpec((tm, D_MODEL), row),
            pl.BlockSpec((tm, ATTN_Q_W), row),
            pl.BlockSpec((tm, M_V_W), row),
            gain, gain, gain,
            wspec(wg), wspec(wa), wspec(wm), wspec(wo), wspec(wfi), wspec(wfo),
        ],
        out_specs=pl.BlockSpec((tm, D_MODEL), row),
        out_shape=jax.ShapeDtypeStruct(x2.shape, F32),
        scratch_shapes=[pltpu.VMEM((tm, D_MODEL), F32)],
        compiler_params=pltpu.CompilerParams(dimension_semantics=("parallel",), vmem_limit_bytes=VMEM_LIMIT),
        name="mix_ffn",
    )(x2, h, attn, ml, gpost, gpre, gffn, wg, wa, wm, wo, wfi, wfo)


def _rope_tables(seq):
    f32 = np.float32
    inv_freq = (f32(ROPE_THETA) ** (-np.arange(0, HEAD_DIM, 2, dtype=f32) / f32(HEAD_DIM))).astype(f32)
    ang = np.arange(seq).astype(f32)[:, None] * inv_freq[None, :]
    emb = np.concatenate([ang, ang], axis=-1)
    cos = np.cos(emb).astype(f32)
    sin = np.sin(emb).astype(f32)
    first_half = np.arange(HEAD_DIM) < HEAD_DIM // 2
    sin_a = np.where(first_half, -sin, f32(0))
    sin_b = np.where(first_half, f32(0), sin)
    rep = LANES // HEAD_DIM
    return tuple(jnp.asarray(np.tile(t, (1, rep))) for t in (cos, sin_a, sin_b))


def _layer(x2, batch, seq, norm_pre_mix, norm_post_mix, norm_pre_ffn, norm_post_ffn, w_in, attn_sinks, conv_w,
           conv_b, b_igate, b_fgate, mlstm_head_norm, w_attn_branch, w_mlstm_branch, w_out, w_ffn_in, w_ffn_out):
    w_in_t = jnp.transpose(w_in)
    cos, sa, sb = _rope_tables(seq)

    h, q, k, v, qk, mv, so, grow = _inproj(
        x2, norm_pre_mix[None, :], cos, sa, sb, conv_w, conv_b[None, :], w_in_t, O_GATES + LANES, seq)
    attn, wg = _attention(attn_sinks, q, k, v, w_in_t, O_BRANCH, 2 * D_MODEL, batch, seq)
    bias = jnp.concatenate([b_igate, b_fgate])
    ml, later_weights = _mlstm(qk, mv, so, grow, bias[:, None], mlstm_head_norm[None, :],
                               (w_attn_branch, w_mlstm_branch, w_out, w_ffn_in, w_ffn_out), batch, seq)
    return _mix_ffn(x2, h, attn, ml, norm_post_mix[None, :], norm_pre_ffn[None, :],
                    norm_post_ffn[None, :], wg, *later_weights)


def kernel(x, norm_pre_mix, norm_post_mix, norm_pre_ffn, norm_post_ffn, w_in, attn_sinks, conv_w, conv_b, b_igate,
           b_fgate, mlstm_head_norm, w_attn_branch, w_mlstm_branch, w_out, w_ffn_in, w_ffn_out):
    B, S, D = x.shape
    assert D == D_MODEL and WINDOW == ATTN_BLOCK == M_CHUNK
    assert S % max(INPROJ_TILE, ATTN_TILE, MLSTM_TILE) == 0 and (B * S) % TOK_TILE == 0
    x2 = x.reshape(B * S, D)
    for l in range(w_in.shape[0]):
        x2 = _layer(x2, B, S, norm_pre_mix[l], norm_post_mix[l], norm_pre_ffn[l], norm_post_ffn[l], w_in[l],
                    attn_sinks[l], conv_w[l], conv_b[l], b_igate[l], b_fgate[l], mlstm_head_norm[l],
                    w_attn_branch[l], w_mlstm_branch[l], w_out[l], w_ffn_in[l], w_ffn_out[l])
    return x2.reshape(B, S, D)
```
